```python
import math, functools
import jax, jax.numpy as jnp
from jax import lax
import numpy as np

D_MODEL = 1024
BATCH = 8
SEQ = 2048
DEPTH = 4
DEC_BATCH = 128
DEC_SEQ = 4
PAST_LEN = 8192
PAGE_SIZE = 128

ATTN_WIDTH = D_MODEL // 2
HEAD_DIM = 64
N_HEADS = ATTN_WIDTH // HEAD_DIM
N_KV_HEADS = 2
GROUP = N_HEADS // N_KV_HEADS
KV_WIDTH = N_KV_HEADS * HEAD_DIM
WINDOW = 128
POOL_WIDTH = D_MODEL - ATTN_WIDTH
POOL_WINDOWS = (2, 4, 8, 16)
N_POOL_GROUPS = len(POOL_WINDOWS)
POOL_GROUP_WIDTH = POOL_WIDTH // N_POOL_GROUPS
POOL_STATE = max(POOL_WINDOWS) - 1
IN_WIDTH = ATTN_WIDTH + 2 * KV_WIDTH + POOL_WIDTH
MIX_WIDTH = ATTN_WIDTH + POOL_WIDTH
D_FF = ((8 * D_MODEL + 3 * 256 - 1) // (3 * 256)) * 256
PLE_DIM = 256
EPS = 1e-6

kernel_name = "hymba_pool_swa_sink_decoder_step"


def _rms(x, g):
    xf = x.astype(jnp.float32)
    y = xf * lax.rsqrt(jnp.mean(xf * xf, axis=-1, keepdims=True) + EPS)
    return (y * g.astype(jnp.float32)).astype(x.dtype)


def _sink_attention(q, k, v, mask, sinks):
    s = jnp.einsum('bnqhgd,bnkhd->bnhgqk', q, k).astype(jnp.float32) * (HEAD_DIM ** -0.5)
    s = jnp.where(mask[None, :, None, None], s, -jnp.inf)
    sink = sinks.astype(jnp.float32)[None, None, :, :, None, None]
    m = jnp.maximum(jnp.max(s, axis=-1, keepdims=True), sink)
    p = jnp.exp(s - m)
    denom = jnp.sum(p, axis=-1, keepdims=True) + jnp.exp(sink - m)
    w = (p / denom).astype(v.dtype)
    return jnp.einsum('bnhgqk,bnkhd->bnqhgd', w, v)


def _swa_prompt(q, k, v, sinks):
    B, S, _ = q.shape
    nb = S // WINDOW
    qb = q.reshape(B, nb, WINDOW, N_KV_HEADS, GROUP, HEAD_DIM)
    k = k.reshape(B, S, N_KV_HEADS, HEAD_DIM)
    v = v.reshape(B, S, N_KV_HEADS, HEAD_DIM)
    pad = jnp.zeros((B, WINDOW, N_KV_HEADS, HEAD_DIM), k.dtype)
    kp = jnp.concatenate([pad, k], axis=1)[:, :S]
    vp = jnp.concatenate([pad, v], axis=1)[:, :S]
    kb = jnp.concatenate([kp.reshape(B, nb, WINDOW, N_KV_HEADS, HEAD_DIM),
                          k.reshape(B, nb, WINDOW, N_KV_HEADS, HEAD_DIM)], axis=2)
    vb = jnp.concatenate([vp.reshape(B, nb, WINDOW, N_KV_HEADS, HEAD_DIM),
                          v.reshape(B, nb, WINDOW, N_KV_HEADS, HEAD_DIM)], axis=2)
    a = jnp.arange(WINDOW)[:, None]
    c = jnp.arange(2 * WINDOW)[None, :]
    d = a + WINDOW - c
    blk = jnp.arange(nb)[:, None, None]
    mask = (d >= 0)[None] & (d < WINDOW)[None] & (blk * WINDOW + c[None] - WINDOW >= 0)
    o = _sink_attention(qb, kb, vb, mask, sinks).reshape(B, S, ATTN_WIDTH)
    return o, k[:, -WINDOW:], v[:, -WINDOW:]


def _swa_sample(q, k, v, k_buf, v_buf, sinks):
    B, S, _ = q.shape
    qb = q.reshape(B, 1, S, N_KV_HEADS, GROUP, HEAD_DIM)
    k_ext = jnp.concatenate([k_buf, k.reshape(B, S, N_KV_HEADS, HEAD_DIM)], axis=1)
    v_ext = jnp.concatenate([v_buf, v.reshape(B, S, N_KV_HEADS, HEAD_DIM)], axis=1)
    j = jnp.arange(S)[:, None]
    c = jnp.arange(WINDOW + S)[None, :]
    d = j + WINDOW - c
    mask = ((d >= 0) & (d < WINDOW))[None]
    o = _sink_attention(qb, k_ext[:, None], v_ext[:, None], mask, sinks).reshape(B, S, ATTN_WIDTH)
    return o, k_ext[:, -WINDOW:], v_ext[:, -WINDOW:]


def _pool_mixer(u, prev, pos0, w_pool, pool_scale):
    B, S, C = u.shape
    P = prev.shape[1]
    ext = jnp.concatenate([prev, u], axis=1)
    cs = jnp.concatenate([jnp.zeros((B, 1, C), jnp.float32),
                          jnp.cumsum(ext.astype(jnp.float32), axis=1)], axis=1)
    pos = pos0 + jnp.arange(S)
    parts = []
    for g, w in enumerate(POOL_WINDOWS):
        sl = slice(g * POOL_GROUP_WIDTH, (g + 1) * POOL_GROUP_WIDTH)
        tot = cs[:, P + 1:P + 1 + S, sl] - cs[:, P + 1 - w:P + 1 - w + S, sl]
        cnt = jnp.minimum(w, pos + 1).astype(jnp.float32)
        parts.append(tot / cnt[None, :, None])
    mean = jnp.concatenate(parts, axis=-1)
    dlt = (mean - u.astype(jnp.float32)).astype(u.dtype).reshape(B, S, N_POOL_GROUPS, POOL_GROUP_WIDTH)
    y = jnp.einsum('bsgc,gcd->bsgd', dlt, w_pool).reshape(B, S, C) * pool_scale
    return y, ext[:, -POOL_STATE:]


def _layer(x, p, attn_fn, pool_prev, pos0, n_mix_pre, n_mix_post, n_ffn_pre, n_ffn_post,
           w_in, w_out, w_pool, pool_scale, w_gate, w_up, w_down, w_ple, w_ple_gate):
    h = _rms(x, n_mix_pre)
    z = h @ w_in
    q = z[..., :ATTN_WIDTH]
    k = z[..., ATTN_WIDTH:ATTN_WIDTH + KV_WIDTH]
    v = z[..., ATTN_WIDTH + KV_WIDTH:ATTN_WIDTH + 2 * KV_WIDTH]
    u = z[..., ATTN_WIDTH + 2 * KV_WIDTH:]
    a, k_state, v_state = attn_fn(q, k, v)
    m, pool_state = _pool_mixer(u, pool_prev, pos0, w_pool, pool_scale)
    mix = jnp.concatenate([a, m], axis=-1) @ w_out
    x = x + _rms(mix, n_mix_post)
    f = _rms(x, n_ffn_pre)
    f = (jax.nn.silu(f @ w_gate) * (f @ w_up)) @ w_down
    x = x + _rms(f, n_ffn_post)
    x = x + jax.nn.sigmoid(x @ w_ple_gate) * (p @ w_ple)
    return x, k_state, v_state, pool_state


def setup_inputs(seed: int = 0) -> dict:
    key = jax.random.key(seed)
    ks = jax.random.split(key, 24)
    f32 = jnp.float32

    def nrm(k, shape, scale):
        return jax.random.normal(k, shape, f32) * scale

    def gain(k, shape):
        return 1.0 + 0.1 * jax.random.normal(k, shape, f32)

    return {
        "x_prompt": nrm(ks[0], (BATCH, SEQ, D_MODEL), 1.0),
        "x_sample": nrm(ks[1], (DEC_BATCH, DEC_SEQ, D_MODEL), 1.0),
        "p_prompt": nrm(ks[2], (DEPTH, BATCH, SEQ, PLE_DIM), 1.0),
        "p_sample": nrm(ks[3], (DEPTH, DEC_BATCH, DEC_SEQ, PLE_DIM), 1.0),
        "cache_k": nrm(ks[4], (DEPTH, DEC_BATCH, WINDOW, N_KV_HEADS, HEAD_DIM), 1.0),
        "cache_v": nrm(ks[5], (DEPTH, DEC_BATCH, WINDOW, N_KV_HEADS, HEAD_DIM), 1.0),
        "state_pool": nrm(ks[6], (DEPTH, DEC_BATCH, POOL_STATE, POOL_WIDTH), 1.0),
        "norm_mix_pre": gain(ks[7], (DEPTH, D_MODEL)),
        "norm_mix_post": gain(ks[8], (DEPTH, D_MODEL)),
        "norm_ffn_pre": gain(ks[9], (DEPTH, D_MODEL)),
        "norm_ffn_post": gain(ks[10], (DEPTH, D_MODEL)),
        "w_in": nrm(ks[11], (DEPTH, D_MODEL, IN_WIDTH), D_MODEL ** -0.5),
        "w_out": nrm(ks[12], (DEPTH, MIX_WIDTH, D_MODEL), MIX_WIDTH ** -0.5),
        "attn_sinks": nrm(ks[13], (DEPTH, N_HEADS), 0.5),
        "w_pool": nrm(ks[14], (DEPTH, N_POOL_GROUPS, POOL_GROUP_WIDTH, POOL_GROUP_WIDTH), POOL_GROUP_WIDTH ** -0.5),
        "pool_scale": gain(ks[15], (DEPTH, POOL_WIDTH)),
        "w_gate": nrm(ks[16], (DEPTH, D_MODEL, D_FF), D_MODEL ** -0.5),
        "w_up": nrm(ks[17], (DEPTH, D_MODEL, D_FF), D_MODEL ** -0.5),
        "w_down": nrm(ks[18], (DEPTH, D_FF, D_MODEL), D_FF ** -0.5),
        "w_ple": nrm(ks[19], (DEPTH, PLE_DIM, D_MODEL), PLE_DIM ** -0.5),
        "w_ple_gate": nrm(ks[20], (DEPTH, D_MODEL, D_MODEL), D_MODEL ** -0.5),
    }


def reference(x_prompt, x_sample, p_prompt, p_sample, cache_k, cache_v, state_pool,
              norm_mix_pre, norm_mix_post, norm_ffn_pre, norm_ffn_post, w_in, w_out,
              attn_sinks, w_pool, pool_scale, w_gate, w_up, w_down, w_ple, w_ple_gate):
    yp = x_prompt
    ys = x_sample
    kp_l, vp_l, sp_l, ks_l, vs_l, ss_l = [], [], [], [], [], []
    pool_zero = jnp.zeros((x_prompt.shape[0], POOL_STATE, POOL_WIDTH), x_prompt.dtype)
    for i in range(DEPTH):
        lw = (norm_mix_pre[i], norm_mix_post[i], norm_ffn_pre[i], norm_ffn_post[i],
              w_in[i], w_out[i], w_pool[i], pool_scale[i], w_gate[i], w_up[i], w_down[i],
              w_ple[i], w_ple_gate[i])
        sinks = attn_sinks[i].reshape(N_KV_HEADS, GROUP)
        yp, kp, vp, sp = _layer(yp, p_prompt[i], functools.partial(_swa_prompt, sinks=sinks),
                                pool_zero, 0, *lw)
        attn_s = functools.partial(_swa_sample, k_buf=cache_k[i], v_buf=cache_v[i], sinks=sinks)
        ys, kss, vss, sss = _layer(ys, p_sample[i], attn_s, state_pool[i], PAST_LEN, *lw)
        kp_l.append(kp); vp_l.append(vp); sp_l.append(sp)
        ks_l.append(kss); vs_l.append(vss); ss_l.append(sss)
    k_prompt = jnp.stack(kp_l)
    v_prompt = jnp.stack(vp_l)
    pool_prompt = jnp.stack(sp_l)
    k_sample = jnp.stack(ks_l)
    v_sample = jnp.stack(vs_l)
    pool_sample = jnp.stack(ss_l)
    return (yp, ys, k_prompt, v_prompt, pool_prompt, k_sample, v_sample, pool_sample)
```

```python
import functools

import jax
import jax.numpy as jnp
from jax import lax
from jax.experimental import pallas as pl
from jax.experimental.pallas import tpu as pltpu

D_MODEL = 1024
DEPTH = 4
ATTN_WIDTH = 512
HEAD_DIM = 64
N_HEADS = 8
N_KV_HEADS = 2
GROUP = 4
KV_WIDTH = 128
WINDOW = 128
POOL_WIDTH = 512
POOL_WINDOWS = (2, 4, 8, 16)
POOL_GROUP_WIDTH = 128
POOL_STATE = 15
IN_WIDTH = 1280
D_FF = 2816
PLE_DIM = 256
EPS = 1e-6

K_OFF = ATTN_WIDTH
V_OFF = ATTN_WIDTH + KV_WIDTH
U_OFF = ATTN_WIDTH + 2 * KV_WIDTH
LANES = 128
POOL_PAD = 16

BF16 = jnp.bfloat16
F32 = jnp.float32
VMEM_LIMIT = 56 * 1024 * 1024


def _rms(x, g):
    return x * lax.rsqrt(jnp.mean(x * x, axis=-1, keepdims=True) + EPS) * g


def _sigmoid(x):
    return 1.0 / (1.0 + jnp.exp(-x))


def _dot(a, b):
    return jnp.dot(a, b, preferred_element_type=F32)


def _dot_t(a, b):
    return lax.dot_general(a, b, (((1,), (1,)), ((), ())), preferred_element_type=F32)


def _const_spec(shape, layer=None):
    if layer is None:
        return pl.BlockSpec(shape, lambda *_: (0,) * len(shape), pipeline_mode=pl.Buffered(1))
    return pl.BlockSpec((None,) + shape, lambda *_: (layer,) + (0,) * len(shape),
                        pipeline_mode=pl.Buffered(1))


def _mix_prompt_kernel(sinks_ref, x_ref, gpre_ref, gpost_ref, win_ref, wout_ref, wpool_ref, pscale_ref,
                       y_ref, klast_ref, vlast_ref, plast_ref,
                       z_ref, kprev_ref, vprev_ref, uext_ref, mix_ref, *, tm, layer):
    s = pl.program_id(1)
    nblk = tm // WINDOW

    @pl.when(s == 0)
    def _():
        kprev_ref[...] = jnp.zeros_like(kprev_ref)
        vprev_ref[...] = jnp.zeros_like(vprev_ref)
        uext_ref[0:POOL_PAD, :] = jnp.zeros((POOL_PAD, POOL_WIDTH), F32)

    x = x_ref[0]
    h = _rms(x, gpre_ref[...]).astype(BF16)
    z_ref[...] = _dot(h, win_ref[...])

    lane = lax.broadcasted_iota(jnp.int32, (WINDOW, LANES), 1)
    lo = lane < HEAD_DIM
    qrow = lax.broadcasted_iota(jnp.int32, (GROUP * WINDOW, 2 * WINDOW), 0) & (WINDOW - 1)
    kcol = lax.broadcasted_iota(jnp.int32, (GROUP * WINDOW, 2 * WINDOW), 1)
    dist = qrow + WINDOW - kcol
    band = (dist >= 0) & (dist < WINDOW)
    first_col = jnp.where(s == 0, WINDOW, 0)

    for n in range(nblk):
        r0 = n * WINDOW
        rows = slice(r0, r0 + WINDOW)
        if n == 0:
            kp, vp = kprev_ref[...], vprev_ref[...]
            mask = band & (kcol >= first_col)
        else:
            kp = z_ref[r0 - WINDOW:r0, K_OFF:K_OFF + KV_WIDTH]
            vp = z_ref[r0 - WINDOW:r0, V_OFF:V_OFF + KV_WIDTH]
            mask = band
        kcat = jnp.concatenate([kp, z_ref[rows, K_OFF:K_OFF + KV_WIDTH]], axis=0)
        vcat = jnp.concatenate([vp, z_ref[rows, V_OFF:V_OFF + KV_WIDTH]], axis=0)
        kswap = pltpu.roll(kcat, HEAD_DIM, 1)
        vswap = pltpu.roll(vcat, HEAD_DIM, 1)
        lo2 = jnp.concatenate([lo, lo], axis=0)
        for kv in range(N_KV_HEADS):
            if kv == 0:
                kk = jnp.where(lo2, kcat, kswap).astype(BF16)
                vv = jnp.where(lo2, vcat, vswap).astype(BF16)
            else:
                kk = jnp.where(lo2, kswap, kcat).astype(BF16)
                vv = jnp.where(lo2, vswap, vcat).astype(BF16)
            pieces, sinks = [], []
            for g in range(GROUP):
                hd = kv * GROUP + g
                slab = hd // 2
                qs = z_ref[rows, slab * LANES:(slab + 1) * LANES] * (HEAD_DIM ** -0.5)
                keep = lo if hd % 2 == 0 else jnp.logical_not(lo)
                pieces.append(jnp.where(keep, qs, 0.0).astype(BF16))
                sinks.append(jnp.full((WINDOW, 1), sinks_ref[layer, hd], F32))
            q4 = jnp.concatenate(pieces, axis=0)
            sink = jnp.concatenate(sinks, axis=0)
            sc = _dot_t(q4, kk)
            sc = jnp.where(mask, sc, -jnp.inf)
            m = jnp.maximum(jnp.max(sc, axis=-1, keepdims=True), sink)
            p = jnp.exp(sc - m)
            denom = jnp.sum(p, axis=-1, keepdims=True) + jnp.exp(sink - m)
            o = _dot(p.astype(BF16), vv) / denom
            for j in range(GROUP // 2):
                even = o[(2 * j) * WINDOW:(2 * j + 1) * WINDOW]
                odd = o[(2 * j + 1) * WINDOW:(2 * j + 2) * WINDOW]
                slab = kv * (GROUP // 2) + j
                mix_ref[rows, slab * LANES:(slab + 1) * LANES] = jnp.where(lo, even, odd).astype(BF16)

    uext_ref[POOL_PAD:POOL_PAD + tm, :] = z_ref[:, U_OFF:U_OFF + POOL_WIDTH]
    pos = lax.broadcasted_iota(jnp.int32, (tm, 1), 0) + s * tm
    for gi, w in enumerate(POOL_WINDOWS):
        cols = slice(gi * POOL_GROUP_WIDTH, (gi + 1) * POOL_GROUP_WIDTH)
        cur = uext_ref[POOL_PAD:POOL_PAD + tm, cols]
        tot = cur
        for j in range(1, w):
            tot = tot + uext_ref[POOL_PAD - j:POOL_PAD - j + tm, cols]
        cnt = jnp.minimum(w, pos + 1).astype(F32)
        dlt = (tot / cnt - cur).astype(BF16)
        yp = _dot(dlt, wpool_ref[gi]) * pscale_ref[:, cols]
        mix_ref[:, ATTN_WIDTH + gi * POOL_GROUP_WIDTH:ATTN_WIDTH + (gi + 1) * POOL_GROUP_WIDTH] = yp.astype(BF16)

    mixed = _dot(mix_ref[...], wout_ref[...])
    y_ref[0] = x + _rms(mixed, gpost_ref[...])

    kprev_ref[...] = z_ref[tm - WINDOW:tm, K_OFF:K_OFF + KV_WIDTH]
    vprev_ref[...] = z_ref[tm - WINDOW:tm, V_OFF:V_OFF + KV_WIDTH]
    uext_ref[0:POOL_PAD, :] = uext_ref[tm:tm + POOL_PAD, :]

    @pl.when(s == pl.num_programs(1) - 1)
    def _():
        klast_ref[0] = z_ref[tm - WINDOW:tm, K_OFF:K_OFF + KV_WIDTH]
        vlast_ref[0] = z_ref[tm - WINDOW:tm, V_OFF:V_OFF + KV_WIDTH]
        plast_ref[0] = uext_ref[tm:tm + POOL_PAD, :]


def _mix_prompt(x, sinks, gpre, gpost, win, wout, wpool, pscale, *, layer, tm):
    b, s, _ = x.shape
    kern = functools.partial(_mix_prompt_kernel, tm=tm, layer=layer)
    return pl.pallas_call(
        kern,
        grid=(b, s // tm),
        in_specs=[
            pl.BlockSpec(memory_space=pltpu.SMEM),
            pl.BlockSpec((1, tm, D_MODEL), lambda i, j: (i, j, 0)),
            _const_spec((1, D_MODEL), layer),
            _const_spec((1, D_MODEL), layer),
            _const_spec((D_MODEL, IN_WIDTH), layer),
            _const_spec((D_MODEL, D_MODEL), layer),
            _const_spec((len(POOL_WINDOWS), POOL_GROUP_WIDTH, POOL_GROUP_WIDTH), layer),
            _const_spec((1, POOL_WIDTH), layer),
        ],
        out_specs=[
            pl.BlockSpec((1, tm, D_MODEL), lambda i, j: (i, j, 0)),
            pl.BlockSpec((1, WINDOW, KV_WIDTH), lambda i, j: (i, 0, 0)),
            pl.BlockSpec((1, WINDOW, KV_WIDTH), lambda i, j: (i, 0, 0)),
            pl.BlockSpec((1, POOL_PAD, POOL_WIDTH), lambda i, j: (i, 0, 0)),
        ],
        out_shape=[
            jax.ShapeDtypeStruct((b, s, D_MODEL), F32),
            jax.ShapeDtypeStruct((b, WINDOW, KV_WIDTH), F32),
            jax.ShapeDtypeStruct((b, WINDOW, KV_WIDTH), F32),
            jax.ShapeDtypeStruct((b, POOL_PAD, POOL_WIDTH), F32),
        ],
        scratch_shapes=[
            pltpu.VMEM((tm, IN_WIDTH), F32),
            pltpu.VMEM((WINDOW, KV_WIDTH), F32),
            pltpu.VMEM((WINDOW, KV_WIDTH), F32),
            pltpu.VMEM((POOL_PAD + tm, POOL_WIDTH), F32),
            pltpu.VMEM((tm, D_MODEL), BF16),
        ],
        compiler_params=pltpu.CompilerParams(
            dimension_semantics=("arbitrary", "arbitrary"), vmem_limit_bytes=VMEM_LIMIT),
        name=f"mix_prompt_{layer}",
    )(sinks, x, gpre, gpost, win, wout, wpool, pscale)


def _ffn_kernel(x_ref, p_ref, gpre_ref, gpost_ref, wg_ref, wu_ref, wd_ref, wple_ref, wpg_ref, o_ref):
    x = x_ref[...]
    f = _rms(x, gpre_ref[...]).astype(BF16)
    g = _dot(f, wg_ref[...])
    u = _dot(f, wu_ref[...])
    a = (g * _sigmoid(g) * u).astype(BF16)
    d = _dot(a, wd_ref[...])
    x = x + _rms(d, gpost_ref[...])
    gate = _sigmoid(_dot(x.astype(BF16), wpg_ref[...]))
    o_ref[...] = x + gate * _dot(p_ref[...].astype(BF16), wple_ref[...])


def _ffn(x, p, gpre, gpost, wg, wu, wd, wple, wpg, *, layer, tm, tag):
    n = x.shape[0]
    return pl.pallas_call(
        _ffn_kernel,
        grid=(n // tm,),
        in_specs=[
            pl.BlockSpec((tm, D_MODEL), lambda i: (i, 0)),
            pl.BlockSpec((None, tm, PLE_DIM), lambda i: (layer, i, 0)),
            _const_spec((1, D_MODEL), layer),
            _const_spec((1, D_MODEL), layer),
            _const_spec((D_MODEL, D_FF), layer),
            _const_spec((D_MODEL, D_FF), layer),
            _const_spec((D_FF, D_MODEL), layer),
            _const_spec((PLE_DIM, D_MODEL), layer),
            _const_spec((D_MODEL, D_MODEL), layer),
        ],
        out_specs=pl.BlockSpec((tm, D_MODEL), lambda i: (i, 0)),
        out_shape=jax.ShapeDtypeStruct((n, D_MODEL), F32),
        compiler_params=pltpu.CompilerParams(
            dimension_semantics=("arbitrary",), vmem_limit_bytes=VMEM_LIMIT),
        name=f"ffn_{tag}_{layer}",
    )(x, p, gpre, gpost, wg, wu, wd, wple, wpg)


def _inproj_kernel(x_ref, gpre_ref, win_ref, z_ref):
    h = _rms(x_ref[...], gpre_ref[...]).astype(BF16)
    z_ref[...] = _dot(h, win_ref[...])


def _inproj(x, gpre, win, *, layer):
    n = x.shape[0]
    return pl.pallas_call(
        _inproj_kernel,
        grid=(1,),
        in_specs=[
            pl.BlockSpec((n, D_MODEL), lambda i: (0, 0)),
            _const_spec((1, D_MODEL), layer),
            _const_spec((D_MODEL, IN_WIDTH), layer),
        ],
        out_specs=pl.BlockSpec((n, IN_WIDTH), lambda i: (0, 0)),
        out_shape=jax.ShapeDtypeStruct((n, IN_WIDTH), F32),
        compiler_params=pltpu.CompilerParams(
            dimension_semantics=("arbitrary",), vmem_limit_bytes=VMEM_LIMIT),
        name=f"inproj_sample_{layer}",
    )(x, gpre, win)


def _outproj_kernel(x_ref, mix_ref, gpost_ref, wout_ref, y_ref):
    mixed = _dot(mix_ref[...].astype(BF16), wout_ref[...])
    y_ref[...] = x_ref[...] + _rms(mixed, gpost_ref[...])


def _outproj(x, mix, gpost, wout, *, layer):
    n = x.shape[0]
    return pl.pallas_call(
        _outproj_kernel,
        grid=(1,),
        in_specs=[
            pl.BlockSpec((n, D_MODEL), lambda i: (0, 0)),
            pl.BlockSpec((n, D_MODEL), lambda i: (0, 0)),
            _const_spec((1, D_MODEL), layer),
            _const_spec((D_MODEL, D_MODEL), layer),
        ],
        out_specs=pl.BlockSpec((n, D_MODEL), lambda i: (0, 0)),
        out_shape=jax.ShapeDtypeStruct((n, D_MODEL), F32),
        compiler_params=pltpu.CompilerParams(
            dimension_semantics=("arbitrary",), vmem_limit_bytes=VMEM_LIMIT),
        name=f"outproj_sample_{layer}",
    )(x, mix, gpost, wout)


def _mix_sample_kernel(sinks_ref, z_ref, ck_ref, cv_ref, st_ref, wpool_ref, pscale_ref,
                       mix_ref, ko_ref, vo_ref, po_ref, ext_ref, dlt_ref, *, bb, dec, layer):
    per_tile = 8 // dec
    lane = lax.broadcasted_iota(jnp.int32, (8, LANES), 1)
    lo = lane < HEAD_DIM
    hi = jnp.logical_not(lo)
    rows64 = N_HEADS * 8
    trow = lax.broadcasted_iota(jnp.int32, (rows64, LANES), 0) & 7
    tstep = trow % dec
    tbat = trow // dec
    klane = lax.broadcasted_iota(jnp.int32, (rows64, LANES), 1)
    valid_c = klane >= tstep + 1
    tstep1 = tstep[:, 0:1]
    tbat1 = tbat[:, 0:1]

    for t in range(bb // per_tile):
        rows = slice(8 * t, 8 * t + 8)
        q = z_ref[rows, 0:ATTN_WIDTH] * (HEAD_DIM ** -0.5)
        knew = z_ref[rows, K_OFF:K_OFF + KV_WIDTH]
        vnew = z_ref[rows, V_OFF:V_OFF + KV_WIDTH]
        pieces, sinks = [], []
        for hd in range(N_HEADS):
            slab, half, kv = hd // 2, hd % 2, hd // GROUP
            qs = q[:, slab * LANES:(slab + 1) * LANES]
            src = qs if half == kv else pltpu.roll(qs, HEAD_DIM, 1)
            pieces.append(jnp.where(lo if kv == 0 else hi, src, 0.0))
            sinks.append(jnp.full((8, 1), sinks_ref[layer, hd], F32))
        lhs = jnp.concatenate(pieces, axis=0)
        sink = jnp.concatenate(sinks, axis=0)
        lhs_b = lhs.astype(BF16)

        sc = None
        for i in range(per_tile):
            kb = ck_ref[per_tile * t + i].astype(BF16)
            si = _dot_t(lhs_b, kb)
            sc = si if sc is None else jnp.where(tbat == i, si, sc)
        sc = jnp.where(valid_c, sc, -jnp.inf)

        sn, vn_rows = [], []
        for i in range(dec):
            kr, vr = None, None
            for bi in range(per_tile):
                kbrow = jnp.broadcast_to(knew[bi * dec + i:bi * dec + i + 1, :], (rows64, LANES))
                vbrow = jnp.broadcast_to(vnew[bi * dec + i:bi * dec + i + 1, :], (rows64, LANES))
                kr = kbrow if kr is None else jnp.where(tbat == bi, kbrow, kr)
                vr = vbrow if vr is None else jnp.where(tbat == bi, vbrow, vr)
            s_i = jnp.sum(lhs * kr, axis=-1, keepdims=True)
            sn.append(jnp.where(tstep1 >= i, s_i, -jnp.inf))
            vn_rows.append(vr)

        m = jnp.maximum(jnp.max(sc, axis=-1, keepdims=True), sink)
        for s_i in sn:
            m = jnp.maximum(m, s_i)
        p = jnp.exp(sc - m)
        denom = jnp.sum(p, axis=-1, keepdims=True) + jnp.exp(sink - m)
        pb = p.astype(BF16)
        o = None
        for i in range(per_tile):
            vb = cv_ref[per_tile * t + i].astype(BF16)
            oi = _dot(pb, vb)
            o = oi if o is None else jnp.where(tbat == i, oi, o)
        for s_i, vr in zip(sn, vn_rows):
            p_i = jnp.exp(s_i - m)
            denom = denom + p_i
            o = o + p_i * vr
        o = o / denom
        for slab in range(N_HEADS // 2):
            kv = (2 * slab) // GROUP
            even = o[16 * slab:16 * slab + 8]
            odd = o[16 * slab + 8:16 * slab + 16]
            if kv == 0:
                res = jnp.where(lo, even, pltpu.roll(odd, HEAD_DIM, 1))
            else:
                res = jnp.where(lo, pltpu.roll(even, HEAD_DIM, 1), odd)
            mix_ref[rows, slab * LANES:(slab + 1) * LANES] = res

    for b in range(bb):
        ko_ref[b, 0:WINDOW - dec, :] = ck_ref[b, dec:WINDOW, :]
        vo_ref[b, 0:WINDOW - dec, :] = cv_ref[b, dec:WINDOW, :]
        ko_ref[b, WINDOW - dec:WINDOW, :] = z_ref[b * dec:(b + 1) * dec, K_OFF:K_OFF + KV_WIDTH]
        vo_ref[b, WINDOW - dec:WINDOW, :] = z_ref[b * dec:(b + 1) * dec, V_OFF:V_OFF + KV_WIDTH]

    for b in range(bb):
        ext_ref[0:POOL_STATE, :] = st_ref[b]
        ext_ref[POOL_STATE:POOL_STATE + dec, :] = z_ref[b * dec:(b + 1) * dec, U_OFF:U_OFF + POOL_WIDTH]
        for gi, w in enumerate(POOL_WINDOWS):
            cols = slice(gi * POOL_GROUP_WIDTH, (gi + 1) * POOL_GROUP_WIDTH)
            cur = ext_ref[POOL_STATE:POOL_STATE + dec, cols]
            tot = cur
            for j in range(1, w):
                tot = tot + ext_ref[POOL_STATE - j:POOL_STATE - j + dec, cols]
            dlt_ref[b * dec:(b + 1) * dec, cols] = tot / float(w) - cur
        po_ref[b] = ext_ref[dec:dec + POOL_STATE, :]
    for gi in range(len(POOL_WINDOWS)):
        cols = slice(gi * POOL_GROUP_WIDTH, (gi + 1) * POOL_GROUP_WIDTH)
        yp = _dot(dlt_ref[:, cols].astype(BF16), wpool_ref[gi]) * pscale_ref[:, cols]
        mix_ref[:, ATTN_WIDTH + gi * POOL_GROUP_WIDTH:ATTN_WIDTH + (gi + 1) * POOL_GROUP_WIDTH] = yp


def _mix_sample(z, sinks, ck, cv, st, wpool, pscale, *, layer, bb, dec):
    nb = ck.shape[1]
    kern = functools.partial(_mix_sample_kernel, bb=bb, dec=dec, layer=layer)
    return pl.pallas_call(
        kern,
        grid=(nb // bb,),
        in_specs=[
            pl.BlockSpec(memory_space=pltpu.SMEM),
            pl.BlockSpec((bb * dec, IN_WIDTH), lambda i: (i, 0)),
            pl.BlockSpec((None, bb, WINDOW, KV_WIDTH), lambda i: (layer, i, 0, 0)),
            pl.BlockSpec((None, bb, WINDOW, KV_WIDTH), lambda i: (layer, i, 0, 0)),
            pl.BlockSpec((None, bb, POOL_STATE, POOL_WIDTH), lambda i: (layer, i, 0, 0)),
            _const_spec((len(POOL_WINDOWS), POOL_GROUP_WIDTH, POOL_GROUP_WIDTH), layer),
            _const_spec((1, POOL_WIDTH), layer),
        ],
        out_specs=[
            pl.BlockSpec((bb * dec, D_MODEL), lambda i: (i, 0)),
            pl.BlockSpec((bb, WINDOW, KV_WIDTH), lambda i: (i, 0, 0)),
            pl.BlockSpec((bb, WINDOW, KV_WIDTH), lambda i: (i, 0, 0)),
            pl.BlockSpec((bb, POOL_STATE, POOL_WIDTH), lambda i: (i, 0, 0)),
        ],
        out_shape=[
            jax.ShapeDtypeStruct((nb * dec, D_MODEL), F32),
            jax.ShapeDtypeStruct((nb, WINDOW, KV_WIDTH), F32),
            jax.ShapeDtypeStruct((nb, WINDOW, KV_WIDTH), F32),
            jax.ShapeDtypeStruct((nb, POOL_STATE, POOL_WIDTH), F32),
        ],
        scratch_shapes=[
            pltpu.VMEM((POOL_STATE + 9, POOL_WIDTH), F32),
            pltpu.VMEM((bb * dec, POOL_WIDTH), F32),
        ],
        compiler_params=pltpu.CompilerParams(
            dimension_semantics=("arbitrary",), vmem_limit_bytes=VMEM_LIMIT),
        name=f"mix_sample_{layer}",
    )(sinks, z, ck, cv, st, wpool, pscale)


def kernel(x_prompt, x_sample, p_prompt, p_sample, cache_k, cache_v, state_pool, norm_mix_pre, norm_mix_post,
           norm_ffn_pre, norm_ffn_post, w_in, w_out, attn_sinks, w_pool, pool_scale, w_gate, w_up, w_down, w_ple,
           w_ple_gate):
    depth, nbat, seq, _ = p_prompt.shape
    dec_b, dec = x_sample.shape[0], x_sample.shape[1]
    assert 8 % dec == 0 and seq % WINDOW == 0

    win, wout, wpool = w_in.astype(BF16), w_out.astype(BF16), w_pool.astype(BF16)
    wg, wu, wd = w_gate.astype(BF16), w_up.astype(BF16), w_down.astype(BF16)
    wple, wpg = w_ple.astype(BF16), w_ple_gate.astype(BF16)
    g_mix_pre = norm_mix_pre.reshape(depth, 1, D_MODEL)
    g_mix_post = norm_mix_post.reshape(depth, 1, D_MODEL)
    g_ffn_pre = norm_ffn_pre.reshape(depth, 1, D_MODEL)
    g_ffn_post = norm_ffn_post.reshape(depth, 1, D_MODEL)
    pscale = pool_scale.reshape(depth, 1, POOL_WIDTH)

    pp = p_prompt.reshape(depth, nbat * seq, PLE_DIM)
    ps = p_sample.reshape(depth, dec_b * dec, PLE_DIM)
    ck = cache_k.reshape(depth, dec_b, WINDOW, KV_WIDTH)
    cv = cache_v.reshape(depth, dec_b, WINDOW, KV_WIDTH)

    yp = x_prompt
    ys = x_sample.reshape(dec_b * dec, D_MODEL)
    kp_l, vp_l, sp_l, ks_l, vs_l, ss_l = [], [], [], [], [], []
    for i in range(depth):
        yp, kp, vp, sp = _mix_prompt(yp, attn_sinks, g_mix_pre, g_mix_post, win, wout, wpool, pscale,
                                     layer=i, tm=256)
        yp = _ffn(yp.reshape(nbat * seq, D_MODEL), pp, g_ffn_pre, g_ffn_post, wg, wu, wd, wple, wpg,
                  layer=i, tm=256, tag="prompt").reshape(nbat, seq, D_MODEL)
        kp_l.append(kp)
        vp_l.append(vp)
        sp_l.append(sp[:, POOL_PAD - POOL_STATE:])

        z = _inproj(ys, g_mix_pre, win, layer=i)
        mix, kss, vss, sss = _mix_sample(z, attn_sinks, ck, cv, state_pool, wpool, pscale,
                                         layer=i, bb=8, dec=dec)
        ys = _outproj(ys, mix, g_mix_post, wout, layer=i)
        ys = _ffn(ys, ps, g_ffn_pre, g_ffn_post, wg, wu, wd, wple, wpg,
                  layer=i, tm=dec_b * dec, tag="sample")
        ks_l.append(kss)
        vs_l.append(vss)
        ss_l.append(sss)

    kv_shape = (depth, -1, WINDOW, N_KV_HEADS, HEAD_DIM)
    return (yp, ys.reshape(dec_b, dec, D_MODEL),
            jnp.stack(kp_l).reshape(kv_shape), jnp.stack(vp_l).reshape(kv_shape), jnp.stack(sp_l),
            jnp.stack(ks_l).reshape(kv_shape), jnp.stack(vs_l).reshape(kv_shape), jnp.stack(ss_l))
```

```python
import functools

import jax
import jax.numpy as jnp
from jax import lax
from jax.experimental import pallas as pl
from jax.experimental.pallas import tpu as pltpu

D_MODEL = 1024
DEPTH = 4
ATTN_WIDTH = 512
HEAD_DIM = 64
N_HEADS = 8
N_KV_HEADS = 2
GROUP = 4
KV_WIDTH = 128
WINDOW = 128
POOL_WIDTH = 512
POOL_WINDOWS = (2, 4, 8, 16)
POOL_GROUP_WIDTH = 128
POOL_STATE = 15
IN_WIDTH = 1280
D_FF = 2816
PLE_DIM = 256
EPS = 1e-6

K_OFF = ATTN_WIDTH
V_OFF = ATTN_WIDTH + KV_WIDTH
U_OFF = ATTN_WIDTH + 2 * KV_WIDTH
LANES = 128
POOL_PAD = 16

BF16 = jnp.bfloat16
F32 = jnp.float32
VMEM_LIMIT = 56 * 1024 * 1024


def _rms(x, g):
    return x * lax.rsqrt(jnp.mean(x * x, axis=-1, keepdims=True) + EPS) * g


def _sigmoid(x):
    return 1.0 / (1.0 + jnp.exp(-x))


def _dot(a, b):
    return jnp.dot(a, b, preferred_element_type=F32)


def _dot_t(a, b):
    return lax.dot_general(a, b, (((1,), (1,)), ((), ())), preferred_element_type=F32)


def _const_spec(shape, layer=None):
    if layer is None:
        return pl.BlockSpec(shape, lambda *_: (0,) * len(shape), pipeline_mode=pl.Buffered(1))
    return pl.BlockSpec((None,) + shape, lambda *_: (layer,) + (0,) * len(shape),
                        pipeline_mode=pl.Buffered(1))


def _mix_prompt_kernel(sinks_ref, x_ref, gpre_ref, gpost_ref, win_ref, wout_ref, wpool_ref, pscale_ref,
                       y_ref, klast_ref, vlast_ref, plast_ref,
                       z_ref, kext_ref, vt_ref, uext_ref, mix_ref, bias_ref, *, tm, layer):
    s = pl.program_id(1)
    nblk = tm // WINDOW

    @pl.when((pl.program_id(0) == 0) & (s == 0))
    def _():
        c = lax.broadcasted_iota(jnp.int32, (2 * WINDOW, WINDOW), 0)
        r = lax.broadcasted_iota(jnp.int32, (2 * WINDOW, WINDOW), 1)
        dist = r + WINDOW - c
        ok = (dist >= 0) & (dist < WINDOW)
        bias_ref[0] = jnp.where(ok, 0.0, -jnp.inf)
        bias_ref[1] = jnp.where(ok & (c >= WINDOW), 0.0, -jnp.inf)

    @pl.when(s == 0)
    def _():
        kext_ref[0:WINDOW, :] = jnp.zeros((WINDOW, KV_WIDTH), F32)
        vt_ref[:, 0:WINDOW] = jnp.zeros((KV_WIDTH, WINDOW), F32)
        uext_ref[0:POOL_PAD, :] = jnp.zeros((POOL_PAD, POOL_WIDTH), F32)

    x = x_ref[0]
    h = _rms(x, gpre_ref[...]).astype(BF16)
    z_ref[...] = _dot(h, win_ref[...])
    kext_ref[WINDOW:WINDOW + tm, :] = z_ref[:, K_OFF:K_OFF + KV_WIDTH]
    for n in range(nblk):
        r0 = n * WINDOW
        vt_ref[:, WINDOW + r0:2 * WINDOW + r0] = z_ref[r0:r0 + WINDOW, V_OFF:V_OFF + KV_WIDTH].T

    lane = lax.broadcasted_iota(jnp.int32, (WINDOW, LANES), 1)
    lo = lane < HEAD_DIM
    lo2 = jnp.concatenate([lo, lo], axis=0)
    row8 = lax.broadcasted_iota(jnp.int32, (8, GROUP * WINDOW), 0)
    kcol = lax.broadcasted_iota(jnp.int32, (HEAD_DIM, 2 * WINDOW), 1)
    ones = jnp.ones((HEAD_DIM, 2 * WINDOW), F32)

    for n in range(nblk):
        r0 = n * WINDOW
        rows = slice(r0, r0 + WINDOW)
        kcat = kext_ref[r0:r0 + 2 * WINDOW, :]
        kswap = pltpu.roll(kcat, HEAD_DIM, 1)
        bias1 = bias_ref[jnp.where(s == 0, 1, 0)] if n == 0 else bias_ref[0]
        bias = jnp.concatenate([bias1] * GROUP, axis=1)
        for kv in range(N_KV_HEADS):
            kk = (jnp.where(lo2, kcat, kswap) if kv == 0 else jnp.where(lo2, kswap, kcat)).astype(BF16)
            pieces, sinks = [], []
            for g in range(GROUP):
                hd = kv * GROUP + g
                slab = hd // 2
                qs = z_ref[rows, slab * LANES:(slab + 1) * LANES] * (HEAD_DIM ** -0.5)
                keep = lo if hd % 2 == 0 else jnp.logical_not(lo)
                pieces.append(jnp.where(keep, qs, 0.0).astype(BF16))
                sinks.append(jnp.full((1, WINDOW), sinks_ref[layer, hd], F32))
            q4 = jnp.concatenate(pieces, axis=0)
            sink = jnp.concatenate(sinks, axis=1)
            st = _dot_t(kk, q4) + bias
            m = jnp.maximum(jnp.max(st, axis=0, keepdims=True), sink)
            p = jnp.exp(st - m)
            top = jnp.where(row8 == 0, jnp.exp(sink - m), p[0:8])
            pb = jnp.concatenate([top, p[8:]], axis=0).astype(BF16)
            vth = vt_ref[kv * HEAD_DIM:(kv + 1) * HEAD_DIM, r0:r0 + 2 * WINDOW]
            a = jnp.concatenate([jnp.where(kcol == 0, 0.0, vth), ones], axis=0).astype(BF16)
            ot = _dot(a, pb)
            inv = 1.0 / ot[HEAD_DIM:HEAD_DIM + 8]
            on = ot[0:HEAD_DIM] * jnp.concatenate([inv] * (HEAD_DIM // 8), axis=0)
            for j in range(GROUP // 2):
                c0 = 2 * j * WINDOW
                pair = jnp.concatenate([on[:, c0:c0 + WINDOW], on[:, c0 + WINDOW:c0 + 2 * WINDOW]], axis=0)
                slab = kv * (GROUP // 2) + j
                mix_ref[rows, slab * LANES:(slab + 1) * LANES] = pair.T.astype(BF16)

    uext_ref[POOL_PAD:POOL_PAD + tm, :] = z_ref[:, U_OFF:U_OFF + POOL_WIDTH]
    pos = lax.broadcasted_iota(jnp.int32, (tm, 1), 0) + s * tm
    for gi, w in enumerate(POOL_WINDOWS):
        cols = slice(gi * POOL_GROUP_WIDTH, (gi + 1) * POOL_GROUP_WIDTH)
        cur = uext_ref[POOL_PAD:POOL_PAD + tm, cols]
        tot = cur
        for j in range(1, w):
            tot = tot + uext_ref[POOL_PAD - j:POOL_PAD - j + tm, cols]
        cnt = jnp.minimum(w, pos + 1).astype(F32)
        dlt = (tot / cnt - cur).astype(BF16)
        yp = _dot(dlt, wpool_ref[gi]) * pscale_ref[:, cols]
        mix_ref[:, ATTN_WIDTH + gi * POOL_GROUP_WIDTH:ATTN_WIDTH + (gi + 1) * POOL_GROUP_WIDTH] = yp.astype(BF16)

    mixed = _dot(mix_ref[...], wout_ref[...])
    y_ref[0] = x + _rms(mixed, gpost_ref[...])

    @pl.when(s == pl.num_programs(1) - 1)
    def _():
        klast_ref[0] = z_ref[tm - WINDOW:tm, K_OFF:K_OFF + KV_WIDTH]
        vlast_ref[0] = z_ref[tm - WINDOW:tm, V_OFF:V_OFF + KV_WIDTH]
        plast_ref[0] = uext_ref[tm:tm + POOL_PAD, :]

    kext_ref[0:WINDOW, :] = kext_ref[tm:tm + WINDOW, :]
    vt_ref[:, 0:WINDOW] = vt_ref[:, tm:tm + WINDOW]
    uext_ref[0:POOL_PAD, :] = uext_ref[tm:tm + POOL_PAD, :]


def _mix_prompt(x, sinks, gpre, gpost, win, wout, wpool, pscale, *, layer, tm):
    b, s, _ = x.shape
    kern = functools.partial(_mix_prompt_kernel, tm=tm, layer=layer)
    return pl.pallas_call(
        kern,
        grid=(b, s // tm),
        in_specs=[
            pl.BlockSpec(memory_space=pltpu.SMEM),
            pl.BlockSpec((1, tm, D_MODEL), lambda i, j: (i, j, 0)),
            _const_spec((1, D_MODEL), layer),
            _const_spec((1, D_MODEL), layer),
            _const_spec((D_MODEL, IN_WIDTH), layer),
            _const_spec((D_MODEL, D_MODEL), layer),
            _const_spec((len(POOL_WINDOWS), POOL_GROUP_WIDTH, POOL_GROUP_WIDTH), layer),
            _const_spec((1, POOL_WIDTH), layer),
        ],
        out_specs=[
            pl.BlockSpec((1, tm, D_MODEL), lambda i, j: (i, j, 0)),
            pl.BlockSpec((1, WINDOW, KV_WIDTH), lambda i, j: (i, 0, 0)),
            pl.BlockSpec((1, WINDOW, KV_WIDTH), lambda i, j: (i, 0, 0)),
            pl.BlockSpec((1, POOL_PAD, POOL_WIDTH), lambda i, j: (i, 0, 0)),
        ],
        out_shape=[
            jax.ShapeDtypeStruct((b, s, D_MODEL), F32),
            jax.ShapeDtypeStruct((b, WINDOW, KV_WIDTH), F32),
            jax.ShapeDtypeStruct((b, WINDOW, KV_WIDTH), F32),
            jax.ShapeDtypeStruct((b, POOL_PAD, POOL_WIDTH), F32),
        ],
        scratch_shapes=[
            pltpu.VMEM((tm, IN_WIDTH), F32),
            pltpu.VMEM((WINDOW + tm, KV_WIDTH), F32),
            pltpu.VMEM((KV_WIDTH, WINDOW + tm), F32),
            pltpu.VMEM((POOL_PAD + tm, POOL_WIDTH), F32),
            pltpu.VMEM((tm, D_MODEL), BF16),
            pltpu.VMEM((2, 2 * WINDOW, WINDOW), F32),
        ],
        compiler_params=pltpu.CompilerParams(
            dimension_semantics=("arbitrary", "arbitrary"), vmem_limit_bytes=VMEM_LIMIT),
        name=f"mix_prompt_{layer}",
    )(sinks, x, gpre, gpost, win, wout, wpool, pscale)


def _ffn_kernel(x_ref, p_ref, gpre_ref, gpost_ref, wg_ref, wu_ref, wd_ref, wple_ref, wpg_ref, o_ref):
    x = x_ref[...]
    f = _rms(x, gpre_ref[...]).astype(BF16)
    g = _dot(f, wg_ref[...])
    u = _dot(f, wu_ref[...])
    a = (g * _sigmoid(g) * u).astype(BF16)
    d = _dot(a, wd_ref[...])
    x = x + _rms(d, gpost_ref[...])
    gate = _sigmoid(_dot(x.astype(BF16), wpg_ref[...]))
    o_ref[...] = x + gate * _dot(p_ref[...].astype(BF16), wple_ref[...])


def _ffn(x, p, gpre, gpost, wg, wu, wd, wple, wpg, *, layer, tm, tag):
    n = x.shape[0]
    return pl.pallas_call(
        _ffn_kernel,
        grid=(n // tm,),
        in_specs=[
            pl.BlockSpec((tm, D_MODEL), lambda i: (i, 0)),
            pl.BlockSpec((None, tm, PLE_DIM), lambda i: (layer, i, 0)),
            _const_spec((1, D_MODEL), layer),
            _const_spec((1, D_MODEL), layer),
            _const_spec((D_MODEL, D_FF), layer),
            _const_spec((D_MODEL, D_FF), layer),
            _const_spec((D_FF, D_MODEL), layer),
            _const_spec((PLE_DIM, D_MODEL), layer),
            _const_spec((D_MODEL, D_MODEL), layer),
        ],
        out_specs=pl.BlockSpec((tm, D_MODEL), lambda i: (i, 0)),
        out_shape=jax.ShapeDtypeStruct((n, D_MODEL), F32),
        compiler_params=pltpu.CompilerParams(
            dimension_semantics=("arbitrary",), vmem_limit_bytes=VMEM_LIMIT),
        name=f"ffn_{tag}_{layer}",
    )(x, p, gpre, gpost, wg, wu, wd, wple, wpg)


def _inproj_kernel(x_ref, gpre_ref, win_ref, z_ref):
    h = _rms(x_ref[...], gpre_ref[...]).astype(BF16)
    z_ref[...] = _dot(h, win_ref[...])


def _inproj(x, gpre, win, *, layer):
    n = x.shape[0]
    return pl.pallas_call(
        _inproj_kernel,
        grid=(1,),
        in_specs=[
            pl.BlockSpec((n, D_MODEL), lambda i: (0, 0)),
            _const_spec((1, D_MODEL), layer),
            _const_spec((D_MODEL, IN_WIDTH), layer),
        ],
        out_specs=pl.BlockSpec((n, IN_WIDTH), lambda i: (0, 0)),
        out_shape=jax.ShapeDtypeStruct((n, IN_WIDTH), F32),
        compiler_params=pltpu.CompilerParams(
            dimension_semantics=("arbitrary",), vmem_limit_bytes=VMEM_LIMIT),
        name=f"inproj_sample_{layer}",
    )(x, gpre, win)


def _outproj_kernel(x_ref, mix_ref, gpost_ref, wout_ref, y_ref):
    mixed = _dot(mix_ref[...].astype(BF16), wout_ref[...])
    y_ref[...] = x_ref[...] + _rms(mixed, gpost_ref[...])


def _outproj(x, mix, gpost, wout, *, layer):
    n = x.shape[0]
    return pl.pallas_call(
        _outproj_kernel,
        grid=(1,),
        in_specs=[
            pl.BlockSpec((n, D_MODEL), lambda i: (0, 0)),
            pl.BlockSpec((n, D_MODEL), lambda i: (0, 0)),
            _const_spec((1, D_MODEL), layer),
            _const_spec((D_MODEL, D_MODEL), layer),
        ],
        out_specs=pl.BlockSpec((n, D_MODEL), lambda i: (0, 0)),
        out_shape=jax.ShapeDtypeStruct((n, D_MODEL), F32),
        compiler_params=pltpu.CompilerParams(
            dimension_semantics=("arbitrary",), vmem_limit_bytes=VMEM_LIMIT),
        name=f"outproj_sample_{layer}",
    )(x, mix, gpost, wout)


def _mix_sample_kernel(sinks_ref, z_ref, ck_ref, cv_ref, st_ref, wpool_ref, pscale_ref,
                       mix_ref, ko_ref, vo_ref, po_ref, ext_ref, dlt_ref, *, bb, dec, layer):
    per_tile = 8 // dec
    lane = lax.broadcasted_iota(jnp.int32, (8, LANES), 1)
    lo = lane < HEAD_DIM
    hi = jnp.logical_not(lo)
    rows64 = N_HEADS * 8
    trow = lax.broadcasted_iota(jnp.int32, (rows64, LANES), 0) & 7
    tstep = trow % dec
    tbat = trow // dec
    klane = lax.broadcasted_iota(jnp.int32, (rows64, LANES), 1)
    valid_c = klane >= tstep + 1
    tstep1 = tstep[:, 0:1]
    tbat1 = tbat[:, 0:1]

    for t in range(bb // per_tile):
        rows = slice(8 * t, 8 * t + 8)
        q = z_ref[rows, 0:ATTN_WIDTH] * (HEAD_DIM ** -0.5)
        knew = z_ref[rows, K_OFF:K_OFF + KV_WIDTH]
        vnew = z_ref[rows, V_OFF:V_OFF + KV_WIDTH]
        pieces, sinks = [], []
        for hd in range(N_HEADS):
            slab, half, kv = hd // 2, hd % 2, hd // GROUP
            qs = q[:, slab * LANES:(slab + 1) * LANES]
            src = qs if half == kv else pltpu.roll(qs, HEAD_DIM, 1)
            pieces.append(jnp.where(lo if kv == 0 else hi, src, 0.0))
            sinks.append(jnp.full((8, 1), sinks_ref[layer, hd], F32))
        lhs = jnp.concatenate(pieces, axis=0)
        sink = jnp.concatenate(sinks, axis=0)
        lhs_b = lhs.astype(BF16)

        sc = None
        for i in range(per_tile):
            kb = ck_ref[per_tile * t + i].astype(BF16)
            si = _dot_t(lhs_b, kb)
            sc = si if sc is None else jnp.where(tbat == i, si, sc)
        sc = jnp.where(valid_c, sc, -jnp.inf)

        sn, vn_rows = [], []
        for i in range(dec):
            kr, vr = None, None
            for bi in range(per_tile):
                kbrow = jnp.broadcast_to(knew[bi * dec + i:bi * dec + i + 1, :], (rows64, LANES))
                vbrow = jnp.broadcast_to(vnew[bi * dec + i:bi * dec + i + 1, :], (rows64, LANES))
                kr = kbrow if kr is None else jnp.where(tbat == bi, kbrow, kr)
                vr = vbrow if vr is None else jnp.where(tbat == bi, vbrow, vr)
            s_i = jnp.sum(lhs * kr, axis=-1, keepdims=True)
            sn.append(jnp.where(tstep1 >= i, s_i, -jnp.inf))
            vn_rows.append(vr)

        m = jnp.maximum(jnp.max(sc, axis=-1, keepdims=True), sink)
        for s_i in sn:
            m = jnp.maximum(m, s_i)
        p = jnp.exp(sc - m)
        denom = jnp.sum(p, axis=-1, keepdims=True) + jnp.exp(sink - m)
        pb = p.astype(BF16)
        o = None
        for i in range(per_tile):
            vb = cv_ref[per_tile * t + i].astype(BF16)
            oi = _dot(pb, vb)
            o = oi if o is None else jnp.where(tbat == i, oi, o)
        for s_i, vr in zip(sn, vn_rows):
            p_i = jnp.exp(s_i - m)
            denom = denom + p_i
            o = o + p_i * vr
        o = o / denom
        for slab in range(N_HEADS // 2):
            kv = (2 * slab) // GROUP
            even = o[16 * slab:16 * slab + 8]
            odd = o[16 * slab + 8:16 * slab + 16]
            if kv == 0:
                res = jnp.where(lo, even, pltpu.roll(odd, HEAD_DIM, 1))
            else:
                res = jnp.where(lo, pltpu.roll(even, HEAD_DIM, 1), odd)
            mix_ref[rows, slab * LANES:(slab + 1) * LANES] = res

    for b in range(bb):
        ko_ref[b, 0:WINDOW - dec, :] = ck_ref[b, dec:WINDOW, :]
        vo_ref[b, 0:WINDOW - dec, :] = cv_ref[b, dec:WINDOW, :]
        ko_ref[b, WINDOW - dec:WINDOW, :] = z_ref[b * dec:(b + 1) * dec, K_OFF:K_OFF + KV_WIDTH]
        vo_ref[b, WINDOW - dec:WINDOW, :] = z_ref[b * dec:(b + 1) * dec, V_OFF:V_OFF + KV_WIDTH]

    for b in range(bb):
        ext_ref[0:POOL_STATE, :] = st_ref[b]
        ext_ref[POOL_STATE:POOL_STATE + dec, :] = z_ref[b * dec:(b + 1) * dec, U_OFF:U_OFF + POOL_WIDTH]
        for gi, w in enumerate(POOL_WINDOWS):
            cols = slice(gi * POOL_GROUP_WIDTH, (gi + 1) * POOL_GROUP_WIDTH)
            cur = ext_ref[POOL_STATE:POOL_STATE + dec, cols]
            tot = cur
            for j in range(1, w):
                tot = tot + ext_ref[POOL_STATE - j:POOL_STATE - j + dec, cols]
            dlt_ref[b * dec:(b + 1) * dec, cols] = tot / float(w) - cur
        po_ref[b] = ext_ref[dec:dec + POOL_STATE, :]
    for gi in range(len(POOL_WINDOWS)):
        cols = slice(gi * POOL_GROUP_WIDTH, (gi + 1) * POOL_GROUP_WIDTH)
        yp = _dot(dlt_ref[:, cols].astype(BF16), wpool_ref[gi]) * pscale_ref[:, cols]
        mix_ref[:, ATTN_WIDTH + gi * POOL_GROUP_WIDTH:ATTN_WIDTH + (gi + 1) * POOL_GROUP_WIDTH] = yp


def _mix_sample(z, sinks, ck, cv, st, wpool, pscale, *, layer, bb, dec):
    nb = ck.shape[1]
    kern = functools.partial(_mix_sample_kernel, bb=bb, dec=dec, layer=layer)
    return pl.pallas_call(
        kern,
        grid=(nb // bb,),
        in_specs=[
            pl.BlockSpec(memory_space=pltpu.SMEM),
            pl.BlockSpec((bb * dec, IN_WIDTH), lambda i: (i, 0)),
            pl.BlockSpec((None, bb, WINDOW, KV_WIDTH), lambda i: (layer, i, 0, 0)),
            pl.BlockSpec((None, bb, WINDOW, KV_WIDTH), lambda i: (layer, i, 0, 0)),
            pl.BlockSpec((None, bb, POOL_STATE, POOL_WIDTH), lambda i: (layer, i, 0, 0)),
            _const_spec((len(POOL_WINDOWS), POOL_GROUP_WIDTH, POOL_GROUP_WIDTH), layer),
            _const_spec((1, POOL_WIDTH), layer),
        ],
        out_specs=[
            pl.BlockSpec((bb * dec, D_MODEL), lambda i: (i, 0)),
            pl.BlockSpec((bb, WINDOW, KV_WIDTH), lambda i: (i, 0, 0)),
            pl.BlockSpec((bb, WINDOW, KV_WIDTH), lambda i: (i, 0, 0)),
            pl.BlockSpec((bb, POOL_STATE, POOL_WIDTH), lambda i: (i, 0, 0)),
        ],
        out_shape=[
            jax.ShapeDtypeStruct((nb * dec, D_MODEL), F32),
            jax.ShapeDtypeStruct((nb, WINDOW, KV_WIDTH), F32),
            jax.ShapeDtypeStruct((nb, WINDOW, KV_WIDTH), F32),
            jax.ShapeDtypeStruct((nb, POOL_STATE, POOL_WIDTH), F32),
        ],
        scratch_shapes=[
            pltpu.VMEM((POOL_STATE + 9, POOL_WIDTH), F32),
            pltpu.VMEM((bb * dec, POOL_WIDTH), F32),
        ],
        compiler_params=pltpu.CompilerParams(
            dimension_semantics=("arbitrary",), vmem_limit_bytes=VMEM_LIMIT),
        name=f"mix_sample_{layer}",
    )(sinks, z, ck, cv, st, wpool, pscale)


def kernel(x_prompt, x_sample, p_prompt, p_sample, cache_k, cache_v, state_pool, norm_mix_pre, norm_mix_post,
           norm_ffn_pre, norm_ffn_post, w_in, w_out, attn_sinks, w_pool, pool_scale, w_gate, w_up, w_down, w_ple,
           w_ple_gate):
    depth, nbat, seq, _ = p_prompt.shape
    dec_b, dec = x_sample.shape[0], x_sample.shape[1]
    assert 8 % dec == 0 and seq % WINDOW == 0

    win, wout, wpool = w_in.astype(BF16), w_out.astype(BF16), w_pool.astype(BF16)
    wg, wu, wd = w_gate.astype(BF16), w_up.astype(BF16), w_down.astype(BF16)
    wple, wpg = w_ple.astype(BF16), w_ple_gate.astype(BF16)
    g_mix_pre = norm_mix_pre.reshape(depth, 1, D_MODEL)
    g_mix_post = norm_mix_post.reshape(depth, 1, D_MODEL)
    g_ffn_pre = norm_ffn_pre.reshape(depth, 1, D_MODEL)
    g_ffn_post = norm_ffn_post.reshape(depth, 1, D_MODEL)
    pscale = pool_scale.reshape(depth, 1, POOL_WIDTH)

    pp = p_prompt.reshape(depth, nbat * seq, PLE_DIM)
    ps = p_sample.reshape(depth, dec_b * dec, PLE_DIM)
    ck = cache_k.reshape(depth, dec_b, WINDOW, KV_WIDTH)
    cv = cache_v.reshape(depth, dec_b, WINDOW, KV_WIDTH)

    yp = x_prompt
    ys = x_sample.reshape(dec_b * dec, D_MODEL)
    kp_l, vp_l, sp_l, ks_l, vs_l, ss_l = [], [], [], [], [], []
    for i in range(depth):
        yp, kp, vp, sp = _mix_prompt(yp, attn_sinks, g_mix_pre, g_mix_post, win, wout, wpool, pscale,
                                     layer=i, tm=256)
        yp = _ffn(yp.reshape(nbat * seq, D_MODEL), pp, g_ffn_pre, g_ffn_post, wg, wu, wd, wple, wpg,
                  layer=i, tm=256, tag="prompt").reshape(nbat, seq, D_MODEL)
        kp_l.append(kp)
        vp_l.append(vp)
        sp_l.append(sp[:, POOL_PAD - POOL_STATE:])

        z = _inproj(ys, g_mix_pre, win, layer=i)
        mix, kss, vss, sss = _mix_sample(z, attn_sinks, ck, cv, state_pool, wpool, pscale,
                                         layer=i, bb=8, dec=dec)
        ys = _outproj(ys, mix, g_mix_post, wout, layer=i)
        ys = _ffn(ys, ps, g_ffn_pre, g_ffn_post, wg, wu, wd, wple, wpg,
                  layer=i, tm=dec_b * dec, tag="sample")
        ks_l.append(kss)
        vs_l.append(vss)
        ss_l.append(sss)

    kv_shape = (depth, -1, WINDOW, N_KV_HEADS, HEAD_DIM)
    return (yp, ys.reshape(dec_b, dec, D_MODEL),
            jnp.stack(kp_l).reshape(kv_shape), jnp.stack(vp_l).reshape(kv_shape), jnp.stack(sp_l),
            jnp.stack(ks_l).reshape(kv_shape), jnp.stack(vs_l).reshape(kv_shape), jnp.stack(ss_l))
```

```python
import functools

import jax
import jax.numpy as jnp
from jax import lax
from jax.experimental import pallas as pl
from jax.experimental.pallas import tpu as pltpu

D_MODEL = 1024
DEPTH = 4
ATTN_WIDTH = 512
HEAD_DIM = 64
N_HEADS = 8
N_KV_HEADS = 2
GROUP = 4
KV_WIDTH = 128
WINDOW = 128
POOL_WIDTH = 512
POOL_WINDOWS = (2, 4, 8, 16)
POOL_GROUP_WIDTH = 128
POOL_STATE = 15
IN_WIDTH = 1280
D_FF = 2816
PLE_DIM = 256
EPS = 1e-6

K_OFF = ATTN_WIDTH
V_OFF = ATTN_WIDTH + KV_WIDTH
U_OFF = ATTN_WIDTH + 2 * KV_WIDTH
LANES = 128
POOL_PAD = 16

BF16 = jnp.bfloat16
F32 = jnp.float32
VMEM_LIMIT = 56 * 1024 * 1024


def _rms(x, g):
    return x * lax.rsqrt(jnp.mean(x * x, axis=-1, keepdims=True) + EPS) * g


def _sigmoid(x):
    return 1.0 / (1.0 + jnp.exp(-x))


def _dot(a, b):
    return jnp.dot(a, b, preferred_element_type=F32)


def _dot_t(a, b):
    return lax.dot_general(a, b, (((1,), (1,)), ((), ())), preferred_element_type=F32)


def _const_spec(shape, layer=None):
    if layer is None:
        return pl.BlockSpec(shape, lambda *_: (0,) * len(shape), pipeline_mode=pl.Buffered(1))
    return pl.BlockSpec((None,) + shape, lambda *_: (layer,) + (0,) * len(shape),
                        pipeline_mode=pl.Buffered(1))


def _mix_prompt_kernel(sinks_ref, x_ref, gpre_ref, gpost_ref, win_ref, wout_ref, wpool_ref, pscale_ref,
                       y_ref, klast_ref, vlast_ref, plast_ref,
                       z_ref, kext_ref, vt_ref, uext_ref, mix_ref, bias_ref, *, tm, layer):
    s = pl.program_id(1)
    nblk = tm // WINDOW

    @pl.when((pl.program_id(0) == 0) & (s == 0))
    def _():
        c = lax.broadcasted_iota(jnp.int32, (2 * WINDOW, WINDOW), 0)
        r = lax.broadcasted_iota(jnp.int32, (2 * WINDOW, WINDOW), 1)
        dist = r + WINDOW - c
        ok = (dist >= 0) & (dist < WINDOW)
        bias_ref[0] = jnp.where(ok, 0.0, -jnp.inf)
        bias_ref[1] = jnp.where(ok & (c >= WINDOW), 0.0, -jnp.inf)

    @pl.when(s == 0)
    def _():
        kext_ref[0:WINDOW, :] = jnp.zeros((WINDOW, KV_WIDTH), F32)
        vt_ref[:, 0:WINDOW] = jnp.zeros((KV_WIDTH, WINDOW), F32)
        uext_ref[0:POOL_PAD, :] = jnp.zeros((POOL_PAD, POOL_WIDTH), F32)

    x = x_ref[0]
    h = _rms(x, gpre_ref[...]).astype(BF16)
    z_ref[...] = _dot(h, win_ref[...])
    kext_ref[WINDOW:WINDOW + tm, :] = z_ref[:, K_OFF:K_OFF + KV_WIDTH]
    for n in range(nblk):
        r0 = n * WINDOW
        vt_ref[:, WINDOW + r0:2 * WINDOW + r0] = z_ref[r0:r0 + WINDOW, V_OFF:V_OFF + KV_WIDTH].T

    lane = lax.broadcasted_iota(jnp.int32, (WINDOW, LANES), 1)
    lo = lane < HEAD_DIM
    lo2 = jnp.concatenate([lo, lo], axis=0)
    row8 = lax.broadcasted_iota(jnp.int32, (8, GROUP * WINDOW), 0)
    kcol = lax.broadcasted_iota(jnp.int32, (HEAD_DIM, 2 * WINDOW), 1)
    ones = jnp.ones((HEAD_DIM, 2 * WINDOW), F32)

    for n in range(nblk):
        r0 = n * WINDOW
        rows = slice(r0, r0 + WINDOW)
        kcat = kext_ref[r0:r0 + 2 * WINDOW, :]
        kswap = pltpu.roll(kcat, HEAD_DIM, 1)
        bias1 = bias_ref[jnp.where(s == 0, 1, 0)] if n == 0 else bias_ref[0]
        bias = jnp.concatenate([bias1] * GROUP, axis=1)
        for kv in range(N_KV_HEADS):
            kk = (jnp.where(lo2, kcat, kswap) if kv == 0 else jnp.where(lo2, kswap, kcat)).astype(BF16)
            pieces, sinks = [], []
            for g in range(GROUP):
                hd = kv * GROUP + g
                slab = hd // 2
                qs = z_ref[rows, slab * LANES:(slab + 1) * LANES] * (HEAD_DIM ** -0.5)
                keep = lo if hd % 2 == 0 else jnp.logical_not(lo)
                pieces.append(jnp.where(keep, qs, 0.0).astype(BF16))
                sinks.append(jnp.full((1, WINDOW), sinks_ref[layer, hd], F32))
            q4 = jnp.concatenate(pieces, axis=0)
            sink = jnp.concatenate(sinks, axis=1)
            st = _dot_t(kk, q4) + bias
            m = jnp.maximum(jnp.max(st, axis=0, keepdims=True), sink)
            p = jnp.exp(st - m)
            top = jnp.where(row8 == 0, jnp.exp(sink - m), p[0:8])
            pb = jnp.concatenate([top, p[8:]], axis=0).astype(BF16)
            vth = vt_ref[kv * HEAD_DIM:(kv + 1) * HEAD_DIM, r0:r0 + 2 * WINDOW]
            a = jnp.concatenate([jnp.where(kcol == 0, 0.0, vth), ones], axis=0).astype(BF16)
            ot = _dot(a, pb)
            inv = 1.0 / ot[HEAD_DIM:HEAD_DIM + 8]
            on = ot[0:HEAD_DIM] * jnp.concatenate([inv] * (HEAD_DIM // 8), axis=0)
            for j in range(GROUP // 2):
                c0 = 2 * j * WINDOW
                pair = jnp.concatenate([on[:, c0:c0 + WINDOW], on[:, c0 + WINDOW:c0 + 2 * WINDOW]], axis=0)
                slab = kv * (GROUP // 2) + j
                mix_ref[rows, slab * LANES:(slab + 1) * LANES] = pair.T.astype(BF16)

    uext_ref[POOL_PAD:POOL_PAD + tm, :] = z_ref[:, U_OFF:U_OFF + POOL_WIDTH]
    pos = lax.broadcasted_iota(jnp.int32, (tm, 1), 0) + s * tm
    for gi, w in enumerate(POOL_WINDOWS):
        cols = slice(gi * POOL_GROUP_WIDTH, (gi + 1) * POOL_GROUP_WIDTH)
        cur = uext_ref[POOL_PAD:POOL_PAD + tm, cols]
        tot = cur
        for j in range(1, w):
            tot = tot + uext_ref[POOL_PAD - j:POOL_PAD - j + tm, cols]
        cnt = jnp.minimum(w, pos + 1).astype(F32)
        dlt = (tot / cnt - cur).astype(BF16)
        yp = _dot(dlt, wpool_ref[gi]) * pscale_ref[:, cols]
        mix_ref[:, ATTN_WIDTH + gi * POOL_GROUP_WIDTH:ATTN_WIDTH + (gi + 1) * POOL_GROUP_WIDTH] = yp.astype(BF16)

    mixed = _dot(mix_ref[...], wout_ref[...])
    y_ref[0] = x + _rms(mixed, gpost_ref[...])

    @pl.when(s == pl.num_programs(1) - 1)
    def _():
        klast_ref[0] = z_ref[tm - WINDOW:tm, K_OFF:K_OFF + KV_WIDTH]
        vlast_ref[0] = z_ref[tm - WINDOW:tm, V_OFF:V_OFF + KV_WIDTH]
        plast_ref[0] = uext_ref[tm:tm + POOL_PAD, :]

    kext_ref[0:WINDOW, :] = kext_ref[tm:tm + WINDOW, :]
    vt_ref[:, 0:WINDOW] = vt_ref[:, tm:tm + WINDOW]
    uext_ref[0:POOL_PAD, :] = uext_ref[tm:tm + POOL_PAD, :]


def _mix_prompt(x, sinks, gpre, gpost, win, wout, wpool, pscale, *, layer, tm):
    b, s, _ = x.shape
    kern = functools.partial(_mix_prompt_kernel, tm=tm, layer=layer)
    return pl.pallas_call(
        kern,
        grid=(b, s // tm),
        in_specs=[
            pl.BlockSpec(memory_space=pltpu.SMEM),
            pl.BlockSpec((1, tm, D_MODEL), lambda i, j: (i, j, 0)),
            _const_spec((1, D_MODEL), layer),
            _const_spec((1, D_MODEL), layer),
            _const_spec((D_MODEL, IN_WIDTH), layer),
            _const_spec((D_MODEL, D_MODEL), layer),
            _const_spec((len(POOL_WINDOWS), POOL_GROUP_WIDTH, POOL_GROUP_WIDTH), layer),
            _const_spec((1, POOL_WIDTH), layer),
        ],
        out_specs=[
            pl.BlockSpec((1, tm, D_MODEL), lambda i, j: (i, j, 0)),
            pl.BlockSpec((1, WINDOW, KV_WIDTH), lambda i, j: (i, 0, 0)),
            pl.BlockSpec((1, WINDOW, KV_WIDTH), lambda i, j: (i, 0, 0)),
            pl.BlockSpec((1, POOL_PAD, POOL_WIDTH), lambda i, j: (i, 0, 0)),
        ],
        out_shape=[
            jax.ShapeDtypeStruct((b, s, D_MODEL), F32),
            jax.ShapeDtypeStruct((b, WINDOW, KV_WIDTH), F32),
            jax.ShapeDtypeStruct((b, WINDOW, KV_WIDTH), F32),
            jax.ShapeDtypeStruct((b, POOL_PAD, POOL_WIDTH), F32),
        ],
        scratch_shapes=[
            pltpu.VMEM((tm, IN_WIDTH), F32),
            pltpu.VMEM((WINDOW + tm, KV_WIDTH), F32),
            pltpu.VMEM((KV_WIDTH, WINDOW + tm), F32),
            pltpu.VMEM((POOL_PAD + tm, POOL_WIDTH), F32),
            pltpu.VMEM((tm, D_MODEL), BF16),
            pltpu.VMEM((2, 2 * WINDOW, WINDOW), F32),
        ],
        compiler_params=pltpu.CompilerParams(
            dimension_semantics=("arbitrary", "arbitrary"), vmem_limit_bytes=VMEM_LIMIT),
        name=f"mix_prompt_{layer}",
    )(sinks, x, gpre, gpost, win, wout, wpool, pscale)


def _ffn_rows(x, p, gpre_ref, gpost_ref, wg_ref, wu_ref, wd_ref, wple_ref, wpg_ref):
    f = _rms(x, gpre_ref[...]).astype(BF16)
    g = _dot(f, wg_ref[...])
    u = _dot(f, wu_ref[...])
    a = (g * _sigmoid(g) * u).astype(BF16)
    d = _dot(a, wd_ref[...])
    x = x + _rms(d, gpost_ref[...])
    gate = _sigmoid(_dot(x.astype(BF16), wpg_ref[...]))
    return x + gate * _dot(p.astype(BF16), wple_ref[...])


def _ffn_kernel(x_ref, p_ref, gpre_ref, gpost_ref, wg_ref, wu_ref, wd_ref, wple_ref, wpg_ref, o_ref, *, nsub):
    sub = x_ref.shape[0] // nsub
    for i in range(nsub):
        rows = slice(i * sub, (i + 1) * sub)
        o_ref[rows, :] = _ffn_rows(x_ref[rows, :], p_ref[rows, :], gpre_ref, gpost_ref, wg_ref, wu_ref, wd_ref,
                                   wple_ref, wpg_ref)


def _ffn(x, p, gpre, gpost, wg, wu, wd, wple, wpg, *, layer, tm, nsub):
    n = x.shape[0]
    return pl.pallas_call(
        functools.partial(_ffn_kernel, nsub=nsub),
        grid=(n // tm,),
        in_specs=[
            pl.BlockSpec((tm, D_MODEL), lambda i: (i, 0)),
            pl.BlockSpec((None, tm, PLE_DIM), lambda i: (layer, i, 0)),
            _const_spec((1, D_MODEL), layer),
            _const_spec((1, D_MODEL), layer),
            _const_spec((D_MODEL, D_FF), layer),
            _const_spec((D_MODEL, D_FF), layer),
            _const_spec((D_FF, D_MODEL), layer),
            _const_spec((PLE_DIM, D_MODEL), layer),
            _const_spec((D_MODEL, D_MODEL), layer),
        ],
        out_specs=pl.BlockSpec((tm, D_MODEL), lambda i: (i, 0)),
        out_shape=jax.ShapeDtypeStruct((n, D_MODEL), F32),
        compiler_params=pltpu.CompilerParams(
            dimension_semantics=("arbitrary",), vmem_limit_bytes=VMEM_LIMIT),
        name=f"ffn_prompt_{layer}",
    )(x, p, gpre, gpost, wg, wu, wd, wple, wpg)


def _layer_spec(shape, single=False):
    kwargs = dict(pipeline_mode=pl.Buffered(1)) if single else {}
    return pl.BlockSpec((None,) + shape, lambda l, c: (l,) + (0,) * len(shape), **kwargs)


def _sample_kernel(sinks_ref, xs_ref, ps_ref, ck_ref, cv_ref, st_ref, gmpre_ref, gmpost_ref, gfpre_ref, gfpost_ref,
                   win_ref, wout_ref, wpool_ref, pscale_ref, wg_ref, wu_ref, wd_ref, wple_ref, wpg_ref,
                   y_ref, ko_ref, vo_ref, po_ref, z_ref, mix_ref, ext_ref, dlt_ref, *, bb, dec, nsub):
    l = pl.program_id(0)
    c = pl.program_id(1)
    n_rows = y_ref.shape[0]

    @pl.when((l == 0) & (c == 0))
    def _():
        y_ref[...] = xs_ref[...]

    @pl.when(c == 0)
    def _():
        h = _rms(y_ref[...], gmpre_ref[...]).astype(BF16)
        z_ref[...] = _dot(h, win_ref[...])

    per_tile = 8 // dec
    base = pl.multiple_of(c * (bb * dec), 8)
    lane = lax.broadcasted_iota(jnp.int32, (8, LANES), 1)
    lo = lane < HEAD_DIM
    hi = jnp.logical_not(lo)
    rows64 = N_HEADS * 8
    trow = lax.broadcasted_iota(jnp.int32, (rows64, LANES), 0) & 7
    tstep = trow % dec
    tbat = trow // dec
    klane = lax.broadcasted_iota(jnp.int32, (rows64, LANES), 1)
    valid_c = klane >= tstep + 1
    tstep1 = tstep[:, 0:1]

    for t in range(bb // per_tile):
        rows = pl.ds(base + 8 * t, 8)
        q = z_ref[rows, 0:ATTN_WIDTH] * (HEAD_DIM ** -0.5)
        knew = z_ref[rows, K_OFF:K_OFF + KV_WIDTH]
        vnew = z_ref[rows, V_OFF:V_OFF + KV_WIDTH]
        unew = z_ref[rows, U_OFF:U_OFF + POOL_WIDTH]
        pieces, sinks = [], []
        for hd in range(N_HEADS):
            slab, half, kv = hd // 2, hd % 2, hd // GROUP
            qs = q[:, slab * LANES:(slab + 1) * LANES]
            src = qs if half == kv else pltpu.roll(qs, HEAD_DIM, 1)
            pieces.append(jnp.where(lo if kv == 0 else hi, src, 0.0))
            sinks.append(jnp.full((8, 1), sinks_ref[l, hd], F32))
        lhs = jnp.concatenate(pieces, axis=0)
        sink = jnp.concatenate(sinks, axis=0)
        lhs_b = lhs.astype(BF16)

        sc = None
        for i in range(per_tile):
            kb = ck_ref[per_tile * t + i].astype(BF16)
            si = _dot_t(lhs_b, kb)
            sc = si if sc is None else jnp.where(tbat == i, si, sc)
        sc = jnp.where(valid_c, sc, -jnp.inf)

        sn, vn_rows = [], []
        for i in range(dec):
            kr, vr = None, None
            for bi in range(per_tile):
                kbrow = jnp.broadcast_to(knew[bi * dec + i:bi * dec + i + 1, :], (rows64, LANES))
                vbrow = jnp.broadcast_to(vnew[bi * dec + i:bi * dec + i + 1, :], (rows64, LANES))
                kr = kbrow if kr is None else jnp.where(tbat == bi, kbrow, kr)
                vr = vbrow if vr is None else jnp.where(tbat == bi, vbrow, vr)
            s_i = jnp.sum(lhs * kr, axis=-1, keepdims=True)
            sn.append(jnp.where(tstep1 >= i, s_i, -jnp.inf))
            vn_rows.append(vr)

        m = jnp.maximum(jnp.max(sc, axis=-1, keepdims=True), sink)
        for s_i in sn:
            m = jnp.maximum(m, s_i)
        p = jnp.exp(sc - m)
        denom = jnp.sum(p, axis=-1, keepdims=True) + jnp.exp(sink - m)
        pb = p.astype(BF16)
        o = None
        for i in range(per_tile):
            vb = cv_ref[per_tile * t + i].astype(BF16)
            oi = _dot(pb, vb)
            o = oi if o is None else jnp.where(tbat == i, oi, o)
        for s_i, vr in zip(sn, vn_rows):
            p_i = jnp.exp(s_i - m)
            denom = denom + p_i
            o = o + p_i * vr
        o = o / denom
        for slab in range(N_HEADS // 2):
            kv = (2 * slab) // GROUP
            even = o[16 * slab:16 * slab + 8]
            odd = o[16 * slab + 8:16 * slab + 16]
            if kv == 0:
                res = jnp.where(lo, even, pltpu.roll(odd, HEAD_DIM, 1))
            else:
                res = jnp.where(lo, pltpu.roll(even, HEAD_DIM, 1), odd)
            mix_ref[rows, slab * LANES:(slab + 1) * LANES] = res

        for bi in range(per_tile):
            b = per_tile * t + bi
            own = slice(bi * dec, (bi + 1) * dec)
            ko_ref[b, 0:WINDOW - dec, :] = ck_ref[b, dec:WINDOW, :]
            vo_ref[b, 0:WINDOW - dec, :] = cv_ref[b, dec:WINDOW, :]
            ko_ref[b, WINDOW - dec:WINDOW, :] = knew[own]
            vo_ref[b, WINDOW - dec:WINDOW, :] = vnew[own]
            ext_ref[0:POOL_STATE, :] = st_ref[b]
            ext_ref[POOL_STATE:POOL_STATE + dec, :] = unew[own]
            for gi, w in enumerate(POOL_WINDOWS):
                cols = slice(gi * POOL_GROUP_WIDTH, (gi + 1) * POOL_GROUP_WIDTH)
                cur = ext_ref[POOL_STATE:POOL_STATE + dec, cols]
                tot = cur
                for j in range(1, w):
                    tot = tot + ext_ref[POOL_STATE - j:POOL_STATE - j + dec, cols]
                dlt_ref[b * dec:(b + 1) * dec, cols] = tot / float(w) - cur
            po_ref[b] = ext_ref[dec:dec + POOL_STATE, :]

    chunk = pl.ds(base, bb * dec)
    for gi in range(len(POOL_WINDOWS)):
        cols = slice(gi * POOL_GROUP_WIDTH, (gi + 1) * POOL_GROUP_WIDTH)
        yp = _dot(dlt_ref[:, cols].astype(BF16), wpool_ref[gi]) * pscale_ref[:, cols]
        mix_ref[chunk, ATTN_WIDTH + gi * POOL_GROUP_WIDTH:ATTN_WIDTH + (gi + 1) * POOL_GROUP_WIDTH] = yp

    @pl.when(c == pl.num_programs(1) - 1)
    def _():
        sub = n_rows // nsub
        for i in range(nsub):
            rs = slice(i * sub, (i + 1) * sub)
            mixed = _dot(mix_ref[rs, :].astype(BF16), wout_ref[...])
            x = y_ref[rs, :] + _rms(mixed, gmpost_ref[...])
            y_ref[rs, :] = _ffn_rows(x, ps_ref[rs, :], gfpre_ref, gfpost_ref, wg_ref, wu_ref, wd_ref, wple_ref,
                                     wpg_ref)


def _sample_path(xs, ps, sinks, ck, cv, st, gmpre, gmpost, gfpre, gfpost, win, wout, wpool, pscale,
                 wg, wu, wd, wple, wpg, *, bb, dec, nsub):
    depth, nb = ck.shape[0], ck.shape[1]
    n = nb * dec
    kern = functools.partial(_sample_kernel, bb=bb, dec=dec, nsub=nsub)
    cache_spec = pl.BlockSpec((None, bb, WINDOW, KV_WIDTH), lambda l, c: (l, c, 0, 0))
    state_spec = pl.BlockSpec((None, bb, POOL_STATE, POOL_WIDTH), lambda l, c: (l, c, 0, 0))
    return pl.pallas_call(
        kern,
        grid=(depth, nb // bb),
        in_specs=[
            pl.BlockSpec(memory_space=pltpu.SMEM),
            pl.BlockSpec((n, D_MODEL), lambda l, c: (0, 0), pipeline_mode=pl.Buffered(1)),
            _layer_spec((n, PLE_DIM)),
            cache_spec, cache_spec, state_spec,
            _layer_spec((1, D_MODEL)), _layer_spec((1, D_MODEL)), _layer_spec((1, D_MODEL)), _layer_spec((1, D_MODEL)),
            _layer_spec((D_MODEL, IN_WIDTH), single=True),
            _layer_spec((D_MODEL, D_MODEL), single=True),
            _layer_spec((len(POOL_WINDOWS), POOL_GROUP_WIDTH, POOL_GROUP_WIDTH)),
            _layer_spec((1, POOL_WIDTH)),
            _layer_spec((D_MODEL, D_FF), single=True),
            _layer_spec((D_MODEL, D_FF), single=True),
            _layer_spec((D_FF, D_MODEL), single=True),
            _layer_spec((PLE_DIM, D_MODEL), single=True),
            _layer_spec((D_MODEL, D_MODEL), single=True),
        ],
        out_specs=[
            pl.BlockSpec((n, D_MODEL), lambda l, c: (0, 0)),
            cache_spec, cache_spec, state_spec,
        ],
        out_shape=[
            jax.ShapeDtypeStruct((n, D_MODEL), F32),
            jax.ShapeDtypeStruct((depth, nb, WINDOW, KV_WIDTH), F32),
            jax.ShapeDtypeStruct((depth, nb, WINDOW, KV_WIDTH), F32),
            jax.ShapeDtypeStruct((depth, nb, POOL_STATE, POOL_WIDTH), F32),
        ],
        scratch_shapes=[
            pltpu.VMEM((n, IN_WIDTH), F32),
            pltpu.VMEM((n, D_MODEL), F32),
            pltpu.VMEM((POOL_STATE + 9, POOL_WIDTH), F32),
            pltpu.VMEM((bb * dec, POOL_WIDTH), F32),
        ],
        compiler_params=pltpu.CompilerParams(
            dimension_semantics=("arbitrary", "arbitrary"), vmem_limit_bytes=VMEM_LIMIT),
        name="sample_path",
    )(sinks, xs, ps, ck, cv, st, gmpre, gmpost, gfpre, gfpost, win, wout, wpool, pscale, wg, wu, wd, wple, wpg)


def kernel(x_prompt, x_sample, p_prompt, p_sample, cache_k, cache_v, state_pool, norm_mix_pre, norm_mix_post,
           norm_ffn_pre, norm_ffn_post, w_in, w_out, attn_sinks, w_pool, pool_scale, w_gate, w_up, w_down, w_ple,
           w_ple_gate):
    depth, nbat, seq, _ = p_prompt.shape
    dec_b, dec = x_sample.shape[0], x_sample.shape[1]
    assert 8 % dec == 0 and seq % WINDOW == 0

    win, wout, wpool = w_in.astype(BF16), w_out.astype(BF16), w_pool.astype(BF16)
    wg, wu, wd = w_gate.astype(BF16), w_up.astype(BF16), w_down.astype(BF16)
    wple, wpg = w_ple.astype(BF16), w_ple_gate.astype(BF16)
    g_mix_pre = norm_mix_pre.reshape(depth, 1, D_MODEL)
    g_mix_post = norm_mix_post.reshape(depth, 1, D_MODEL)
    g_ffn_pre = norm_ffn_pre.reshape(depth, 1, D_MODEL)
    g_ffn_post = norm_ffn_post.reshape(depth, 1, D_MODEL)
    pscale = pool_scale.reshape(depth, 1, POOL_WIDTH)

    pp = p_prompt.reshape(depth, nbat * seq, PLE_DIM)
    ps = p_sample.reshape(depth, dec_b * dec, PLE_DIM)
    ck = cache_k.reshape(depth, dec_b, WINDOW, KV_WIDTH)
    cv = cache_v.reshape(depth, dec_b, WINDOW, KV_WIDTH)

    yp = x_prompt
    kp_l, vp_l, sp_l = [], [], []
    for i in range(depth):
        yp, kp, vp, sp = _mix_prompt(yp, attn_sinks, g_mix_pre, g_mix_post, win, wout, wpool, pscale,
                                     layer=i, tm=256)
        yp = _ffn(yp.reshape(nbat * seq, D_MODEL), pp, g_ffn_pre, g_ffn_post, wg, wu, wd, wple, wpg,
                  layer=i, tm=512, nsub=2).reshape(nbat, seq, D_MODEL)
        kp_l.append(kp)
        vp_l.append(vp)
        sp_l.append(sp[:, POOL_PAD - POOL_STATE:])

    ys, ks, vs, ss = _sample_path(x_sample.reshape(dec_b * dec, D_MODEL), ps, attn_sinks, ck, cv, state_pool,
                                  g_mix_pre, g_mix_post, g_ffn_pre, g_ffn_post, win, wout, wpool, pscale,
                                  wg, wu, wd, wple, wpg, bb=8, dec=dec, nsub=2)

    kv_shape = (depth, -1, WINDOW, N_KV_HEADS, HEAD_DIM)
    return (yp, ys.reshape(dec_b, dec, D_MODEL),
            jnp.stack(kp_l).reshape(kv_shape), jnp.stack(vp_l).reshape(kv_shape), jnp.stack(sp_l),
            ks.reshape(kv_shape), vs.reshape(kv_shape), ss)
```

```python
import functools

import jax
import jax.numpy as jnp
from jax import lax
from jax.experimental import pallas as pl
from jax.experimental.pallas import tpu as pltpu

D_MODEL = 1024
DEPTH = 4
ATTN_WIDTH = 512
HEAD_DIM = 64
N_HEADS = 8
N_KV_HEADS = 2
GROUP = 4
KV_WIDTH = 128
WINDOW = 128
POOL_WIDTH = 512
POOL_WINDOWS = (2, 4, 8, 16)
POOL_GROUP_WIDTH = 128
POOL_STATE = 15
IN_WIDTH = 1280
D_FF = 2816
PLE_DIM = 256
EPS = 1e-6

K_OFF = ATTN_WIDTH
V_OFF = ATTN_WIDTH + KV_WIDTH
U_OFF = ATTN_WIDTH + 2 * KV_WIDTH
LANES = 128
POOL_PAD = 16
POOL_HEAD = 8 + POOL_PAD
LOG2E = 1.4426950408889634

BF16 = jnp.bfloat16
F32 = jnp.float32
VMEM_LIMIT = 56 * 1024 * 1024


def _rms(x, g):
    return x * lax.rsqrt(jnp.mean(x * x, axis=-1, keepdims=True) + EPS) * g


def _sigmoid(x):
    return 1.0 / (1.0 + jnp.exp(-x))


def _dot(a, b):
    return jnp.dot(a, b, preferred_element_type=F32)


def _dot_t(a, b):
    return lax.dot_general(a, b, (((1,), (1,)), ((), ())), preferred_element_type=F32)


def _const_spec(shape, layer=None):
    if layer is None:
        return pl.BlockSpec(shape, lambda *_: (0,) * len(shape), pipeline_mode=pl.Buffered(1))
    return pl.BlockSpec((None,) + shape, lambda *_: (layer,) + (0,) * len(shape),
                        pipeline_mode=pl.Buffered(1))


def _mix_prompt_kernel(sinks_ref, x_ref, gpre_ref, gpost_ref, win_ref, wout_ref, wpool_ref, pscale_ref,
                       y_ref, klast_ref, vlast_ref, plast_ref,
                       z_ref, kext_ref, vt_ref, uext_ref, ps1_ref, ps2_ref, mix_ref, bias_ref, *, tm, layer):
    s = pl.program_id(1)
    nblk = tm // WINDOW

    @pl.when((pl.program_id(0) == 0) & (s == 0))
    def _():
        c = lax.broadcasted_iota(jnp.int32, (2 * WINDOW, WINDOW), 0)
        r = lax.broadcasted_iota(jnp.int32, (2 * WINDOW, WINDOW), 1)
        dist = r + WINDOW - c
        ok = (dist >= 0) & (dist < WINDOW)
        bias_ref[0] = jnp.where(ok, 0.0, -jnp.inf)
        bias_ref[1] = jnp.where(ok & (c >= WINDOW), 0.0, -jnp.inf)

    @pl.when(s == 0)
    def _():
        kext_ref[0:WINDOW, :] = jnp.zeros((WINDOW, KV_WIDTH), F32)
        vt_ref[:, 0:WINDOW] = jnp.zeros((KV_WIDTH, WINDOW), F32)
        uext_ref[0:POOL_HEAD, :] = jnp.zeros((POOL_HEAD, POOL_WIDTH), F32)
        ps1_ref[0:8, :] = jnp.zeros((8, POOL_GROUP_WIDTH), F32)
        ps2_ref[0:8, :] = jnp.zeros((8, POOL_GROUP_WIDTH), F32)

    lane = lax.broadcasted_iota(jnp.int32, (WINDOW, LANES), 1)
    lo = lane < HEAD_DIM
    lo2 = jnp.concatenate([lo, lo], axis=0)
    row8 = lax.broadcasted_iota(jnp.int32, (8, GROUP * WINDOW), 0)
    kcol = lax.broadcasted_iota(jnp.int32, (HEAD_DIM, 2 * WINDOW), 1)
    ones = jnp.ones((HEAD_DIM, 2 * WINDOW), F32)
    half = tm // 2

    def project_in(hf):
        hr = slice(hf * half, (hf + 1) * half)
        h = _rms(x_ref[0, hr, :], gpre_ref[...]).astype(BF16)
        z_ref[hr, :] = _dot(h, win_ref[...])
        uext_ref[POOL_HEAD + hf * half:POOL_HEAD + (hf + 1) * half, :] = z_ref[hr, U_OFF:U_OFF + POOL_WIDTH]
        kext_ref[WINDOW + hf * half:WINDOW + (hf + 1) * half, :] = z_ref[hr, K_OFF:K_OFF + KV_WIDTH]
        for n in range(hf * half // WINDOW, (hf + 1) * half // WINDOW):
            r0 = n * WINDOW
            vt_ref[:, WINDOW + r0:2 * WINDOW + r0] = z_ref[r0:r0 + WINDOW, V_OFF:V_OFF + KV_WIDTH].T

    def attend(n, kv):
        r0 = n * WINDOW
        rows = slice(r0, r0 + WINDOW)
        kcat = kext_ref[r0:r0 + 2 * WINDOW, :]
        kswap = pltpu.roll(kcat, HEAD_DIM, 1)
        bias1 = bias_ref[jnp.where(s == 0, 1, 0)] if n == 0 else bias_ref[0]
        bias = jnp.concatenate([bias1] * GROUP, axis=1)
        kk = (jnp.where(lo2, kcat, kswap) if kv == 0 else jnp.where(lo2, kswap, kcat)).astype(BF16)
        pieces, sinks = [], []
        for g in range(GROUP):
            hd = kv * GROUP + g
            slab = hd // 2
            qs = z_ref[rows, slab * LANES:(slab + 1) * LANES] * (HEAD_DIM ** -0.5 * LOG2E)
            keep = lo if hd % 2 == 0 else jnp.logical_not(lo)
            pieces.append(jnp.where(keep, qs, 0.0).astype(BF16))
            sinks.append(jnp.full((1, WINDOW), sinks_ref[layer, hd] * LOG2E, F32))
        q4 = jnp.concatenate(pieces, axis=0)
        sink = jnp.concatenate(sinks, axis=1)
        st = _dot_t(kk, q4) + bias
        m = jnp.maximum(jnp.max(st, axis=0, keepdims=True), sink)
        p = jnp.exp2(st - m)
        top = jnp.where(row8 == 0, jnp.exp2(sink - m), p[0:8])
        pb = jnp.concatenate([top, p[8:]], axis=0).astype(BF16)
        vth = vt_ref[kv * HEAD_DIM:(kv + 1) * HEAD_DIM, r0:r0 + 2 * WINDOW]
        a = jnp.concatenate([jnp.where(kcol == 0, 0.0, vth), ones], axis=0).astype(BF16)
        ot = _dot(a, pb)
        inv = 1.0 / ot[HEAD_DIM:HEAD_DIM + 8]
        on = ot[0:HEAD_DIM] * jnp.concatenate([inv] * (HEAD_DIM // 8), axis=0)
        for j in range(GROUP // 2):
            c0 = 2 * j * WINDOW
            pair = jnp.concatenate([on[:, c0:c0 + WINDOW], on[:, c0 + WINDOW:c0 + 2 * WINDOW]], axis=0)
            slab = kv * (GROUP // 2) + j
            mix_ref[rows, slab * LANES:(slab + 1) * LANES] = pair.T.astype(BF16)

    def pool(hf):
        t0 = hf * half
        pos = lax.broadcasted_iota(jnp.int32, (half, 1), 0) + (s * tm + t0)
        first = POOL_HEAD + t0
        ext = slice(first - POOL_PAD, first + half)
        for gi, w in enumerate(POOL_WINDOWS):
            cols = slice(gi * POOL_GROUP_WIDTH, (gi + 1) * POOL_GROUP_WIDTH)
            src, src_cols, span, bufs = uext_ref, cols, 1, [ps1_ref, ps2_ref]
            while 2 * span < w:
                dst = bufs[0]
                dst[ext, :] = src[ext, src_cols] + src[first - POOL_PAD - span:first + half - span, src_cols]
                src, src_cols, span, bufs = dst, slice(None), 2 * span, bufs[::-1]
            tot = src[first:first + half, src_cols] + src[first - span:first - span + half, src_cols]
            cnt = jnp.minimum(w, pos + 1).astype(F32)
            dlt = (tot / cnt - uext_ref[first:first + half, cols]).astype(BF16)
            yp = _dot(dlt, wpool_ref[gi]) * pscale_ref[:, cols]
            mix_ref[t0:t0 + half, ATTN_WIDTH + gi * POOL_GROUP_WIDTH:ATTN_WIDTH + (gi + 1) * POOL_GROUP_WIDTH] = (
                yp.astype(BF16))

    def project_out(hf):
        hr = slice(hf * half, (hf + 1) * half)
        mixed = _dot(mix_ref[hr, :], wout_ref[...])
        y_ref[0, hr, :] = x_ref[0, hr, :] + _rms(mixed, gpost_ref[...])

    per_half = half // WINDOW
    blocks = [(n, kv) for n in range(nblk) for kv in range(N_KV_HEADS)]
    project_in(0)
    for i, (n, kv) in enumerate(blocks):
        attend(n, kv)
        if i == 0:
            project_in(1)
        if i == len(blocks) // 2:
            project_out(0)
        if kv == N_KV_HEADS - 1 and (n + 1) % per_half == 0:
            pool(n // per_half)
    project_out(1)

    @pl.when(s == pl.num_programs(1) - 1)
    def _():
        klast_ref[0] = z_ref[tm - WINDOW:tm, K_OFF:K_OFF + KV_WIDTH]
        vlast_ref[0] = z_ref[tm - WINDOW:tm, V_OFF:V_OFF + KV_WIDTH]
        plast_ref[0] = uext_ref[POOL_HEAD + tm - POOL_PAD:POOL_HEAD + tm, :]

    kext_ref[0:WINDOW, :] = kext_ref[tm:tm + WINDOW, :]
    vt_ref[:, 0:WINDOW] = vt_ref[:, tm:tm + WINDOW]
    uext_ref[POOL_HEAD - POOL_PAD:POOL_HEAD, :] = uext_ref[POOL_HEAD + tm - POOL_PAD:POOL_HEAD + tm, :]


def _mix_prompt(x, sinks, gpre, gpost, win, wout, wpool, pscale, *, layer, tm):
    b, s, _ = x.shape
    kern = functools.partial(_mix_prompt_kernel, tm=tm, layer=layer)
    return pl.pallas_call(
        kern,
        grid=(b, s // tm),
        in_specs=[
            pl.BlockSpec(memory_space=pltpu.SMEM),
            pl.BlockSpec((1, tm, D_MODEL), lambda i, j: (i, j, 0)),
            _const_spec((1, D_MODEL), layer),
            _const_spec((1, D_MODEL), layer),
            _const_spec((D_MODEL, IN_WIDTH), layer),
            _const_spec((D_MODEL, D_MODEL), layer),
            _const_spec((len(POOL_WINDOWS), POOL_GROUP_WIDTH, POOL_GROUP_WIDTH), layer),
            _const_spec((1, POOL_WIDTH), layer),
        ],
        out_specs=[
            pl.BlockSpec((1, tm, D_MODEL), lambda i, j: (i, j, 0)),
            pl.BlockSpec((1, WINDOW, KV_WIDTH), lambda i, j: (i, 0, 0)),
            pl.BlockSpec((1, WINDOW, KV_WIDTH), lambda i, j: (i, 0, 0)),
            pl.BlockSpec((1, POOL_PAD, POOL_WIDTH), lambda i, j: (i, 0, 0)),
        ],
        out_shape=[
            jax.ShapeDtypeStruct((b, s, D_MODEL), F32),
            jax.ShapeDtypeStruct((b, WINDOW, KV_WIDTH), F32),
            jax.ShapeDtypeStruct((b, WINDOW, KV_WIDTH), F32),
            jax.ShapeDtypeStruct((b, POOL_PAD, POOL_WIDTH), F32),
        ],
        scratch_shapes=[
            pltpu.VMEM((tm, IN_WIDTH), F32),
            pltpu.VMEM((WINDOW + tm, KV_WIDTH), F32),
            pltpu.VMEM((KV_WIDTH, WINDOW + tm), F32),
            pltpu.VMEM((POOL_HEAD + tm, POOL_WIDTH), F32),
            pltpu.VMEM((POOL_HEAD + tm, POOL_GROUP_WIDTH), F32),
            pltpu.VMEM((POOL_HEAD + tm, POOL_GROUP_WIDTH), F32),
            pltpu.VMEM((tm, D_MODEL), BF16),
            pltpu.VMEM((2, 2 * WINDOW, WINDOW), F32),
        ],
        compiler_params=pltpu.CompilerParams(
            dimension_semantics=("arbitrary", "arbitrary"), vmem_limit_bytes=VMEM_LIMIT),
        name=f"mix_prompt_{layer}",
    )(sinks, x, gpre, gpost, win, wout, wpool, pscale)


def _ffn_rows(x, p, gpre_ref, gpost_ref, wg_ref, wu_ref, wd_ref, wple_ref, wpg_ref):
    f = _rms(x, gpre_ref[...]).astype(BF16)
    g = _dot(f, wg_ref[...])
    u = _dot(f, wu_ref[...])
    a = (g * _sigmoid(g) * u).astype(BF16)
    d = _dot(a, wd_ref[...])
    x = x + _rms(d, gpost_ref[...])
    gate = _sigmoid(_dot(x.astype(BF16), wpg_ref[...]))
    return x + gate * _dot(p.astype(BF16), wple_ref[...])


def _ffn_kernel(x_ref, p_ref, gpre_ref, gpost_ref, wg_ref, wu_ref, wd_ref, wple_ref, wpg_ref, o_ref, *, nsub):
    sub = x_ref.shape[0] // nsub
    rows = [slice(i * sub, (i + 1) * sub) for i in range(nsub)]

    def head(i):
        return _rms(x_ref[rows[i], :], gpre_ref[...]).astype(BF16)

    def tail(i, d):
        x = x_ref[rows[i], :] + _rms(d, gpost_ref[...])
        gate = _sigmoid(_dot(x.astype(BF16), wpg_ref[...]))
        o_ref[rows[i], :] = x + gate * _dot(p_ref[rows[i], :].astype(BF16), wple_ref[...])

    f = head(0)
    pending = None
    for i in range(nsub):
        g = _dot(f, wg_ref[...])
        if pending is not None:
            tail(*pending)
        u = _dot(f, wu_ref[...])
        if i + 1 < nsub:
            f = head(i + 1)
        a = (g * _sigmoid(g) * u).astype(BF16)
        pending = (i, _dot(a, wd_ref[...]))
    tail(*pending)


def _ffn(x, p, gpre, gpost, wg, wu, wd, wple, wpg, *, layer, tm, nsub):
    n = x.shape[0]
    return pl.pallas_call(
        functools.partial(_ffn_kernel, nsub=nsub),
        grid=(n // tm,),
        in_specs=[
            pl.BlockSpec((tm, D_MODEL), lambda i: (i, 0)),
            pl.BlockSpec((None, tm, PLE_DIM), lambda i: (layer, i, 0)),
            _const_spec((1, D_MODEL), layer),
            _const_spec((1, D_MODEL), layer),
            _const_spec((D_MODEL, D_FF), layer),
            _const_spec((D_MODEL, D_FF), layer),
            _const_spec((D_FF, D_MODEL), layer),
            _const_spec((PLE_DIM, D_MODEL), layer),
            _const_spec((D_MODEL, D_MODEL), layer),
        ],
        out_specs=pl.BlockSpec((tm, D_MODEL), lambda i: (i, 0)),
        out_shape=jax.ShapeDtypeStruct((n, D_MODEL), F32),
        compiler_params=pltpu.CompilerParams(
            dimension_semantics=("arbitrary",), vmem_limit_bytes=VMEM_LIMIT),
        name=f"ffn_prompt_{layer}",
    )(x, p, gpre, gpost, wg, wu, wd, wple, wpg)


def _layer_spec(shape, single=False):
    kwargs = dict(pipeline_mode=pl.Buffered(1)) if single else {}
    return pl.BlockSpec((None,) + shape, lambda l, c: (l,) + (0,) * len(shape), **kwargs)


def _sample_kernel(sinks_ref, xs_ref, ps_ref, ck_ref, cv_ref, st_ref, gmpre_ref, gmpost_ref, gfpre_ref, gfpost_ref,
                   win_ref, wout_ref, wpool_ref, pscale_ref, wg_ref, wu_ref, wd_ref, wple_ref, wpg_ref,
                   y_ref, ko_ref, vo_ref, po_ref, z_ref, mix_ref, ext_ref, dlt_ref, *, bb, dec, nsub):
    l = pl.program_id(0)
    c = pl.program_id(1)
    n_rows = y_ref.shape[0]

    @pl.when((l == 0) & (c == 0))
    def _():
        y_ref[...] = xs_ref[...]

    @pl.when(c == 0)
    def _():
        h = _rms(y_ref[...], gmpre_ref[...]).astype(BF16)
        z_ref[...] = _dot(h, win_ref[...])

    per_tile = 8 // dec
    base = pl.multiple_of(c * (bb * dec), 8)
    lane = lax.broadcasted_iota(jnp.int32, (8, LANES), 1)
    lo = lane < HEAD_DIM
    hi = jnp.logical_not(lo)
    rows64 = N_HEADS * 8
    trow = lax.broadcasted_iota(jnp.int32, (rows64, LANES), 0) & 7
    tstep = trow % dec
    tbat = trow // dec
    klane = lax.broadcasted_iota(jnp.int32, (rows64, LANES), 1)
    valid_c = klane >= tstep + 1
    tstep1 = tstep[:, 0:1]

    for t in range(bb // per_tile):
        rows = pl.ds(base + 8 * t, 8)
        q = z_ref[rows, 0:ATTN_WIDTH] * (HEAD_DIM ** -0.5)
        knew = z_ref[rows, K_OFF:K_OFF + KV_WIDTH]
        vnew = z_ref[rows, V_OFF:V_OFF + KV_WIDTH]
        unew = z_ref[rows, U_OFF:U_OFF + POOL_WIDTH]
        pieces, sinks = [], []
        for hd in range(N_HEADS):
            slab, half, kv = hd // 2, hd % 2, hd // GROUP
            qs = q[:, slab * LANES:(slab + 1) * LANES]
            src = qs if half == kv else pltpu.roll(qs, HEAD_DIM, 1)
            pieces.append(jnp.where(lo if kv == 0 else hi, src, 0.0))
            sinks.append(jnp.full((8, 1), sinks_ref[l, hd], F32))
        lhs = jnp.concatenate(pieces, axis=0)
        sink = jnp.concatenate(sinks, axis=0)
        lhs_b = lhs.astype(BF16)

        sc = None
        for i in range(per_tile):
            kb = ck_ref[per_tile * t + i].astype(BF16)
            si = _dot_t(lhs_b, kb)
            sc = si if sc is None else jnp.where(tbat == i, si, sc)
        sc = jnp.where(valid_c, sc, -jnp.inf)

        sn, vn_rows = [], []
        for i in range(dec):
            kr, vr = None, None
            for bi in range(per_tile):
                kbrow = jnp.broadcast_to(knew[bi * dec + i:bi * dec + i + 1, :], (rows64, LANES))
                vbrow = jnp.broadcast_to(vnew[bi * dec + i:bi * dec + i + 1, :], (rows64, LANES))
                kr = kbrow if kr is None else jnp.where(tbat == bi, kbrow, kr)
                vr = vbrow if vr is None else jnp.where(tbat == bi, vbrow, vr)
            s_i = jnp.sum(lhs * kr, axis=-1, keepdims=True)
            sn.append(jnp.where(tstep1 >= i, s_i, -jnp.inf))
            vn_rows.append(vr)

        m = jnp.maximum(jnp.max(sc, axis=-1, keepdims=True), sink)
        for s_i in sn:
            m = jnp.maximum(m, s_i)
        p = jnp.exp(sc - m)
        denom = jnp.sum(p, axis=-1, keepdims=True) + jnp.exp(sink - m)
        pb = p.astype(BF16)
        o = None
        for i in range(per_tile):
            vb = cv_ref[per_tile * t + i].astype(BF16)
            oi = _dot(pb, vb)
            o = oi if o is None else jnp.where(tbat == i, oi, o)
        for s_i, vr in zip(sn, vn_rows):
            p_i = jnp.exp(s_i - m)
            denom = denom + p_i
            o = o + p_i * vr
        o = o / denom
        for slab in range(N_HEADS // 2):
            kv = (2 * slab) // GROUP
            even = o[16 * slab:16 * slab + 8]
            odd = o[16 * slab + 8:16 * slab + 16]
            if kv == 0:
                res = jnp.where(lo, even, pltpu.roll(odd, HEAD_DIM, 1))
            else:
                res = jnp.where(lo, pltpu.roll(even, HEAD_DIM, 1), odd)
            mix_ref[rows, slab * LANES:(slab + 1) * LANES] = res

        for bi in range(per_tile):
            b = per_tile * t + bi
            own = slice(bi * dec, (bi + 1) * dec)
            ko_ref[b, 0:WINDOW - dec, :] = ck_ref[b, dec:WINDOW, :]
            vo_ref[b, 0:WINDOW - dec, :] = cv_ref[b, dec:WINDOW, :]
            ko_ref[b, WINDOW - dec:WINDOW, :] = knew[own]
            vo_ref[b, WINDOW - dec:WINDOW, :] = vnew[own]
            ext_ref[0:POOL_STATE, :] = st_ref[b]
            ext_ref[POOL_STATE:POOL_STATE + dec, :] = unew[own]
            for gi, w in enumerate(POOL_WINDOWS):
                cols = slice(gi * POOL_GROUP_WIDTH, (gi + 1) * POOL_GROUP_WIDTH)
                cur = ext_ref[POOL_STATE:POOL_STATE + dec, cols]
                tot = cur
                for j in range(1, w):
                    tot = tot + ext_ref[POOL_STATE - j:POOL_STATE - j + dec, cols]
                dlt_ref[b * dec:(b + 1) * dec, cols] = tot / float(w) - cur
            po_ref[b] = ext_ref[dec:dec + POOL_STATE, :]

    chunk = pl.ds(base, bb * dec)
    for gi in range(len(POOL_WINDOWS)):
        cols = slice(gi * POOL_GROUP_WIDTH, (gi + 1) * POOL_GROUP_WIDTH)
        yp = _dot(dlt_ref[:, cols].astype(BF16), wpool_ref[gi]) * pscale_ref[:, cols]
        mix_ref[chunk, ATTN_WIDTH + gi * POOL_GROUP_WIDTH:ATTN_WIDTH + (gi + 1) * POOL_GROUP_WIDTH] = yp

    @pl.when(c == pl.num_programs(1) - 1)
    def _():
        sub = n_rows // nsub
        for i in range(nsub):
            rs = slice(i * sub, (i + 1) * sub)
            mixed = _dot(mix_ref[rs, :].astype(BF16), wout_ref[...])
            x = y_ref[rs, :] + _rms(mixed, gmpost_ref[...])
            y_ref[rs, :] = _ffn_rows(x, ps_ref[rs, :], gfpre_ref, gfpost_ref, wg_ref, wu_ref, wd_ref, wple_ref,
                                     wpg_ref)


def _sample_path(xs, ps, sinks, ck, cv, st, gmpre, gmpost, gfpre, gfpost, win, wout, wpool, pscale,
                 wg, wu, wd, wple, wpg, *, bb, dec, nsub):
    depth, nb = ck.shape[0], ck.shape[1]
    n = nb * dec
    kern = functools.partial(_sample_kernel, bb=bb, dec=dec, nsub=nsub)
    cache_spec = pl.BlockSpec((None, bb, WINDOW, KV_WIDTH), lambda l, c: (l, c, 0, 0))
    state_spec = pl.BlockSpec((None, bb, POOL_STATE, POOL_WIDTH), lambda l, c: (l, c, 0, 0))
    return pl.pallas_call(
        kern,
        grid=(depth, nb // bb),
        in_specs=[
            pl.BlockSpec(memory_space=pltpu.SMEM),
            pl.BlockSpec((n, D_MODEL), lambda l, c: (0, 0), pipeline_mode=pl.Buffered(1)),
            _layer_spec((n, PLE_DIM)),
            cache_spec, cache_spec, state_spec,
            _layer_spec((1, D_MODEL)), _layer_spec((1, D_MODEL)), _layer_spec((1, D_MODEL)), _layer_spec((1, D_MODEL)),
            _layer_spec((D_MODEL, IN_WIDTH), single=True),
            _layer_spec((D_MODEL, D_MODEL), single=True),
            _layer_spec((len(POOL_WINDOWS), POOL_GROUP_WIDTH, POOL_GROUP_WIDTH)),
            _layer_spec((1, POOL_WIDTH)),
            _layer_spec((D_MODEL, D_FF), single=True),
            _layer_spec((D_MODEL, D_FF), single=True),
            _layer_spec((D_FF, D_MODEL), single=True),
            _layer_spec((PLE_DIM, D_MODEL), single=True),
            _layer_spec((D_MODEL, D_MODEL), single=True),
        ],
        out_specs=[
            pl.BlockSpec((n, D_MODEL), lambda l, c: (0, 0)),
            cache_spec, cache_spec, state_spec,
        ],
        out_shape=[
            jax.ShapeDtypeStruct((n, D_MODEL), F32),
            jax.ShapeDtypeStruct((depth, nb, WINDOW, KV_WIDTH), F32),
            jax.ShapeDtypeStruct((depth, nb, WINDOW, KV_WIDTH), F32),
            jax.ShapeDtypeStruct((depth, nb, POOL_STATE, POOL_WIDTH), F32),
        ],
        scratch_shapes=[
            pltpu.VMEM((n, IN_WIDTH), F32),
            pltpu.VMEM((n, D_MODEL), F32),
            pltpu.VMEM((POOL_STATE + 9, POOL_WIDTH), F32),
            pltpu.VMEM((bb * dec, POOL_WIDTH), F32),
        ],
        compiler_params=pltpu.CompilerParams(
            dimension_semantics=("arbitrary", "arbitrary"), vmem_limit_bytes=VMEM_LIMIT),
        name="sample_path",
    )(sinks, xs, ps, ck, cv, st, gmpre, gmpost, gfpre, gfpost, win, wout, wpool, pscale, wg, wu, wd, wple, wpg)


def kernel(x_prompt, x_sample, p_prompt, p_sample, cache_k, cache_v, state_pool, norm_mix_pre, norm_mix_post,
           norm_ffn_pre, norm_ffn_post, w_in, w_out, attn_sinks, w_pool, pool_scale, w_gate, w_up, w_down, w_ple,
           w_ple_gate):
    depth, nbat, seq, _ = p_prompt.shape
    dec_b, dec = x_sample.shape[0], x_sample.shape[1]
    assert 8 % dec == 0 and seq % WINDOW == 0

    win, wout, wpool = w_in.astype(BF16), w_out.astype(BF16), w_pool.astype(BF16)
    wg, wu, wd = w_gate.astype(BF16), w_up.astype(BF16), w_down.astype(BF16)
    wple, wpg = w_ple.astype(BF16), w_ple_gate.astype(BF16)
    g_mix_pre = norm_mix_pre.reshape(depth, 1, D_MODEL)
    g_mix_post = norm_mix_post.reshape(depth, 1, D_MODEL)
    g_ffn_pre = norm_ffn_pre.reshape(depth, 1, D_MODEL)
    g_ffn_post = norm_ffn_post.reshape(depth, 1, D_MODEL)
    pscale = pool_scale.reshape(depth, 1, POOL_WIDTH)

    pp = p_prompt.reshape(depth, nbat * seq, PLE_DIM)
    ps = p_sample.reshape(depth, dec_b * dec, PLE_DIM)
    ck = cache_k.reshape(depth, dec_b, WINDOW, KV_WIDTH)
    cv = cache_v.reshape(depth, dec_b, WINDOW, KV_WIDTH)

    yp = x_prompt
    kp_l, vp_l, sp_l = [], [], []
    for i in range(depth):
        yp, kp, vp, sp = _mix_prompt(yp, attn_sinks, g_mix_pre, g_mix_post, win, wout, wpool, pscale,
                                     layer=i, tm=512)
        yp = _ffn(yp.reshape(nbat * seq, D_MODEL), pp, g_ffn_pre, g_ffn_post, wg, wu, wd, wple, wpg,
                  layer=i, tm=512, nsub=1).reshape(nbat, seq, D_MODEL)
        kp_l.append(kp)
        vp_l.append(vp)
        sp_l.append(sp[:, POOL_PAD - POOL_STATE:])

    ys, ks, vs, ss = _sample_path(x_sample.reshape(dec_b * dec, D_MODEL), ps, attn_sinks, ck, cv, state_pool,
                                  g_mix_pre, g_mix_post, g_ffn_pre, g_ffn_post, win, wout, wpool, pscale,
                                  wg, wu, wd, wple, wpg, bb=8, dec=dec, nsub=2)

    kv_shape = (depth, -1, WINDOW, N_KV_HEADS, HEAD_DIM)
    return (yp, ys.reshape(dec_b, dec, D_MODEL),
            jnp.stack(kp_l).reshape(kv_shape), jnp.stack(vp_l).reshape(kv_shape), jnp.stack(sp_l),
            ks.reshape(kv_shape), vs.reshape(kv_shape), ss)
```

```python
import functools

import jax
import jax.numpy as jnp
from jax import lax
from jax.experimental import pallas as pl
from jax.experimental.pallas import tpu as pltpu

D_MODEL = 1024
DEPTH = 4
ATTN_WIDTH = 512
HEAD_DIM = 64
N_HEADS = 8
N_KV_HEADS = 2
GROUP = 4
KV_WIDTH = 128
WINDOW = 128
POOL_WIDTH = 512
POOL_WINDOWS = (2, 4, 8, 16)
POOL_GROUP_WIDTH = 128
POOL_STATE = 15
IN_WIDTH = 1280
D_FF = 2816
PLE_DIM = 256
EPS = 1e-6

K_OFF = ATTN_WIDTH
V_OFF = ATTN_WIDTH + KV_WIDTH
U_OFF = ATTN_WIDTH + 2 * KV_WIDTH
LANES = 128
POOL_PAD = 16
POOL_HEAD = 8 + POOL_PAD
LOG2E = 1.4426950408889634

BF16 = jnp.bfloat16
F32 = jnp.float32
VMEM_LIMIT = 56 * 1024 * 1024


def _rms(x, g):
    return x * lax.rsqrt(jnp.mean(x * x, axis=-1, keepdims=True) + EPS) * g


def _sigmoid(x):
    return 1.0 / (1.0 + jnp.exp(-x))


def _dot(a, b):
    return jnp.dot(a, b, preferred_element_type=F32)


def _dot_t(a, b):
    return lax.dot_general(a, b, (((1,), (1,)), ((), ())), preferred_element_type=F32)


def _const_spec(shape, layer=None):
    if layer is None:
        return pl.BlockSpec(shape, lambda *_: (0,) * len(shape), pipeline_mode=pl.Buffered(1))
    return pl.BlockSpec((None,) + shape, lambda *_: (layer,) + (0,) * len(shape),
                        pipeline_mode=pl.Buffered(1))


def _mix_prompt_kernel(sinks_ref, x_ref, gpre_ref, gpost_ref, win_ref, wout_ref, wpool_ref, pscale_ref,
                       y_ref, klast_ref, vlast_ref, plast_ref,
                       z_ref, kext_ref, vt_ref, uext_ref, ps1_ref, ps2_ref, mix_ref, bias_ref, *, tm, piece, layer):
    s = pl.program_id(1)
    nblk = tm // WINDOW

    @pl.when((pl.program_id(0) == 0) & (s == 0))
    def _():
        c = lax.broadcasted_iota(jnp.int32, (2 * WINDOW, WINDOW), 0)
        r = lax.broadcasted_iota(jnp.int32, (2 * WINDOW, WINDOW), 1)
        dist = r + WINDOW - c
        ok = (dist >= 0) & (dist < WINDOW)
        bias_ref[0] = jnp.where(ok, 0.0, -jnp.inf)
        bias_ref[1] = jnp.where(ok & (c >= WINDOW), 0.0, -jnp.inf)

    @pl.when(s == 0)
    def _():
        kext_ref[0:WINDOW, :] = jnp.zeros((WINDOW, KV_WIDTH), F32)
        vt_ref[:, 0:WINDOW] = jnp.zeros((KV_WIDTH, WINDOW), F32)
        uext_ref[0:POOL_HEAD, :] = jnp.zeros((POOL_HEAD, POOL_WIDTH), F32)
        ps1_ref[0:8, :] = jnp.zeros((8, POOL_GROUP_WIDTH), F32)
        ps2_ref[0:8, :] = jnp.zeros((8, POOL_GROUP_WIDTH), F32)

    lane = lax.broadcasted_iota(jnp.int32, (WINDOW, LANES), 1)
    lo = lane < HEAD_DIM
    lo2 = jnp.concatenate([lo, lo], axis=0)
    row8 = lax.broadcasted_iota(jnp.int32, (8, GROUP * WINDOW), 0)
    kcol = lax.broadcasted_iota(jnp.int32, (HEAD_DIM, 2 * WINDOW), 1)
    ones = jnp.ones((HEAD_DIM, 2 * WINDOW), F32)
    half = piece

    def project_in(hf):
        hr = slice(hf * half, (hf + 1) * half)
        h = _rms(x_ref[0, hr, :], gpre_ref[...]).astype(BF16)
        z_ref[hr, :] = _dot(h, win_ref[...])
        uext_ref[POOL_HEAD + hf * half:POOL_HEAD + (hf + 1) * half, :] = z_ref[hr, U_OFF:U_OFF + POOL_WIDTH]
        kext_ref[WINDOW + hf * half:WINDOW + (hf + 1) * half, :] = z_ref[hr, K_OFF:K_OFF + KV_WIDTH]
        for n in range(hf * half // WINDOW, (hf + 1) * half // WINDOW):
            r0 = n * WINDOW
            vt_ref[:, WINDOW + r0:2 * WINDOW + r0] = z_ref[r0:r0 + WINDOW, V_OFF:V_OFF + KV_WIDTH].T

    def attend_stages(blocks):
        st8 = {}

        def scores():
            for n in blocks:
                r0 = n * WINDOW
                rows = slice(r0, r0 + WINDOW)
                kcat = kext_ref[r0:r0 + 2 * WINDOW, :]
                kswap = pltpu.roll(kcat, HEAD_DIM, 1)
                bias1 = bias_ref[jnp.where(s == 0, 1, 0)] if n == 0 else bias_ref[0]
                bias = jnp.concatenate([bias1] * GROUP, axis=1)
                for kv in range(N_KV_HEADS):
                    kk = (jnp.where(lo2, kcat, kswap) if kv == 0 else jnp.where(lo2, kswap, kcat)).astype(BF16)
                    pieces, sinks = [], []
                    for g in range(GROUP):
                        hd = kv * GROUP + g
                        slab = hd // 2
                        qs = z_ref[rows, slab * LANES:(slab + 1) * LANES] * (HEAD_DIM ** -0.5 * LOG2E)
                        keep = lo if hd % 2 == 0 else jnp.logical_not(lo)
                        pieces.append(jnp.where(keep, qs, 0.0).astype(BF16))
                        sinks.append(jnp.full((1, WINDOW), sinks_ref[layer, hd] * LOG2E, F32))
                    q4 = jnp.concatenate(pieces, axis=0)
                    st8[n, kv, "sink"] = jnp.concatenate(sinks, axis=1)
                    st8[n, kv, "st"] = _dot_t(kk, q4) + bias

        def softmax():
            for n in blocks:
                for kv in range(N_KV_HEADS):
                    st, sink = st8.pop((n, kv, "st")), st8.pop((n, kv, "sink"))
                    m = jnp.maximum(jnp.max(st, axis=0, keepdims=True), sink)
                    p = jnp.exp2(st - m)
                    top = jnp.where(row8 == 0, jnp.exp2(sink - m), p[0:8])
                    st8[n, kv, "p"] = jnp.concatenate([top, p[8:]], axis=0).astype(BF16)

        def values():
            for n in blocks:
                r0 = n * WINDOW
                for kv in range(N_KV_HEADS):
                    vth = vt_ref[kv * HEAD_DIM:(kv + 1) * HEAD_DIM, r0:r0 + 2 * WINDOW]
                    a = jnp.concatenate([jnp.where(kcol == 0, 0.0, vth), ones], axis=0).astype(BF16)
                    st8[n, kv, "ot"] = _dot(a, st8.pop((n, kv, "p")))

        def store():
            for n in blocks:
                rows = slice(n * WINDOW, (n + 1) * WINDOW)
                for kv in range(N_KV_HEADS):
                    ot = st8.pop((n, kv, "ot"))
                    inv = 1.0 / ot[HEAD_DIM:HEAD_DIM + 8]
                    on = ot[0:HEAD_DIM] * jnp.concatenate([inv] * (HEAD_DIM // 8), axis=0)
                    for j in range(GROUP // 2):
                        c0 = 2 * j * WINDOW
                        pair = jnp.concatenate([on[:, c0:c0 + WINDOW], on[:, c0 + WINDOW:c0 + 2 * WINDOW]], axis=0)
                        slab = kv * (GROUP // 2) + j
                        mix_ref[rows, slab * LANES:(slab + 1) * LANES] = pair.T.astype(BF16)

        return scores, softmax, values, store

    def pool(hf):
        t0 = hf * half
        pos = lax.broadcasted_iota(jnp.int32, (half, 1), 0) + (s * tm + t0)
        first = POOL_HEAD + t0
        ext = slice(first - POOL_PAD, first + half)
        for gi, w in enumerate(POOL_WINDOWS):
            cols = slice(gi * POOL_GROUP_WIDTH, (gi + 1) * POOL_GROUP_WIDTH)
            src, src_cols, span, bufs = uext_ref, cols, 1, [ps1_ref, ps2_ref]
            while 2 * span < w:
                dst = bufs[0]
                dst[ext, :] = src[ext, src_cols] + src[first - POOL_PAD - span:first + half - span, src_cols]
                src, src_cols, span, bufs = dst, slice(None), 2 * span, bufs[::-1]
            tot = src[first:first + half, src_cols] + src[first - span:first - span + half, src_cols]
            cnt = jnp.minimum(w, pos + 1).astype(F32)
            dlt = (tot / cnt - uext_ref[first:first + half, cols]).astype(BF16)
            yp = _dot(dlt, wpool_ref[gi]) * pscale_ref[:, cols]
            mix_ref[t0:t0 + half, ATTN_WIDTH + gi * POOL_GROUP_WIDTH:ATTN_WIDTH + (gi + 1) * POOL_GROUP_WIDTH] = (
                yp.astype(BF16))

    def project_out(hf):
        hr = slice(hf * half, (hf + 1) * half)
        mixed = _dot(mix_ref[hr, :], wout_ref[...])
        y_ref[0, hr, :] = x_ref[0, hr, :] + _rms(mixed, gpost_ref[...])

    per_half = half // WINDOW
    n_piece = tm // half
    project_in(0)
    for hf in range(n_piece):
        scores, softmax, values, store = attend_stages(range(hf * per_half, (hf + 1) * per_half))
        scores()
        if hf + 1 < n_piece:
            project_in(hf + 1)
        if hf > 0:
            project_out(hf - 1)
        softmax()
        values()
        store()
        pool(hf)
    project_out(n_piece - 1)

    @pl.when(s == pl.num_programs(1) - 1)
    def _():
        klast_ref[0] = z_ref[tm - WINDOW:tm, K_OFF:K_OFF + KV_WIDTH]
        vlast_ref[0] = z_ref[tm - WINDOW:tm, V_OFF:V_OFF + KV_WIDTH]
        plast_ref[0] = uext_ref[POOL_HEAD + tm - POOL_PAD:POOL_HEAD + tm, :]

    kext_ref[0:WINDOW, :] = kext_ref[tm:tm + WINDOW, :]
    vt_ref[:, 0:WINDOW] = vt_ref[:, tm:tm + WINDOW]
    uext_ref[POOL_HEAD - POOL_PAD:POOL_HEAD, :] = uext_ref[POOL_HEAD + tm - POOL_PAD:POOL_HEAD + tm, :]


def _mix_prompt(x, sinks, gpre, gpost, win, wout, wpool, pscale, *, layer, tm, piece):
    b, s, _ = x.shape
    kern = functools.partial(_mix_prompt_kernel, tm=tm, piece=piece, layer=layer)
    return pl.pallas_call(
        kern,
        grid=(b, s // tm),
        in_specs=[
            pl.BlockSpec(memory_space=pltpu.SMEM),
            pl.BlockSpec((1, tm, D_MODEL), lambda i, j: (i, j, 0)),
            _const_spec((1, D_MODEL), layer),
            _const_spec((1, D_MODEL), layer),
            _const_spec((D_MODEL, IN_WIDTH), layer),
            _const_spec((D_MODEL, D_MODEL), layer),
            _const_spec((len(POOL_WINDOWS), POOL_GROUP_WIDTH, POOL_GROUP_WIDTH), layer),
            _const_spec((1, POOL_WIDTH), layer),
        ],
        out_specs=[
            pl.BlockSpec((1, tm, D_MODEL), lambda i, j: (i, j, 0)),
            pl.BlockSpec((1, WINDOW, KV_WIDTH), lambda i, j: (i, 0, 0)),
            pl.BlockSpec((1, WINDOW, KV_WIDTH), lambda i, j: (i, 0, 0)),
            pl.BlockSpec((1, POOL_PAD, POOL_WIDTH), lambda i, j: (i, 0, 0)),
        ],
        out_shape=[
            jax.ShapeDtypeStruct((b, s, D_MODEL), F32),
            jax.ShapeDtypeStruct((b, WINDOW, KV_WIDTH), F32),
            jax.ShapeDtypeStruct((b, WINDOW, KV_WIDTH), F32),
            jax.ShapeDtypeStruct((b, POOL_PAD, POOL_WIDTH), F32),
        ],
        scratch_shapes=[
            pltpu.VMEM((tm, IN_WIDTH), F32),
            pltpu.VMEM((WINDOW + tm, KV_WIDTH), F32),
            pltpu.VMEM((KV_WIDTH, WINDOW + tm), F32),
            pltpu.VMEM((POOL_HEAD + tm, POOL_WIDTH), F32),
            pltpu.VMEM((POOL_HEAD + tm, POOL_GROUP_WIDTH), F32),
            pltpu.VMEM((POOL_HEAD + tm, POOL_GROUP_WIDTH), F32),
            pltpu.VMEM((tm, D_MODEL), BF16),
            pltpu.VMEM((2, 2 * WINDOW, WINDOW), F32),
        ],
        compiler_params=pltpu.CompilerParams(
            dimension_semantics=("arbitrary", "arbitrary"), vmem_limit_bytes=VMEM_LIMIT),
        name=f"mix_prompt_{layer}",
    )(sinks, x, gpre, gpost, win, wout, wpool, pscale)


def _ffn_rows(x, p, gpre_ref, gpost_ref, wg_ref, wu_ref, wd_ref, wple_ref, wpg_ref):
    f = _rms(x, gpre_ref[...]).astype(BF16)
    g = _dot(f, wg_ref[...])
    u = _dot(f, wu_ref[...])
    a = (g * _sigmoid(g) * u).astype(BF16)
    d = _dot(a, wd_ref[...])
    x = x + _rms(d, gpost_ref[...])
    gate = _sigmoid(_dot(x.astype(BF16), wpg_ref[...]))
    return x + gate * _dot(p.astype(BF16), wple_ref[...])


def _ffn_kernel(x_ref, p_ref, gpre_ref, gpost_ref, wg_ref, wu_ref, wd_ref, wple_ref, wpg_ref, o_ref, *, nsub):
    sub = x_ref.shape[0] // nsub
    rows = [slice(i * sub, (i + 1) * sub) for i in range(nsub)]

    def head(i):
        return _rms(x_ref[rows[i], :], gpre_ref[...]).astype(BF16)

    def tail(i, d):
        x = x_ref[rows[i], :] + _rms(d, gpost_ref[...])
        gate = _sigmoid(_dot(x.astype(BF16), wpg_ref[...]))
        o_ref[rows[i], :] = x + gate * _dot(p_ref[rows[i], :].astype(BF16), wple_ref[...])

    f = head(0)
    pending = None
    for i in range(nsub):
        g = _dot(f, wg_ref[...])
        if pending is not None:
            tail(*pending)
        u = _dot(f, wu_ref[...])
        if i + 1 < nsub:
            f = head(i + 1)
        a = (g * _sigmoid(g) * u).astype(BF16)
        pending = (i, _dot(a, wd_ref[...]))
    tail(*pending)


def _ffn(x, p, gpre, gpost, wg, wu, wd, wple, wpg, *, layer, tm, nsub):
    n = x.shape[0]
    return pl.pallas_call(
        functools.partial(_ffn_kernel, nsub=nsub),
        grid=(n // tm,),
        in_specs=[
            pl.BlockSpec((tm, D_MODEL), lambda i: (i, 0)),
            pl.BlockSpec((None, tm, PLE_DIM), lambda i: (layer, i, 0)),
            _const_spec((1, D_MODEL), layer),
            _const_spec((1, D_MODEL), layer),
            _const_spec((D_MODEL, D_FF), layer),
            _const_spec((D_MODEL, D_FF), layer),
            _const_spec((D_FF, D_MODEL), layer),
            _const_spec((PLE_DIM, D_MODEL), layer),
            _const_spec((D_MODEL, D_MODEL), layer),
        ],
        out_specs=pl.BlockSpec((tm, D_MODEL), lambda i: (i, 0)),
        out_shape=jax.ShapeDtypeStruct((n, D_MODEL), F32),
        compiler_params=pltpu.CompilerParams(
            dimension_semantics=("arbitrary",), vmem_limit_bytes=VMEM_LIMIT),
        name=f"ffn_prompt_{layer}",
    )(x, p, gpre, gpost, wg, wu, wd, wple, wpg)


def _layer_spec(shape, single=False):
    kwargs = dict(pipeline_mode=pl.Buffered(1)) if single else {}
    return pl.BlockSpec((None,) + shape, lambda l, c: (l,) + (0,) * len(shape), **kwargs)


def _sample_kernel(sinks_ref, xs_ref, ps_ref, ck_ref, cv_ref, st_ref, gmpre_ref, gmpost_ref, gfpre_ref, gfpost_ref,
                   win_ref, wout_ref, wpool_ref, pscale_ref, wg_ref, wu_ref, wd_ref, wple_ref, wpg_ref,
                   y_ref, ko_ref, vo_ref, po_ref, z_ref, mix_ref, ext_ref, dlt_ref, *, bb, dec, nsub):
    l = pl.program_id(0)
    c = pl.program_id(1)
    n_rows = y_ref.shape[0]

    @pl.when((l == 0) & (c == 0))
    def _():
        y_ref[...] = xs_ref[...]

    @pl.when(c == 0)
    def _():
        h = _rms(y_ref[...], gmpre_ref[...]).astype(BF16)
        z_ref[...] = _dot(h, win_ref[...])

    per_tile = 8 // dec
    base = pl.multiple_of(c * (bb * dec), 8)
    lane = lax.broadcasted_iota(jnp.int32, (8, LANES), 1)
    lo = lane < HEAD_DIM
    hi = jnp.logical_not(lo)
    rows64 = N_HEADS * 8
    trow = lax.broadcasted_iota(jnp.int32, (rows64, LANES), 0) & 7
    tstep = trow % dec
    tbat = trow // dec
    klane = lax.broadcasted_iota(jnp.int32, (rows64, LANES), 1)
    valid_c = klane >= tstep + 1
    tstep1 = tstep[:, 0:1]

    tiles = range(bb // per_tile)
    st8 = {}

    for t in tiles:
        rows = pl.ds(base + 8 * t, 8)
        q = z_ref[rows, 0:ATTN_WIDTH] * (HEAD_DIM ** -0.5)
        knew = z_ref[rows, K_OFF:K_OFF + KV_WIDTH]
        vnew = z_ref[rows, V_OFF:V_OFF + KV_WIDTH]
        st8[t, "knew"], st8[t, "vnew"] = knew, vnew
        st8[t, "unew"] = z_ref[rows, U_OFF:U_OFF + POOL_WIDTH]
        pieces, sinks = [], []
        for hd in range(N_HEADS):
            slab, half, kv = hd // 2, hd % 2, hd // GROUP
            qs = q[:, slab * LANES:(slab + 1) * LANES]
            src = qs if half == kv else pltpu.roll(qs, HEAD_DIM, 1)
            pieces.append(jnp.where(lo if kv == 0 else hi, src, 0.0))
            sinks.append(jnp.full((8, 1), sinks_ref[l, hd], F32))
        lhs = jnp.concatenate(pieces, axis=0)
        st8[t, "sink"] = jnp.concatenate(sinks, axis=0)
        lhs_b = lhs.astype(BF16)
        sc = None
        for i in range(per_tile):
            kb = ck_ref[per_tile * t + i].astype(BF16)
            si = _dot_t(lhs_b, kb)
            sc = si if sc is None else jnp.where(tbat == i, si, sc)
        st8[t, "sc"] = jnp.where(valid_c, sc, -jnp.inf)
        sn, vn_rows = [], []
        for i in range(dec):
            kr, vr = None, None
            for bi in range(per_tile):
                kbrow = jnp.broadcast_to(knew[bi * dec + i:bi * dec + i + 1, :], (rows64, LANES))
                vbrow = jnp.broadcast_to(vnew[bi * dec + i:bi * dec + i + 1, :], (rows64, LANES))
                kr = kbrow if kr is None else jnp.where(tbat == bi, kbrow, kr)
                vr = vbrow if vr is None else jnp.where(tbat == bi, vbrow, vr)
            s_i = jnp.sum(lhs * kr, axis=-1, keepdims=True)
            sn.append(jnp.where(tstep1 >= i, s_i, -jnp.inf))
            vn_rows.append(vr)
        st8[t, "sn"], st8[t, "vn"] = sn, vn_rows

    for t in tiles:
        sc, sink, sn = st8.pop((t, "sc")), st8.pop((t, "sink")), st8[t, "sn"]
        m = jnp.maximum(jnp.max(sc, axis=-1, keepdims=True), sink)
        for s_i in sn:
            m = jnp.maximum(m, s_i)
        p = jnp.exp(sc - m)
        st8[t, "denom"] = jnp.sum(p, axis=-1, keepdims=True) + jnp.exp(sink - m)
        st8[t, "m"] = m
        st8[t, "pb"] = p.astype(BF16)

    for t in tiles:
        pb = st8.pop((t, "pb"))
        o = None
        for i in range(per_tile):
            vb = cv_ref[per_tile * t + i].astype(BF16)
            oi = _dot(pb, vb)
            o = oi if o is None else jnp.where(tbat == i, oi, o)
        st8[t, "o"] = o

    for t in tiles:
        rows = pl.ds(base + 8 * t, 8)
        o, m, denom = st8.pop((t, "o")), st8.pop((t, "m")), st8.pop((t, "denom"))
        for s_i, vr in zip(st8.pop((t, "sn")), st8.pop((t, "vn"))):
            p_i = jnp.exp(s_i - m)
            denom = denom + p_i
            o = o + p_i * vr
        o = o / denom
        for slab in range(N_HEADS // 2):
            kv = (2 * slab) // GROUP
            even = o[16 * slab:16 * slab + 8]
            odd = o[16 * slab + 8:16 * slab + 16]
            if kv == 0:
                res = jnp.where(lo, even, pltpu.roll(odd, HEAD_DIM, 1))
            else:
                res = jnp.where(lo, pltpu.roll(even, HEAD_DIM, 1), odd)
            mix_ref[rows, slab * LANES:(slab + 1) * LANES] = res

    for t in tiles:
        knew, vnew, unew = st8.pop((t, "knew")), st8.pop((t, "vnew")), st8.pop((t, "unew"))
        for bi in range(per_tile):
            b = per_tile * t + bi
            own = slice(bi * dec, (bi + 1) * dec)
            ko_ref[b, 0:WINDOW - dec, :] = ck_ref[b, dec:WINDOW, :]
            vo_ref[b, 0:WINDOW - dec, :] = cv_ref[b, dec:WINDOW, :]
            ko_ref[b, WINDOW - dec:WINDOW, :] = knew[own]
            vo_ref[b, WINDOW - dec:WINDOW, :] = vnew[own]
            ext_ref[b, 0:POOL_STATE, :] = st_ref[b]
            ext_ref[b, POOL_STATE:POOL_STATE + dec, :] = unew[own]
            for gi, w in enumerate(POOL_WINDOWS):
                cols = slice(gi * POOL_GROUP_WIDTH, (gi + 1) * POOL_GROUP_WIDTH)
                cur = ext_ref[b, POOL_STATE:POOL_STATE + dec, cols]
                tot = cur
                for j in range(1, w):
                    tot = tot + ext_ref[b, POOL_STATE - j:POOL_STATE - j + dec, cols]
                dlt_ref[b * dec:(b + 1) * dec, cols] = tot / float(w) - cur
            po_ref[b] = ext_ref[b, dec:dec + POOL_STATE, :]

    chunk = pl.ds(base, bb * dec)
    for gi in range(len(POOL_WINDOWS)):
        cols = slice(gi * POOL_GROUP_WIDTH, (gi + 1) * POOL_GROUP_WIDTH)
        yp = _dot(dlt_ref[:, cols].astype(BF16), wpool_ref[gi]) * pscale_ref[:, cols]
        mix_ref[chunk, ATTN_WIDTH + gi * POOL_GROUP_WIDTH:ATTN_WIDTH + (gi + 1) * POOL_GROUP_WIDTH] = yp

    @pl.when(c == pl.num_programs(1) - 1)
    def _():
        sub = n_rows // nsub
        for i in range(nsub):
            rs = slice(i * sub, (i + 1) * sub)
            mixed = _dot(mix_ref[rs, :].astype(BF16), wout_ref[...])
            x = y_ref[rs, :] + _rms(mixed, gmpost_ref[...])
            y_ref[rs, :] = _ffn_rows(x, ps_ref[rs, :], gfpre_ref, gfpost_ref, wg_ref, wu_ref, wd_ref, wple_ref,
                                     wpg_ref)


def _sample_path(xs, ps, sinks, ck, cv, st, gmpre, gmpost, gfpre, gfpost, win, wout, wpool, pscale,
                 wg, wu, wd, wple, wpg, *, bb, dec, nsub):
    depth, nb = ck.shape[0], ck.shape[1]
    n = nb * dec
    kern = functools.partial(_sample_kernel, bb=bb, dec=dec, nsub=nsub)
    cache_spec = pl.BlockSpec((None, bb, WINDOW, KV_WIDTH), lambda l, c: (l, c, 0, 0))
    state_spec = pl.BlockSpec((None, bb, POOL_STATE, POOL_WIDTH), lambda l, c: (l, c, 0, 0))
    return pl.pallas_call(
        kern,
        grid=(depth, nb // bb),
        in_specs=[
            pl.BlockSpec(memory_space=pltpu.SMEM),
            pl.BlockSpec((n, D_MODEL), lambda l, c: (0, 0), pipeline_mode=pl.Buffered(1)),
            _layer_spec((n, PLE_DIM)),
            cache_spec, cache_spec, state_spec,
            _layer_spec((1, D_MODEL)), _layer_spec((1, D_MODEL)), _layer_spec((1, D_MODEL)), _layer_spec((1, D_MODEL)),
            _layer_spec((D_MODEL, IN_WIDTH), single=True),
            _layer_spec((D_MODEL, D_MODEL), single=True),
            _layer_spec((len(POOL_WINDOWS), POOL_GROUP_WIDTH, POOL_GROUP_WIDTH)),
            _layer_spec((1, POOL_WIDTH)),
            _layer_spec((D_MODEL, D_FF), single=True),
            _layer_spec((D_MODEL, D_FF), single=True),
            _layer_spec((D_FF, D_MODEL), single=True),
            _layer_spec((PLE_DIM, D_MODEL), single=True),
            _layer_spec((D_MODEL, D_MODEL), single=True),
        ],
        out_specs=[
            pl.BlockSpec((n, D_MODEL), lambda l, c: (0, 0)),
            cache_spec, cache_spec, state_spec,
        ],
        out_shape=[
            jax.ShapeDtypeStruct((n, D_MODEL), F32),
            jax.ShapeDtypeStruct((depth, nb, WINDOW, KV_WIDTH), F32),
            jax.ShapeDtypeStruct((depth, nb, WINDOW, KV_WIDTH), F32),
            jax.ShapeDtypeStruct((depth, nb, POOL_STATE, POOL_WIDTH), F32),
        ],
        scratch_shapes=[
            pltpu.VMEM((n, IN_WIDTH), F32),
            pltpu.VMEM((n, D_MODEL), F32),
            pltpu.VMEM((bb, POOL_STATE + 9, POOL_WIDTH), F32),
            pltpu.VMEM((bb * dec, POOL_WIDTH), F32),
        ],
        compiler_params=pltpu.CompilerParams(
            dimension_semantics=("arbitrary", "arbitrary"), vmem_limit_bytes=VMEM_LIMIT),
        name="sample_path",
    )(sinks, xs, ps, ck, cv, st, gmpre, gmpost, gfpre, gfpost, win, wout, wpool, pscale, wg, wu, wd, wple, wpg)


def kernel(x_prompt, x_sample, p_prompt, p_sample, cache_k, cache_v, state_pool, norm_mix_pre, norm_mix_post,
           norm_ffn_pre, norm_ffn_post, w_in, w_out, attn_sinks, w_pool, pool_scale, w_gate, w_up, w_down, w_ple,
           w_ple_gate):
    depth, nbat, seq, _ = p_prompt.shape
    dec_b, dec = x_sample.shape[0], x_sample.shape[1]
    assert 8 % dec == 0 and seq % WINDOW == 0

    win, wout, wpool = w_in.astype(BF16), w_out.astype(BF16), w_pool.astype(BF16)
    wg, wu, wd = w_gate.astype(BF16), w_up.astype(BF16), w_down.astype(BF16)
    wple, wpg = w_ple.astype(BF16), w_ple_gate.astype(BF16)
    g_mix_pre = norm_mix_pre.reshape(depth, 1, D_MODEL)
    g_mix_post = norm_mix_post.reshape(depth, 1, D_MODEL)
    g_ffn_pre = norm_ffn_pre.reshape(depth, 1, D_MODEL)
    g_ffn_post = norm_ffn_post.reshape(depth, 1, D_MODEL)
    pscale = pool_scale.reshape(depth, 1, POOL_WIDTH)

    pp = p_prompt.reshape(depth, nbat * seq, PLE_DIM)
    ps = p_sample.reshape(depth, dec_b * dec, PLE_DIM)
    ck = cache_k.reshape(depth, dec_b, WINDOW, KV_WIDTH)
    cv = cache_v.reshape(depth, dec_b, WINDOW, KV_WIDTH)

    yp = x_prompt
    kp_l, vp_l, sp_l = [], [], []
    for i in range(depth):
        yp, kp, vp, sp = _mix_prompt(yp, attn_sinks, g_mix_pre, g_mix_post, win, wout, wpool, pscale,
                                     layer=i, tm=1024, piece=256)
        yp = _ffn(yp.reshape(nbat * seq, D_MODEL), pp, g_ffn_pre, g_ffn_post, wg, wu, wd, wple, wpg,
                  layer=i, tm=512, nsub=1).reshape(nbat, seq, D_MODEL)
        kp_l.append(kp)
        vp_l.append(vp)
        sp_l.append(sp[:, POOL_PAD - POOL_STATE:])

    ys, ks, vs, ss = _sample_path(x_sample.reshape(dec_b * dec, D_MODEL), ps, attn_sinks, ck, cv, state_pool,
                                  g_mix_pre, g_mix_post, g_ffn_pre, g_ffn_post, win, wout, wpool, pscale,
                                  wg, wu, wd, wple, wpg, bb=8, dec=dec, nsub=2)

    kv_shape = (depth, -1, WINDOW, N_KV_HEADS, HEAD_DIM)
    return (yp, ys.reshape(dec_b, dec, D_MODEL),
            jnp.stack(kp_l).reshape(kv_shape), jnp.stack(vp_l).reshape(kv_shape), jnp.stack(sp_l),
            ks.reshape(kv_shape), vs.reshape(kv_shape), ss)
```

```python
import functools

import jax
import jax.numpy as jnp
from jax import lax
from jax.experimental import pallas as pl
from jax.experimental.pallas import tpu as pltpu

D_MODEL = 1024
DEPTH = 4
ATTN_WIDTH = 512
HEAD_DIM = 64
N_HEADS = 8
N_KV_HEADS = 2
GROUP = 4
KV_WIDTH = 128
WINDOW = 128
POOL_WIDTH = 512
POOL_WINDOWS = (2, 4, 8, 16)
POOL_GROUP_WIDTH = 128
POOL_STATE = 15
IN_WIDTH = 1280
D_FF = 2816
PLE_DIM = 256
EPS = 1e-6

K_OFF = ATTN_WIDTH
V_OFF = ATTN_WIDTH + KV_WIDTH
U_OFF = ATTN_WIDTH + 2 * KV_WIDTH
LANES = 128
POOL_PAD = 16
POOL_HEAD = 8 + POOL_PAD
LOG2E = 1.4426950408889634

BF16 = jnp.bfloat16
F32 = jnp.float32
VMEM_LIMIT = 56 * 1024 * 1024


def _rms(x, g):
    return x * lax.rsqrt(jnp.mean(x * x, axis=-1, keepdims=True) + EPS) * g


def _sigmoid(x):
    return 1.0 / (1.0 + jnp.exp(-x))


def _dot(a, b):
    return jnp.dot(a, b, preferred_element_type=F32)


def _dot_t(a, b):
    return lax.dot_general(a, b, (((1,), (1,)), ((), ())), preferred_element_type=F32)


def _const_spec(shape, layer=None):
    if layer is None:
        return pl.BlockSpec(shape, lambda *_: (0,) * len(shape), pipeline_mode=pl.Buffered(1))
    return pl.BlockSpec((None,) + shape, lambda *_: (layer,) + (0,) * len(shape),
                        pipeline_mode=pl.Buffered(1))


def _mix_prompt_kernel(sinks_ref, x_ref, gpre_ref, gpost_ref, win_ref, wout_ref, wpool_ref, pscale_ref,
                       y_ref, klast_ref, vlast_ref, plast_ref,
                       z_ref, kext_ref, vt_ref, uext_ref, ps1_ref, ps2_ref, mix_ref, bias_ref, *, tm, piece, layer):
    s = pl.program_id(1)
    nblk = tm // WINDOW

    @pl.when((pl.program_id(0) == 0) & (s == 0))
    def _():
        c = lax.broadcasted_iota(jnp.int32, (2 * WINDOW, WINDOW), 0)
        r = lax.broadcasted_iota(jnp.int32, (2 * WINDOW, WINDOW), 1)
        dist = r + WINDOW - c
        ok = (dist >= 0) & (dist < WINDOW)
        bias_ref[0] = jnp.where(ok, 0.0, -jnp.inf)
        bias_ref[1] = jnp.where(ok & (c >= WINDOW), 0.0, -jnp.inf)

    @pl.when(s == 0)
    def _():
        kext_ref[0:WINDOW, :] = jnp.zeros((WINDOW, KV_WIDTH), F32)
        vt_ref[:, 0:WINDOW] = jnp.zeros((KV_WIDTH, WINDOW), F32)
        uext_ref[0:POOL_HEAD, :] = jnp.zeros((POOL_HEAD, POOL_WIDTH), F32)
        ps1_ref[0:8, :] = jnp.zeros((8, POOL_GROUP_WIDTH), F32)
        ps2_ref[0:8, :] = jnp.zeros((8, POOL_GROUP_WIDTH), F32)

    lane = lax.broadcasted_iota(jnp.int32, (WINDOW, LANES), 1)
    lo = lane < HEAD_DIM
    lo2 = jnp.concatenate([lo, lo], axis=0)
    row8 = lax.broadcasted_iota(jnp.int32, (8, GROUP * WINDOW), 0)
    kcol = lax.broadcasted_iota(jnp.int32, (HEAD_DIM, 2 * WINDOW), 1)
    ones = jnp.ones((HEAD_DIM, 2 * WINDOW), F32)
    half = piece

    def project_in(hf):
        hr = slice(hf * half, (hf + 1) * half)
        h = _rms(x_ref[0, hr, :], gpre_ref[...]).astype(BF16)
        z_ref[hr, :] = _dot(h, win_ref[...])
        uext_ref[POOL_HEAD + hf * half:POOL_HEAD + (hf + 1) * half, :] = z_ref[hr, U_OFF:U_OFF + POOL_WIDTH]
        kext_ref[WINDOW + hf * half:WINDOW + (hf + 1) * half, :] = z_ref[hr, K_OFF:K_OFF + KV_WIDTH]
        for n in range(hf * half // WINDOW, (hf + 1) * half // WINDOW):
            r0 = n * WINDOW
            vt_ref[:, WINDOW + r0:2 * WINDOW + r0] = z_ref[r0:r0 + WINDOW, V_OFF:V_OFF + KV_WIDTH].T

    def attend_stages(blocks):
        st8 = {}

        def scores():
            for n in blocks:
                r0 = n * WINDOW
                rows = slice(r0, r0 + WINDOW)
                kcat = kext_ref[r0:r0 + 2 * WINDOW, :]
                kswap = pltpu.roll(kcat, HEAD_DIM, 1)
                bias1 = bias_ref[jnp.where(s == 0, 1, 0)] if n == 0 else bias_ref[0]
                bias = jnp.concatenate([bias1] * GROUP, axis=1)
                for kv in range(N_KV_HEADS):
                    kk = (jnp.where(lo2, kcat, kswap) if kv == 0 else jnp.where(lo2, kswap, kcat)).astype(BF16)
                    pieces, sinks = [], []
                    for g in range(GROUP):
                        hd = kv * GROUP + g
                        slab = hd // 2
                        qs = z_ref[rows, slab * LANES:(slab + 1) * LANES] * (HEAD_DIM ** -0.5 * LOG2E)
                        keep = lo if hd % 2 == 0 else jnp.logical_not(lo)
                        pieces.append(jnp.where(keep, qs, 0.0).astype(BF16))
                        sinks.append(jnp.full((1, WINDOW), sinks_ref[layer, hd] * LOG2E, F32))
                    q4 = jnp.concatenate(pieces, axis=0)
                    st8[n, kv, "sink"] = jnp.concatenate(sinks, axis=1)
                    st8[n, kv, "st"] = _dot_t(kk, q4) + bias

        def softmax():
            for n in blocks:
                for kv in range(N_KV_HEADS):
                    st, sink = st8.pop((n, kv, "st")), st8.pop((n, kv, "sink"))
                    m = jnp.maximum(jnp.max(st, axis=0, keepdims=True), sink)
                    p = jnp.exp2(st - m)
                    top = jnp.where(row8 == 0, jnp.exp2(sink - m), p[0:8])
                    st8[n, kv, "p"] = jnp.concatenate([top, p[8:]], axis=0).astype(BF16)

        def values():
            for n in blocks:
                r0 = n * WINDOW
                for kv in range(N_KV_HEADS):
                    vth = vt_ref[kv * HEAD_DIM:(kv + 1) * HEAD_DIM, r0:r0 + 2 * WINDOW]
                    a = jnp.concatenate([jnp.where(kcol == 0, 0.0, vth), ones], axis=0).astype(BF16)
                    st8[n, kv, "ot"] = _dot(a, st8.pop((n, kv, "p")))

        def store():
            for n in blocks:
                rows = slice(n * WINDOW, (n + 1) * WINDOW)
                for kv in range(N_KV_HEADS):
                    ot = st8.pop((n, kv, "ot"))
                    inv = 1.0 / ot[HEAD_DIM:HEAD_DIM + 8]
                    on = ot[0:HEAD_DIM] * jnp.concatenate([inv] * (HEAD_DIM // 8), axis=0)
                    for j in range(GROUP // 2):
                        c0 = 2 * j * WINDOW
                        pair = jnp.concatenate([on[:, c0:c0 + WINDOW], on[:, c0 + WINDOW:c0 + 2 * WINDOW]], axis=0)
                        slab = kv * (GROUP // 2) + j
                        mix_ref[rows, slab * LANES:(slab + 1) * LANES] = pair.T.astype(BF16)

        return scores, softmax, values, store

    def pool(hf):
        t0 = hf * half
        pos = lax.broadcasted_iota(jnp.int32, (half, 1), 0) + (s * tm + t0)
        first = POOL_HEAD + t0
        ext = slice(first - POOL_PAD, first + half)
        for gi, w in enumerate(POOL_WINDOWS):
            cols = slice(gi * POOL_GROUP_WIDTH, (gi + 1) * POOL_GROUP_WIDTH)
            src, src_cols, span, bufs = uext_ref, cols, 1, [ps1_ref, ps2_ref]
            while 2 * span < w:
                dst = bufs[0]
                dst[ext, :] = src[ext, src_cols] + src[first - POOL_PAD - span:first + half - span, src_cols]
                src, src_cols, span, bufs = dst, slice(None), 2 * span, bufs[::-1]
            tot = src[first:first + half, src_cols] + src[first - span:first - span + half, src_cols]
            cnt = jnp.minimum(w, pos + 1).astype(F32)
            dlt = (tot / cnt - uext_ref[first:first + half, cols]).astype(BF16)
            yp = _dot(dlt, wpool_ref[gi]) * pscale_ref[:, cols]
            mix_ref[t0:t0 + half, ATTN_WIDTH + gi * POOL_GROUP_WIDTH:ATTN_WIDTH + (gi + 1) * POOL_GROUP_WIDTH] = (
                yp.astype(BF16))

    def project_out(hf):
        hr = slice(hf * half, (hf + 1) * half)
        mixed = _dot(mix_ref[hr, :], wout_ref[...])
        y_ref[0, hr, :] = x_ref[0, hr, :] + _rms(mixed, gpost_ref[...])

    per_half = half // WINDOW
    n_piece = tm // half
    project_in(0)
    for hf in range(n_piece):
        scores, softmax, values, store = attend_stages(range(hf * per_half, (hf + 1) * per_half))
        scores()
        if hf + 1 < n_piece:
            project_in(hf + 1)
        if hf > 0:
            project_out(hf - 1)
        softmax()
        values()
        store()
        pool(hf)
    project_out(n_piece - 1)

    @pl.when(s == pl.num_programs(1) - 1)
    def _():
        klast_ref[0] = z_ref[tm - WINDOW:tm, K_OFF:K_OFF + KV_WIDTH]
        vlast_ref[0] = z_ref[tm - WINDOW:tm, V_OFF:V_OFF + KV_WIDTH]
        plast_ref[0] = uext_ref[POOL_HEAD + tm - POOL_PAD:POOL_HEAD + tm, :]

    kext_ref[0:WINDOW, :] = kext_ref[tm:tm + WINDOW, :]
    vt_ref[:, 0:WINDOW] = vt_ref[:, tm:tm + WINDOW]
    uext_ref[POOL_HEAD - POOL_PAD:POOL_HEAD, :] = uext_ref[POOL_HEAD + tm - POOL_PAD:POOL_HEAD + tm, :]


def _mix_prompt(x, sinks, gpre, gpost, win, wout, wpool, pscale, *, layer, tm, piece):
    b, s, _ = x.shape
    kern = functools.partial(_mix_prompt_kernel, tm=tm, piece=piece, layer=layer)
    return pl.pallas_call(
        kern,
        grid=(b, s // tm),
        in_specs=[
            pl.BlockSpec(memory_space=pltpu.SMEM),
            pl.BlockSpec((1, tm, D_MODEL), lambda i, j: (i, j, 0)),
            _const_spec((1, D_MODEL), layer),
            _const_spec((1, D_MODEL), layer),
            _const_spec((D_MODEL, IN_WIDTH), layer),
            _const_spec((D_MODEL, D_MODEL), layer),
            _const_spec((len(POOL_WINDOWS), POOL_GROUP_WIDTH, POOL_GROUP_WIDTH), layer),
            _const_spec((1, POOL_WIDTH), layer),
        ],
        out_specs=[
            pl.BlockSpec((1, tm, D_MODEL), lambda i, j: (i, j, 0)),
            pl.BlockSpec((1, WINDOW, KV_WIDTH), lambda i, j: (i, 0, 0)),
            pl.BlockSpec((1, WINDOW, KV_WIDTH), lambda i, j: (i, 0, 0)),
            pl.BlockSpec((1, POOL_PAD, POOL_WIDTH), lambda i, j: (i, 0, 0)),
        ],
        out_shape=[
            jax.ShapeDtypeStruct((b, s, D_MODEL), F32),
            jax.ShapeDtypeStruct((b, WINDOW, KV_WIDTH), F32),
            jax.ShapeDtypeStruct((b, WINDOW, KV_WIDTH), F32),
            jax.ShapeDtypeStruct((b, POOL_PAD, POOL_WIDTH), F32),
        ],
        scratch_shapes=[
            pltpu.VMEM((tm, IN_WIDTH), F32),
            pltpu.VMEM((WINDOW + tm, KV_WIDTH), F32),
            pltpu.VMEM((KV_WIDTH, WINDOW + tm), F32),
            pltpu.VMEM((POOL_HEAD + tm, POOL_WIDTH), F32),
            pltpu.VMEM((POOL_HEAD + tm, POOL_GROUP_WIDTH), F32),
            pltpu.VMEM((POOL_HEAD + tm, POOL_GROUP_WIDTH), F32),
            pltpu.VMEM((tm, D_MODEL), BF16),
            pltpu.VMEM((2, 2 * WINDOW, WINDOW), F32),
        ],
        compiler_params=pltpu.CompilerParams(
            dimension_semantics=("arbitrary", "arbitrary"), vmem_limit_bytes=VMEM_LIMIT),
        name=f"mix_prompt_{layer}",
    )(sinks, x, gpre, gpost, win, wout, wpool, pscale)


def _ffn_rows(x, p, gpre_ref, gpost_ref, wg_ref, wu_ref, wd_ref, wple_ref, wpg_ref):
    f = _rms(x, gpre_ref[...]).astype(BF16)
    g = _dot(f, wg_ref[...])
    u = _dot(f, wu_ref[...])
    a = (g * _sigmoid(g) * u).astype(BF16)
    d = _dot(a, wd_ref[...])
    x = x + _rms(d, gpost_ref[...])
    gate = _sigmoid(_dot(x.astype(BF16), wpg_ref[...]))
    return x + gate * _dot(p.astype(BF16), wple_ref[...])


def _ffn_kernel(x_ref, p_ref, gpre_ref, gpost_ref, wg_ref, wu_ref, wd_ref, wple_ref, wpg_ref, o_ref, *, nsub):
    sub = x_ref.shape[0] // nsub
    rows = [slice(i * sub, (i + 1) * sub) for i in range(nsub)]

    def head(i):
        return _rms(x_ref[rows[i], :], gpre_ref[...]).astype(BF16)

    def tail(i, d):
        x = x_ref[rows[i], :] + _rms(d, gpost_ref[...])
        gate = _sigmoid(_dot(x.astype(BF16), wpg_ref[...]))
        o_ref[rows[i], :] = x + gate * _dot(p_ref[rows[i], :].astype(BF16), wple_ref[...])

    f = head(0)
    pending = None
    for i in range(nsub):
        g = _dot(f, wg_ref[...])
        if pending is not None:
            tail(*pending)
        u = _dot(f, wu_ref[...])
        if i + 1 < nsub:
            f = head(i + 1)
        a = (g * _sigmoid(g) * u).astype(BF16)
        pending = (i, _dot(a, wd_ref[...]))
    tail(*pending)


def _ffn(x, p, gpre, gpost, wg, wu, wd, wple, wpg, *, layer, tm, nsub):
    n = x.shape[0]
    return pl.pallas_call(
        functools.partial(_ffn_kernel, nsub=nsub),
        grid=(n // tm,),
        in_specs=[
            pl.BlockSpec((tm, D_MODEL), lambda i: (i, 0)),
            pl.BlockSpec((None, tm, PLE_DIM), lambda i: (layer, i, 0)),
            _const_spec((1, D_MODEL), layer),
            _const_spec((1, D_MODEL), layer),
            _const_spec((D_MODEL, D_FF), layer),
            _const_spec((D_MODEL, D_FF), layer),
            _const_spec((D_FF, D_MODEL), layer),
            _const_spec((PLE_DIM, D_MODEL), layer),
            _const_spec((D_MODEL, D_MODEL), layer),
        ],
        out_specs=pl.BlockSpec((tm, D_MODEL), lambda i: (i, 0)),
        out_shape=jax.ShapeDtypeStruct((n, D_MODEL), F32),
        compiler_params=pltpu.CompilerParams(
            dimension_semantics=("arbitrary",), vmem_limit_bytes=VMEM_LIMIT),
        name=f"ffn_prompt_{layer}",
    )(x, p, gpre, gpost, wg, wu, wd, wple, wpg)


def _layer_spec(shape, single=False):
    kwargs = dict(pipeline_mode=pl.Buffered(1)) if single else {}
    return pl.BlockSpec((None,) + shape, lambda l, c: (l,) + (0,) * len(shape), **kwargs)


def _sample_kernel(sinks_ref, xs_ref, ps_ref, ck_ref, cv_ref, st_ref, gmpre_ref, gmpost_ref, gfpre_ref, gfpost_ref,
                   win_ref, wout_ref, wpool_ref, pscale_ref, wg_ref, wu_ref, wd_ref, wple_ref, wpg_ref,
                   y_ref, ko_ref, vo_ref, po_ref, z_ref, mix_ref, ustage_ref, dlt_ref, *, bb, dec, nsub):
    l = pl.program_id(0)
    c = pl.program_id(1)
    n_rows = y_ref.shape[0]

    @pl.when((l == 0) & (c == 0))
    def _():
        y_ref[...] = xs_ref[...]

    @pl.when(c == 0)
    def _():
        h = _rms(y_ref[...], gmpre_ref[...]).astype(BF16)
        z_ref[...] = _dot(h, win_ref[...])

    per_tile = 8 // dec
    base = pl.multiple_of(c * (bb * dec), 8)
    lane = lax.broadcasted_iota(jnp.int32, (8, LANES), 1)
    lo = lane < HEAD_DIM
    hi = jnp.logical_not(lo)
    rows64 = N_HEADS * 8
    ext_w = WINDOW + 16
    trow = lax.broadcasted_iota(jnp.int32, (rows64, ext_w), 0) & 7
    tstep = trow % dec
    tbat = trow // dec
    klane = lax.broadcasted_iota(jnp.int32, (rows64, ext_w), 1)
    fresh = klane - WINDOW
    valid = ((klane < WINDOW) & (klane >= tstep + 1)) | (
        (klane >= WINDOW) & (fresh < 8) & (fresh // dec == tbat) & (fresh % dec <= tstep))
    tbat_o = tbat[:, 0:LANES]
    zpad = jnp.zeros((ext_w - WINDOW - 8, KV_WIDTH), F32)

    tiles = range(bb // per_tile)
    st8 = {}

    for t in tiles:
        rows = pl.ds(base + 8 * t, 8)
        q = z_ref[rows, 0:ATTN_WIDTH] * (HEAD_DIM ** -0.5)
        knew = z_ref[rows, K_OFF:K_OFF + KV_WIDTH]
        st8[t, "knew"] = knew
        st8[t, "vnew"] = z_ref[rows, V_OFF:V_OFF + KV_WIDTH]
        st8[t, "unew"] = z_ref[rows, U_OFF:U_OFF + POOL_WIDTH]
        pieces, sinks = [], []
        for hd in range(N_HEADS):
            slab, half, kv = hd // 2, hd % 2, hd // GROUP
            qs = q[:, slab * LANES:(slab + 1) * LANES]
            src = qs if half == kv else pltpu.roll(qs, HEAD_DIM, 1)
            pieces.append(jnp.where(lo if kv == 0 else hi, src, 0.0))
            sinks.append(jnp.full((8, 1), sinks_ref[l, hd], F32))
        lhs_b = jnp.concatenate(pieces, axis=0).astype(BF16)
        st8[t, "sink"] = jnp.concatenate(sinks, axis=0)
        sc = None
        for i in range(per_tile):
            kx = jnp.concatenate([ck_ref[per_tile * t + i], knew, zpad], axis=0).astype(BF16)
            si = _dot_t(lhs_b, kx)
            sc = si if sc is None else jnp.where(tbat == i, si, sc)
        st8[t, "sc"] = jnp.where(valid, sc, -jnp.inf)

    for t in tiles:
        sc, sink = st8.pop((t, "sc")), st8.pop((t, "sink"))
        m = jnp.maximum(jnp.max(sc, axis=-1, keepdims=True), sink)
        p = jnp.exp(sc - m)
        st8[t, "denom"] = jnp.sum(p, axis=-1, keepdims=True) + jnp.exp(sink - m)
        st8[t, "pb"] = p.astype(BF16)

    for t in tiles:
        pb = st8.pop((t, "pb"))
        o = None
        for i in range(per_tile):
            vx = jnp.concatenate([cv_ref[per_tile * t + i], st8[t, "vnew"], zpad], axis=0).astype(BF16)
            oi = _dot(pb, vx)
            o = oi if o is None else jnp.where(tbat_o == i, oi, o)
        st8[t, "o"] = o

    for t in tiles:
        rows = pl.ds(base + 8 * t, 8)
        o = st8.pop((t, "o")) / st8.pop((t, "denom"))
        for slab in range(N_HEADS // 2):
            kv = (2 * slab) // GROUP
            even = o[16 * slab:16 * slab + 8]
            odd = o[16 * slab + 8:16 * slab + 16]
            if kv == 0:
                res = jnp.where(lo, even, pltpu.roll(odd, HEAD_DIM, 1))
            else:
                res = jnp.where(lo, pltpu.roll(even, HEAD_DIM, 1), odd)
            mix_ref[rows, slab * LANES:(slab + 1) * LANES] = res

    for t in tiles:
        knew, vnew, unew = st8.pop((t, "knew")), st8.pop((t, "vnew")), st8.pop((t, "unew"))
        for bi in range(per_tile):
            b = per_tile * t + bi
            own = slice(bi * dec, (bi + 1) * dec)
            ko_ref[b, 0:WINDOW - dec, :] = ck_ref[b, dec:WINDOW, :]
            vo_ref[b, 0:WINDOW - dec, :] = cv_ref[b, dec:WINDOW, :]
            ko_ref[b, WINDOW - dec:WINDOW, :] = knew[own]
            vo_ref[b, WINDOW - dec:WINDOW, :] = vnew[own]

    chunk = pl.ds(base, bb * dec)
    for r in range(POOL_STATE - dec):
        po_ref[r] = st_ref[r + dec]
    for gi, w in enumerate(POOL_WINDOWS):
        cols = slice(gi * POOL_GROUP_WIDTH, (gi + 1) * POOL_GROUP_WIDTH)
        ustage_ref[gi] = z_ref[chunk, U_OFF + gi * POOL_GROUP_WIDTH:U_OFF + (gi + 1) * POOL_GROUP_WIDTH]
        u = [ustage_ref[gi, pl.ds(j, bb, stride=dec), :] for j in range(dec)]
        suffix, acc = {}, None
        for m in range(1, w):
            row = st_ref[POOL_STATE - m, :, cols]
            acc = row if acc is None else acc + row
            suffix[m] = acc
        for j in range(dec):
            po_ref[POOL_STATE - dec + j, :, cols] = u[j]
            tot = suffix.get(w - 1 - j)
            for i in range(max(0, j - w + 1), j + 1):
                tot = u[i] if tot is None else tot + u[i]
            dlt_ref[gi, pl.ds(j, bb, stride=dec), :] = tot / float(w) - u[j]

    for gi in range(len(POOL_WINDOWS)):
        cols = slice(gi * POOL_GROUP_WIDTH, (gi + 1) * POOL_GROUP_WIDTH)
        yp = _dot(dlt_ref[gi].astype(BF16), wpool_ref[gi]) * pscale_ref[:, cols]
        mix_ref[chunk, ATTN_WIDTH + gi * POOL_GROUP_WIDTH:ATTN_WIDTH + (gi + 1) * POOL_GROUP_WIDTH] = yp

    @pl.when(c == pl.num_programs(1) - 1)
    def _():
        sub = n_rows // nsub
        for i in range(nsub):
            rs = slice(i * sub, (i + 1) * sub)
            mixed = _dot(mix_ref[rs, :].astype(BF16), wout_ref[...])
            x = y_ref[rs, :] + _rms(mixed, gmpost_ref[...])
            y_ref[rs, :] = _ffn_rows(x, ps_ref[rs, :], gfpre_ref, gfpost_ref, wg_ref, wu_ref, wd_ref, wple_ref,
                                     wpg_ref)


def _sample_path(xs, ps, sinks, ck, cv, st, gmpre, gmpost, gfpre, gfpost, win, wout, wpool, pscale,
                 wg, wu, wd, wple, wpg, *, bb, dec, nsub):
    depth, nb = ck.shape[0], ck.shape[1]
    n = nb * dec
    kern = functools.partial(_sample_kernel, bb=bb, dec=dec, nsub=nsub)
    cache_spec = pl.BlockSpec((None, bb, WINDOW, KV_WIDTH), lambda l, c: (l, c, 0, 0))
    state_spec = pl.BlockSpec((None, POOL_STATE, bb, POOL_WIDTH), lambda l, c: (l, 0, c, 0))
    return pl.pallas_call(
        kern,
        grid=(depth, nb // bb),
        in_specs=[
            pl.BlockSpec(memory_space=pltpu.SMEM),
            pl.BlockSpec((n, D_MODEL), lambda l, c: (0, 0), pipeline_mode=pl.Buffered(1)),
            _layer_spec((n, PLE_DIM)),
            cache_spec, cache_spec, state_spec,
            _layer_spec((1, D_MODEL)), _layer_spec((1, D_MODEL)), _layer_spec((1, D_MODEL)), _layer_spec((1, D_MODEL)),
            _layer_spec((D_MODEL, IN_WIDTH), single=True),
            _layer_spec((D_MODEL, D_MODEL), single=True),
            _layer_spec((len(POOL_WINDOWS), POOL_GROUP_WIDTH, POOL_GROUP_WIDTH)),
            _layer_spec((1, POOL_WIDTH)),
            _layer_spec((D_MODEL, D_FF), single=True),
            _layer_spec((D_MODEL, D_FF), single=True),
            _layer_spec((D_FF, D_MODEL), single=True),
            _layer_spec((PLE_DIM, D_MODEL), single=True),
            _layer_spec((D_MODEL, D_MODEL), single=True),
        ],
        out_specs=[
            pl.BlockSpec((n, D_MODEL), lambda l, c: (0, 0)),
            cache_spec, cache_spec, state_spec,
        ],
        out_shape=[
            jax.ShapeDtypeStruct((n, D_MODEL), F32),
            jax.ShapeDtypeStruct((depth, nb, WINDOW, KV_WIDTH), F32),
            jax.ShapeDtypeStruct((depth, nb, WINDOW, KV_WIDTH), F32),
            jax.ShapeDtypeStruct((depth, POOL_STATE, nb, POOL_WIDTH), F32),
        ],
        scratch_shapes=[
            pltpu.VMEM((n, IN_WIDTH), F32),
            pltpu.VMEM((n, D_MODEL), F32),
            pltpu.VMEM((len(POOL_WINDOWS), bb * dec, POOL_GROUP_WIDTH), F32),
            pltpu.VMEM((len(POOL_WINDOWS), bb * dec, POOL_GROUP_WIDTH), F32),
        ],
        compiler_params=pltpu.CompilerParams(
            dimension_semantics=("arbitrary", "arbitrary"), vmem_limit_bytes=VMEM_LIMIT),
        name="sample_path",
    )(sinks, xs, ps, ck, cv, st, gmpre, gmpost, gfpre, gfpost, win, wout, wpool, pscale, wg, wu, wd, wple, wpg)


def kernel(x_prompt, x_sample, p_prompt, p_sample, cache_k, cache_v, state_pool, norm_mix_pre, norm_mix_post,
           norm_ffn_pre, norm_ffn_post, w_in, w_out, attn_sinks, w_pool, pool_scale, w_gate, w_up, w_down, w_ple,
           w_ple_gate):
    depth, nbat, seq, _ = p_prompt.shape
    dec_b, dec = x_sample.shape[0], x_sample.shape[1]
    assert 8 % dec == 0 and seq % WINDOW == 0

    win, wout, wpool = w_in.astype(BF16), w_out.astype(BF16), w_pool.astype(BF16)
    wg, wu, wd = w_gate.astype(BF16), w_up.astype(BF16), w_down.astype(BF16)
    wple, wpg = w_ple.astype(BF16), w_ple_gate.astype(BF16)
    g_mix_pre = norm_mix_pre.reshape(depth, 1, D_MODEL)
    g_mix_post = norm_mix_post.reshape(depth, 1, D_MODEL)
    g_ffn_pre = norm_ffn_pre.reshape(depth, 1, D_MODEL)
    g_ffn_post = norm_ffn_post.reshape(depth, 1, D_MODEL)
    pscale = pool_scale.reshape(depth, 1, POOL_WIDTH)

    pp = p_prompt.reshape(depth, nbat * seq, PLE_DIM)
    ps = p_sample.reshape(depth, dec_b * dec, PLE_DIM)
    ck = cache_k.reshape(depth, dec_b, WINDOW, KV_WIDTH)
    cv = cache_v.reshape(depth, dec_b, WINDOW, KV_WIDTH)

    yp = x_prompt
    kp_l, vp_l, sp_l = [], [], []
    for i in range(depth):
        yp, kp, vp, sp = _mix_prompt(yp, attn_sinks, g_mix_pre, g_mix_post, win, wout, wpool, pscale,
                                     layer=i, tm=1024, piece=256)
        yp = _ffn(yp.reshape(nbat * seq, D_MODEL), pp, g_ffn_pre, g_ffn_post, wg, wu, wd, wple, wpg,
                  layer=i, tm=512, nsub=1).reshape(nbat, seq, D_MODEL)
        kp_l.append(kp)
        vp_l.append(vp)
        sp_l.append(sp[:, POOL_PAD - POOL_STATE:])

    ys, ks, vs, ss = _sample_path(x_sample.reshape(dec_b * dec, D_MODEL), ps, attn_sinks, ck, cv,
                                  state_pool.transpose(0, 2, 1, 3),
                                  g_mix_pre, g_mix_post, g_ffn_pre, g_ffn_post, win, wout, wpool, pscale,
                                  wg, wu, wd, wple, wpg, bb=8, dec=dec, nsub=2)

    kv_shape = (depth, -1, WINDOW, N_KV_HEADS, HEAD_DIM)
    return (yp, ys.reshape(dec_b, dec, D_MODEL),
            jnp.stack(kp_l).reshape(kv_shape), jnp.stack(vp_l).reshape(kv_shape), jnp.stack(sp_l),
            ks.reshape(kv_shape), vs.reshape(kv_shape), ss.transpose(0, 2, 1, 3))
```

```python
import functools

import jax
import jax.numpy as jnp
from jax import lax
from jax.experimental import pallas as pl
from jax.experimental.pallas import tpu as pltpu

D_MODEL = 1024
DEPTH = 4
ATTN_WIDTH = 512
HEAD_DIM = 64
N_HEADS = 8
N_KV_HEADS = 2
GROUP = 4
KV_WIDTH = 128
WINDOW = 128
POOL_WIDTH = 512
POOL_WINDOWS = (2, 4, 8, 16)
POOL_GROUP_WIDTH = 128
POOL_STATE = 15
IN_WIDTH = 1280
D_FF = 2816
PLE_DIM = 256
EPS = 1e-6

K_OFF = ATTN_WIDTH
V_OFF = ATTN_WIDTH + KV_WIDTH
U_OFF = ATTN_WIDTH + 2 * KV_WIDTH
LANES = 128
POOL_PAD = 16
POOL_HEAD = 8 + POOL_PAD
LOG2E = 1.4426950408889634
FF_CHUNKS = ((0, 1024), (1024, 2048), (2048, D_FF))

BF16 = jnp.bfloat16
F32 = jnp.float32
VMEM_LIMIT = 56 * 1024 * 1024


def _rms(x, g):
    return x * lax.rsqrt(jnp.mean(x * x, axis=-1, keepdims=True) + EPS) * g


def _sigmoid(x):
    return 1.0 / (1.0 + jnp.exp(-x))


def _dot(a, b):
    return jnp.dot(a, b, preferred_element_type=F32)


def _dot_t(a, b):
    return lax.dot_general(a, b, (((1,), (1,)), ((), ())), preferred_element_type=F32)


def _const_spec(shape, layer=None):
    if layer is None:
        return pl.BlockSpec(shape, lambda *_: (0,) * len(shape), pipeline_mode=pl.Buffered(1))
    return pl.BlockSpec((None,) + shape, lambda *_: (layer,) + (0,) * len(shape),
                        pipeline_mode=pl.Buffered(1))


def _mix_prompt_kernel(sinks_ref, x_ref, gpre_ref, gpost_ref, win_ref, wout_ref, wpool_ref, pscale_ref,
                       y_ref, klast_ref, vlast_ref, plast_ref,
                       z_ref, kext_ref, vt_ref, uext_ref, ps1_ref, ps2_ref, mix_ref, bias_ref, *, tm, piece, layer):
    s = pl.program_id(1)
    nblk = tm // WINDOW

    @pl.when((pl.program_id(0) == 0) & (s == 0))
    def _():
        c = lax.broadcasted_iota(jnp.int32, (2 * WINDOW, WINDOW), 0)
        r = lax.broadcasted_iota(jnp.int32, (2 * WINDOW, WINDOW), 1)
        dist = r + WINDOW - c
        ok = (dist >= 0) & (dist < WINDOW)
        bias_ref[0] = jnp.where(ok, 0.0, -jnp.inf)
        bias_ref[1] = jnp.where(ok & (c >= WINDOW), 0.0, -jnp.inf)

    @pl.when(s == 0)
    def _():
        kext_ref[0:WINDOW, :] = jnp.zeros((WINDOW, KV_WIDTH), F32)
        vt_ref[:, 0:WINDOW] = jnp.zeros((KV_WIDTH, WINDOW), F32)
        uext_ref[0:POOL_HEAD, :] = jnp.zeros((POOL_HEAD, POOL_WIDTH), F32)
        ps1_ref[0:8, :] = jnp.zeros((8, POOL_GROUP_WIDTH), F32)
        ps2_ref[0:8, :] = jnp.zeros((8, POOL_GROUP_WIDTH), F32)

    lane = lax.broadcasted_iota(jnp.int32, (WINDOW, LANES), 1)
    lo = lane < HEAD_DIM
    lo2 = jnp.concatenate([lo, lo], axis=0)
    row8 = lax.broadcasted_iota(jnp.int32, (8, GROUP * WINDOW), 0)
    kcol = lax.broadcasted_iota(jnp.int32, (HEAD_DIM, 2 * WINDOW), 1)
    ones = jnp.ones((HEAD_DIM, 2 * WINDOW), F32)
    half = piece

    def project_in(hf):
        hr = slice(hf * half, (hf + 1) * half)
        h = _rms(x_ref[0, hr, :], gpre_ref[...]).astype(BF16)
        z_ref[hr, :] = _dot(h, win_ref[...])
        uext_ref[POOL_HEAD + hf * half:POOL_HEAD + (hf + 1) * half, :] = z_ref[hr, U_OFF:U_OFF + POOL_WIDTH]
        kext_ref[WINDOW + hf * half:WINDOW + (hf + 1) * half, :] = z_ref[hr, K_OFF:K_OFF + KV_WIDTH]
        for n in range(hf * half // WINDOW, (hf + 1) * half // WINDOW):
            r0 = n * WINDOW
            vt_ref[:, WINDOW + r0:2 * WINDOW + r0] = z_ref[r0:r0 + WINDOW, V_OFF:V_OFF + KV_WIDTH].T

    def attend_stages(blocks):
        st8 = {}

        def scores():
            for n in blocks:
                r0 = n * WINDOW
                rows = slice(r0, r0 + WINDOW)
                kcat = kext_ref[r0:r0 + 2 * WINDOW, :]
                kswap = pltpu.roll(kcat, HEAD_DIM, 1)
                bias1 = bias_ref[jnp.where(s == 0, 1, 0)] if n == 0 else bias_ref[0]
                bias = jnp.concatenate([bias1] * GROUP, axis=1)
                for kv in range(N_KV_HEADS):
                    kk = (jnp.where(lo2, kcat, kswap) if kv == 0 else jnp.where(lo2, kswap, kcat)).astype(BF16)
                    pieces, sinks = [], []
                    for g in range(GROUP):
                        hd = kv * GROUP + g
                        slab = hd // 2
                        qs = z_ref[rows, slab * LANES:(slab + 1) * LANES] * (HEAD_DIM ** -0.5 * LOG2E)
                        keep = lo if hd % 2 == 0 else jnp.logical_not(lo)
                        pieces.append(jnp.where(keep, qs, 0.0).astype(BF16))
                        sinks.append(jnp.full((1, WINDOW), sinks_ref[layer, hd] * LOG2E, F32))
                    q4 = jnp.concatenate(pieces, axis=0)
                    st8[n, kv, "sink"] = jnp.concatenate(sinks, axis=1)
                    st8[n, kv, "st"] = _dot_t(kk, q4) + bias

        def softmax():
            for n in blocks:
                for kv in range(N_KV_HEADS):
                    st, sink = st8.pop((n, kv, "st")), st8.pop((n, kv, "sink"))
                    m = jnp.maximum(jnp.max(st, axis=0, keepdims=True), sink)
                    p = jnp.exp2(st - m)
                    top = jnp.where(row8 == 0, jnp.exp2(sink - m), p[0:8])
                    st8[n, kv, "p"] = jnp.concatenate([top, p[8:]], axis=0).astype(BF16)

        def values():
            for n in blocks:
                r0 = n * WINDOW
                for kv in range(N_KV_HEADS):
                    vth = vt_ref[kv * HEAD_DIM:(kv + 1) * HEAD_DIM, r0:r0 + 2 * WINDOW]
                    a = jnp.concatenate([jnp.where(kcol == 0, 0.0, vth), ones], axis=0).astype(BF16)
                    st8[n, kv, "ot"] = _dot(a, st8.pop((n, kv, "p")))

        def store():
            for n in blocks:
                rows = slice(n * WINDOW, (n + 1) * WINDOW)
                for kv in range(N_KV_HEADS):
                    ot = st8.pop((n, kv, "ot"))
                    inv = 1.0 / ot[HEAD_DIM:HEAD_DIM + 8]
                    on = ot[0:HEAD_DIM] * jnp.concatenate([inv] * (HEAD_DIM // 8), axis=0)
                    for j in range(GROUP // 2):
                        c0 = 2 * j * WINDOW
                        pair = jnp.concatenate([on[:, c0:c0 + WINDOW], on[:, c0 + WINDOW:c0 + 2 * WINDOW]], axis=0)
                        slab = kv * (GROUP // 2) + j
                        mix_ref[rows, slab * LANES:(slab + 1) * LANES] = pair.T.astype(BF16)

        return scores, softmax, values, store

    def pool(hf):
        t0 = hf * half
        pos = lax.broadcasted_iota(jnp.int32, (half, 1), 0) + (s * tm + t0)
        first = POOL_HEAD + t0
        ext = slice(first - POOL_PAD, first + half)
        for gi, w in enumerate(POOL_WINDOWS):
            cols = slice(gi * POOL_GROUP_WIDTH, (gi + 1) * POOL_GROUP_WIDTH)
            src, src_cols, span, bufs = uext_ref, cols, 1, [ps1_ref, ps2_ref]
            while 2 * span < w:
                dst = bufs[0]
                dst[ext, :] = src[ext, src_cols] + src[first - POOL_PAD - span:first + half - span, src_cols]
                src, src_cols, span, bufs = dst, slice(None), 2 * span, bufs[::-1]
            tot = src[first:first + half, src_cols] + src[first - span:first - span + half, src_cols]
            cnt = jnp.minimum(w, pos + 1).astype(F32)
            dlt = (tot / cnt - uext_ref[first:first + half, cols]).astype(BF16)
            yp = _dot(dlt, wpool_ref[gi]) * pscale_ref[:, cols]
            mix_ref[t0:t0 + half, ATTN_WIDTH + gi * POOL_GROUP_WIDTH:ATTN_WIDTH + (gi + 1) * POOL_GROUP_WIDTH] = (
                yp.astype(BF16))

    def project_out(hf):
        hr = slice(hf * half, (hf + 1) * half)
        mixed = _dot(mix_ref[hr, :], wout_ref[...])
        y_ref[0, hr, :] = x_ref[0, hr, :] + _rms(mixed, gpost_ref[...])

    per_half = half // WINDOW
    n_piece = tm // half
    project_in(0)
    for hf in range(n_piece):
        scores, softmax, values, store = attend_stages(range(hf * per_half, (hf + 1) * per_half))
        scores()
        if hf + 1 < n_piece:
            project_in(hf + 1)
        if hf > 0:
            project_out(hf - 1)
        softmax()
        values()
        store()
        pool(hf)
    project_out(n_piece - 1)

    @pl.when(s == pl.num_programs(1) - 1)
    def _():
        klast_ref[0] = z_ref[tm - WINDOW:tm, K_OFF:K_OFF + KV_WIDTH]
        vlast_ref[0] = z_ref[tm - WINDOW:tm, V_OFF:V_OFF + KV_WIDTH]
        plast_ref[0] = uext_ref[POOL_HEAD + tm - POOL_PAD:POOL_HEAD + tm, :]

    kext_ref[0:WINDOW, :] = kext_ref[tm:tm + WINDOW, :]
    vt_ref[:, 0:WINDOW] = vt_ref[:, tm:tm + WINDOW]
    uext_ref[POOL_HEAD - POOL_PAD:POOL_HEAD, :] = uext_ref[POOL_HEAD + tm - POOL_PAD:POOL_HEAD + tm, :]


def _mix_prompt(x, sinks, gpre, gpost, win, wout, wpool, pscale, *, layer, tm, piece):
    b, s, _ = x.shape
    kern = functools.partial(_mix_prompt_kernel, tm=tm, piece=piece, layer=layer)
    return pl.pallas_call(
        kern,
        grid=(b, s // tm),
        in_specs=[
            pl.BlockSpec(memory_space=pltpu.SMEM),
            pl.BlockSpec((1, tm, D_MODEL), lambda i, j: (i, j, 0)),
            _const_spec((1, D_MODEL), layer),
            _const_spec((1, D_MODEL), layer),
            _const_spec((D_MODEL, IN_WIDTH), layer),
            _const_spec((D_MODEL, D_MODEL), layer),
            _const_spec((len(POOL_WINDOWS), POOL_GROUP_WIDTH, POOL_GROUP_WIDTH), layer),
            _const_spec((1, POOL_WIDTH), layer),
        ],
        out_specs=[
            pl.BlockSpec((1, tm, D_MODEL), lambda i, j: (i, j, 0)),
            pl.BlockSpec((1, WINDOW, KV_WIDTH), lambda i, j: (i, 0, 0)),
            pl.BlockSpec((1, WINDOW, KV_WIDTH), lambda i, j: (i, 0, 0)),
            pl.BlockSpec((1, POOL_PAD, POOL_WIDTH), lambda i, j: (i, 0, 0)),
        ],
        out_shape=[
            jax.ShapeDtypeStruct((b, s, D_MODEL), F32),
            jax.ShapeDtypeStruct((b, WINDOW, KV_WIDTH), F32),
            jax.ShapeDtypeStruct((b, WINDOW, KV_WIDTH), F32),
            jax.ShapeDtypeStruct((b, POOL_PAD, POOL_WIDTH), F32),
        ],
        scratch_shapes=[
            pltpu.VMEM((tm, IN_WIDTH), F32),
            pltpu.VMEM((WINDOW + tm, KV_WIDTH), F32),
            pltpu.VMEM((KV_WIDTH, WINDOW + tm), F32),
            pltpu.VMEM((POOL_HEAD + tm, POOL_WIDTH), F32),
            pltpu.VMEM((POOL_HEAD + tm, POOL_GROUP_WIDTH), F32),
            pltpu.VMEM((POOL_HEAD + tm, POOL_GROUP_WIDTH), F32),
            pltpu.VMEM((tm, D_MODEL), BF16),
            pltpu.VMEM((2, 2 * WINDOW, WINDOW), F32),
        ],
        compiler_params=pltpu.CompilerParams(
            dimension_semantics=("arbitrary", "arbitrary"), vmem_limit_bytes=VMEM_LIMIT),
        name=f"mix_prompt_{layer}",
    )(sinks, x, gpre, gpost, win, wout, wpool, pscale)


def _swiglu(f, wg_ref, wu_ref, wd_ref):
    d, pending = None, None
    for c0, c1 in FF_CHUNKS:
        g = _dot(f, wg_ref[:, c0:c1])
        u = _dot(f, wu_ref[:, c0:c1])
        if pending is not None:
            part = _dot(pending[0], wd_ref[pending[1]:pending[2], :])
            d = part if d is None else d + part
        pending = ((g * _sigmoid(g) * u).astype(BF16), c0, c1)
    return d + _dot(pending[0], wd_ref[pending[1]:pending[2], :])


def _ffn_rows(x, p, gpre_ref, gpost_ref, wg_ref, wu_ref, wd_ref, wple_ref, wpg_ref):
    f = _rms(x, gpre_ref[...]).astype(BF16)
    x = x + _rms(_swiglu(f, wg_ref, wu_ref, wd_ref), gpost_ref[...])
    gate = _sigmoid(_dot(x.astype(BF16), wpg_ref[...]))
    return x + gate * _dot(p.astype(BF16), wple_ref[...])


def _ffn_kernel(x_ref, p_ref, gpre_ref, gpost_ref, wg_ref, wu_ref, wd_ref, wple_ref, wpg_ref, o_ref, *, nsub):
    sub = x_ref.shape[0] // nsub
    for i in range(nsub):
        rows = slice(i * sub, (i + 1) * sub)
        o_ref[rows, :] = _ffn_rows(x_ref[rows, :], p_ref[rows, :], gpre_ref, gpost_ref, wg_ref, wu_ref, wd_ref,
                                   wple_ref, wpg_ref)


def _ffn(x, p, gpre, gpost, wg, wu, wd, wple, wpg, *, layer, tm, nsub):
    n = x.shape[0]
    return pl.pallas_call(
        functools.partial(_ffn_kernel, nsub=nsub),
        grid=(n // tm,),
        in_specs=[
            pl.BlockSpec((tm, D_MODEL), lambda i: (i, 0)),
            pl.BlockSpec((None, tm, PLE_DIM), lambda i: (layer, i, 0)),
            _const_spec((1, D_MODEL), layer),
            _const_spec((1, D_MODEL), layer),
            _const_spec((D_MODEL, D_FF), layer),
            _const_spec((D_MODEL, D_FF), layer),
            _const_spec((D_FF, D_MODEL), layer),
            _const_spec((PLE_DIM, D_MODEL), layer),
            _const_spec((D_MODEL, D_MODEL), layer),
        ],
        out_specs=pl.BlockSpec((tm, D_MODEL), lambda i: (i, 0)),
        out_shape=jax.ShapeDtypeStruct((n, D_MODEL), F32),
        compiler_params=pltpu.CompilerParams(
            dimension_semantics=("arbitrary",), vmem_limit_bytes=VMEM_LIMIT),
        name=f"ffn_prompt_{layer}",
    )(x, p, gpre, gpost, wg, wu, wd, wple, wpg)


def _layer_spec(shape, single=False):
    kwargs = dict(pipeline_mode=pl.Buffered(1)) if single else {}
    return pl.BlockSpec((None,) + shape, lambda l, c: (l,) + (0,) * len(shape), **kwargs)


def _sample_kernel(sinks_ref, xs_ref, ps_ref, ck_ref, cv_ref, st_ref, gmpre_ref, gmpost_ref, gfpre_ref, gfpost_ref,
                   win_ref, wout_ref, wpool_ref, pscale_ref, wg_ref, wu_ref, wd_ref, wple_ref, wpg_ref,
                   y_ref, ko_ref, vo_ref, po_ref, z_ref, mix_ref, ustage_ref, dlt_ref, *, bb, dec, nsub):
    l = pl.program_id(0)
    c = pl.program_id(1)
    n_rows = y_ref.shape[0]

    @pl.when((l == 0) & (c == 0))
    def _():
        y_ref[...] = xs_ref[...]

    @pl.when(c == 0)
    def _():
        h = _rms(y_ref[...], gmpre_ref[...]).astype(BF16)
        z_ref[...] = _dot(h, win_ref[...])

    per_tile = 8 // dec
    base = pl.multiple_of(c * (bb * dec), 8)
    lane = lax.broadcasted_iota(jnp.int32, (8, LANES), 1)
    lo = lane < HEAD_DIM
    hi = jnp.logical_not(lo)
    rows64 = N_HEADS * 8
    ext_w = 2 * WINDOW
    new0 = ext_w - 8
    trow = lax.broadcasted_iota(jnp.int32, (rows64, ext_w), 0) & 7
    tstep = trow % dec
    tbat = trow // dec
    klane = lax.broadcasted_iota(jnp.int32, (rows64, ext_w), 1)
    fresh = klane - new0
    valid = ((klane < WINDOW) & (klane >= tstep + 1)) | (
        (klane >= new0) & (fresh // dec == tbat) & (fresh % dec <= tstep))
    tbat_o = tbat[:, 0:LANES]
    wlane = lax.broadcasted_iota(jnp.int32, (KV_WIDTH, WINDOW), 1)
    zpad = jnp.zeros((WINDOW - 8, KV_WIDTH), F32)

    tiles = range(bb // per_tile)
    st8 = {}

    for t in tiles:
        rows = pl.ds(base + 8 * t, 8)
        q = z_ref[rows, 0:ATTN_WIDTH] * (HEAD_DIM ** -0.5)
        knew = jnp.concatenate([zpad, z_ref[rows, K_OFF:K_OFF + KV_WIDTH]], axis=0)
        vnew = jnp.concatenate([zpad, z_ref[rows, V_OFF:V_OFF + KV_WIDTH]], axis=0)
        st8[t, "knew"], st8[t, "vnew"] = knew, vnew
        pieces, sinks = [], []
        for hd in range(N_HEADS):
            slab, half, kv = hd // 2, hd % 2, hd // GROUP
            qs = q[:, slab * LANES:(slab + 1) * LANES]
            src = qs if half == kv else pltpu.roll(qs, HEAD_DIM, 1)
            pieces.append(jnp.where(lo if kv == 0 else hi, src, 0.0))
            sinks.append(jnp.full((8, 1), sinks_ref[l, hd], F32))
        lhs_b = jnp.concatenate(pieces, axis=0).astype(BF16)
        st8[t, "sink"] = jnp.concatenate(sinks, axis=0)
        sc = None
        for i in range(per_tile):
            si = _dot(lhs_b, ck_ref[per_tile * t + i].astype(BF16))
            sc = si if sc is None else jnp.where(tbat_o == i, si, sc)
        sc = jnp.concatenate([sc, _dot_t(lhs_b, knew.astype(BF16))], axis=1)
        st8[t, "sc"] = jnp.where(valid, sc, -jnp.inf)

    for t in tiles:
        sc, sink = st8.pop((t, "sc")), st8.pop((t, "sink"))
        m = jnp.maximum(jnp.max(sc, axis=-1, keepdims=True), sink)
        p = jnp.exp(sc - m)
        st8[t, "denom"] = jnp.sum(p, axis=-1, keepdims=True) + jnp.exp(sink - m)
        st8[t, "pb"] = p.astype(BF16)

    for t in tiles:
        pb = st8.pop((t, "pb"))
        o = None
        for i in range(per_tile):
            oi = _dot_t(pb[:, 0:WINDOW], cv_ref[per_tile * t + i].astype(BF16))
            o = oi if o is None else jnp.where(tbat_o == i, oi, o)
        st8[t, "o"] = o + _dot(pb[:, WINDOW:ext_w], st8[t, "vnew"].astype(BF16))

    for t in tiles:
        rows = pl.ds(base + 8 * t, 8)
        o = st8.pop((t, "o")) / st8.pop((t, "denom"))
        for slab in range(N_HEADS // 2):
            kv = (2 * slab) // GROUP
            even = o[16 * slab:16 * slab + 8]
            odd = o[16 * slab + 8:16 * slab + 16]
            if kv == 0:
                res = jnp.where(lo, even, pltpu.roll(odd, HEAD_DIM, 1))
            else:
                res = jnp.where(lo, pltpu.roll(even, HEAD_DIM, 1), odd)
            mix_ref[rows, slab * LANES:(slab + 1) * LANES] = res

    for t in tiles:
        for new, c_ref, o_ref in ((st8.pop((t, "knew")), ck_ref, ko_ref), (st8.pop((t, "vnew")), cv_ref, vo_ref)):
            cols = new.T
            for bi in range(per_tile):
                b = per_tile * t + bi
                shift = dec * (per_tile - 1 - bi)
                mine = cols if shift == 0 else pltpu.roll(cols, shift, 1)
                kept = pltpu.roll(c_ref[b], WINDOW - dec, 1)
                o_ref[b] = jnp.where(wlane < WINDOW - dec, kept, mine)

    chunk = pl.ds(base, bb * dec)
    for r in range(POOL_STATE - dec):
        po_ref[r] = st_ref[r + dec]
    for gi, w in enumerate(POOL_WINDOWS):
        cols = slice(gi * POOL_GROUP_WIDTH, (gi + 1) * POOL_GROUP_WIDTH)
        ustage_ref[gi] = z_ref[chunk, U_OFF + gi * POOL_GROUP_WIDTH:U_OFF + (gi + 1) * POOL_GROUP_WIDTH]
        u = [ustage_ref[gi, pl.ds(j, bb, stride=dec), :] for j in range(dec)]
        suffix, acc = {}, None
        for m in range(1, w):
            row = st_ref[POOL_STATE - m, :, cols]
            acc = row if acc is None else acc + row
            suffix[m] = acc
        for j in range(dec):
            po_ref[POOL_STATE - dec + j, :, cols] = u[j]
            tot = suffix.get(w - 1 - j)
            for i in range(max(0, j - w + 1), j + 1):
                tot = u[i] if tot is None else tot + u[i]
            dlt_ref[gi, pl.ds(j, bb, stride=dec), :] = tot / float(w) - u[j]

    for gi in range(len(POOL_WINDOWS)):
        cols = slice(gi * POOL_GROUP_WIDTH, (gi + 1) * POOL_GROUP_WIDTH)
        yp = _dot(dlt_ref[gi].astype(BF16), wpool_ref[gi]) * pscale_ref[:, cols]
        mix_ref[chunk, ATTN_WIDTH + gi * POOL_GROUP_WIDTH:ATTN_WIDTH + (gi + 1) * POOL_GROUP_WIDTH] = yp

    @pl.when(c == pl.num_programs(1) - 1)
    def _():
        sub = n_rows // nsub
        for i in range(nsub):
            rs = slice(i * sub, (i + 1) * sub)
            mixed = _dot(mix_ref[rs, :].astype(BF16), wout_ref[...])
            x = y_ref[rs, :] + _rms(mixed, gmpost_ref[...])
            y_ref[rs, :] = _ffn_rows(x, ps_ref[rs, :], gfpre_ref, gfpost_ref, wg_ref, wu_ref, wd_ref, wple_ref,
                                     wpg_ref)


def _sample_path(xs, ps, sinks, ck, cv, st, gmpre, gmpost, gfpre, gfpost, win, wout, wpool, pscale,
                 wg, wu, wd, wple, wpg, *, bb, dec, nsub):
    depth, nb = ck.shape[0], ck.shape[1]
    n = nb * dec
    kern = functools.partial(_sample_kernel, bb=bb, dec=dec, nsub=nsub)
    cache_spec = pl.BlockSpec((None, bb, WINDOW, KV_WIDTH), lambda l, c: (l, c, 0, 0))
    state_spec = pl.BlockSpec((None, POOL_STATE, bb, POOL_WIDTH), lambda l, c: (l, 0, c, 0))
    return pl.pallas_call(
        kern,
        grid=(depth, nb // bb),
        in_specs=[
            pl.BlockSpec(memory_space=pltpu.SMEM),
            pl.BlockSpec((n, D_MODEL), lambda l, c: (0, 0), pipeline_mode=pl.Buffered(1)),
            _layer_spec((n, PLE_DIM)),
            cache_spec, cache_spec, state_spec,
            _layer_spec((1, D_MODEL)), _layer_spec((1, D_MODEL)), _layer_spec((1, D_MODEL)), _layer_spec((1, D_MODEL)),
            _layer_spec((D_MODEL, IN_WIDTH), single=True),
            _layer_spec((D_MODEL, D_MODEL), single=True),
            _layer_spec((len(POOL_WINDOWS), POOL_GROUP_WIDTH, POOL_GROUP_WIDTH)),
            _layer_spec((1, POOL_WIDTH)),
            _layer_spec((D_MODEL, D_FF), single=True),
            _layer_spec((D_MODEL, D_FF), single=True),
            _layer_spec((D_FF, D_MODEL), single=True),
            _layer_spec((PLE_DIM, D_MODEL), single=True),
            _layer_spec((D_MODEL, D_MODEL), single=True),
        ],
        out_specs=[
            pl.BlockSpec((n, D_MODEL), lambda l, c: (0, 0)),
            cache_spec, cache_spec, state_spec,
        ],
        out_shape=[
            jax.ShapeDtypeStruct((n, D_MODEL), F32),
            jax.ShapeDtypeStruct((depth, nb, WINDOW, KV_WIDTH), F32),
            jax.ShapeDtypeStruct((depth, nb, WINDOW, KV_WIDTH), F32),
            jax.ShapeDtypeStruct((depth, POOL_STATE, nb, POOL_WIDTH), F32),
        ],
        scratch_shapes=[
            pltpu.VMEM((n, IN_WIDTH), F32),
            pltpu.VMEM((n, D_MODEL), F32),
            pltpu.VMEM((len(POOL_WINDOWS), bb * dec, POOL_GROUP_WIDTH), F32),
            pltpu.VMEM((len(POOL_WINDOWS), bb * dec, POOL_GROUP_WIDTH), F32),
        ],
        compiler_params=pltpu.CompilerParams(
            dimension_semantics=("arbitrary", "arbitrary"), vmem_limit_bytes=VMEM_LIMIT),
        name="sample_path",
    )(sinks, xs, ps, ck, cv, st, gmpre, gmpost, gfpre, gfpost, win, wout, wpool, pscale, wg, wu, wd, wple, wpg)


def _cache_from_device_layout(c):
    return c.reshape(c.shape[0], c.shape[1], N_KV_HEADS, HEAD_DIM, WINDOW).transpose(0, 1, 4, 2, 3)


def kernel(x_prompt, x_sample, p_prompt, p_sample, cache_k, cache_v, state_pool, norm_mix_pre, norm_mix_post,
           norm_ffn_pre, norm_ffn_post, w_in, w_out, attn_sinks, w_pool, pool_scale, w_gate, w_up, w_down, w_ple,
           w_ple_gate):
    depth, nbat, seq, _ = p_prompt.shape
    dec_b, dec = x_sample.shape[0], x_sample.shape[1]
    assert 8 % dec == 0 and seq % WINDOW == 0

    win, wout, wpool = w_in.astype(BF16), w_out.astype(BF16), w_pool.astype(BF16)
    wg, wu, wd = w_gate.astype(BF16), w_up.astype(BF16), w_down.astype(BF16)
    wple, wpg = w_ple.astype(BF16), w_ple_gate.astype(BF16)
    g_mix_pre = norm_mix_pre.reshape(depth, 1, D_MODEL)
    g_mix_post = norm_mix_post.reshape(depth, 1, D_MODEL)
    g_ffn_pre = norm_ffn_pre.reshape(depth, 1, D_MODEL)
    g_ffn_post = norm_ffn_post.reshape(depth, 1, D_MODEL)
    pscale = pool_scale.reshape(depth, 1, POOL_WIDTH)

    pp = p_prompt.reshape(depth, nbat * seq, PLE_DIM)
    ps = p_sample.reshape(depth, dec_b * dec, PLE_DIM)
    ck = cache_k.transpose(0, 1, 3, 4, 2).reshape(depth, dec_b, KV_WIDTH, WINDOW)
    cv = cache_v.transpose(0, 1, 3, 4, 2).reshape(depth, dec_b, KV_WIDTH, WINDOW)

    yp = x_prompt
    kp_l, vp_l, sp_l = [], [], []
    for i in range(depth):
        yp, kp, vp, sp = _mix_prompt(yp, attn_sinks, g_mix_pre, g_mix_post, win, wout, wpool, pscale,
                                     layer=i, tm=1024, piece=256)
        yp = _ffn(yp.reshape(nbat * seq, D_MODEL), pp, g_ffn_pre, g_ffn_post, wg, wu, wd, wple, wpg,
                  layer=i, tm=512, nsub=1).reshape(nbat, seq, D_MODEL)
        kp_l.append(kp)
        vp_l.append(vp)
        sp_l.append(sp[:, POOL_PAD - POOL_STATE:])

    ys, ks, vs, ss = _sample_path(x_sample.reshape(dec_b * dec, D_MODEL), ps, attn_sinks, ck, cv,
                                  state_pool.transpose(0, 2, 1, 3),
                                  g_mix_pre, g_mix_post, g_ffn_pre, g_ffn_post, win, wout, wpool, pscale,
                                  wg, wu, wd, wple, wpg, bb=8, dec=dec, nsub=2)

    kv_shape = (depth, -1, WINDOW, N_KV_HEADS, HEAD_DIM)
    return (yp, ys.reshape(dec_b, dec, D_MODEL),
            jnp.stack(kp_l).reshape(kv_shape), jnp.stack(vp_l).reshape(kv_shape), jnp.stack(sp_l),
            _cache_from_device_layout(ks), _cache_from_device_layout(vs), ss.transpose(0, 2, 1, 3))
```

```python
import functools

import jax
import jax.numpy as jnp
from jax import lax
from jax.experimental import pallas as pl
from jax.experimental.pallas import tpu as pltpu

D_MODEL = 1024
DEPTH = 4
ATTN_WIDTH = 512
HEAD_DIM = 64
N_HEADS = 8
N_KV_HEADS = 2
GROUP = 4
KV_WIDTH = 128
WINDOW = 128
POOL_WIDTH = 512
POOL_WINDOWS = (2, 4, 8, 16)
POOL_GROUP_WIDTH = 128
POOL_STATE = 15
IN_WIDTH = 1280
D_FF = 2816
PLE_DIM = 256
EPS = 1e-6

K_OFF = ATTN_WIDTH
V_OFF = ATTN_WIDTH + KV_WIDTH
U_OFF = ATTN_WIDTH + 2 * KV_WIDTH
LANES = 128
POOL_PAD = 16
POOL_HEAD = 8 + POOL_PAD
LOG2E = 1.4426950408889634
FF_CHUNKS = ((0, 1024), (1024, 2048), (2048, D_FF))

BF16 = jnp.bfloat16
F32 = jnp.float32
VMEM_LIMIT = 56 * 1024 * 1024


def _rms(x, g):
    return x * lax.rsqrt(jnp.mean(x * x, axis=-1, keepdims=True) + EPS) * g


def _sigmoid(x):
    return 1.0 / (1.0 + jnp.exp(-x))


def _dot(a, b):
    return jnp.dot(a, b, preferred_element_type=F32)


def _dot_t(a, b):
    return lax.dot_general(a, b, (((1,), (1,)), ((), ())), preferred_element_type=F32)


def _const_spec(shape, layer=None):
    if layer is None:
        return pl.BlockSpec(shape, lambda *_: (0,) * len(shape), pipeline_mode=pl.Buffered(1))
    return pl.BlockSpec((None,) + shape, lambda *_: (layer,) + (0,) * len(shape),
                        pipeline_mode=pl.Buffered(1))


def _mix_prompt_kernel(sinks_ref, x_ref, gpre_ref, gpost_ref, win_ref, wout_ref, wpool_ref, pscale_ref,
                       wg32_ref, wu32_ref, wd32_ref, wpg32_ref,
                       y_ref, klast_ref, vlast_ref, plast_ref, wg16_ref, wu16_ref, wd16_ref, wpg16_ref,
                       z_ref, kext_ref, vt_ref, uext_ref, ps1_ref, ps2_ref, mix_ref, bias_ref, *, tm, piece, layer):
    s = pl.program_id(1)
    nblk = tm // WINDOW

    for w32_ref, w16_ref in ((wg32_ref, wg16_ref), (wu32_ref, wu16_ref), (wd32_ref, wd16_ref),
                             (wpg32_ref, wpg16_ref)):
        w16_ref[...] = w32_ref[...].astype(BF16)

    @pl.when((pl.program_id(0) == 0) & (s == 0))
    def _():
        c = lax.broadcasted_iota(jnp.int32, (2 * WINDOW, WINDOW), 0)
        r = lax.broadcasted_iota(jnp.int32, (2 * WINDOW, WINDOW), 1)
        dist = r + WINDOW - c
        ok = (dist >= 0) & (dist < WINDOW)
        bias_ref[0] = jnp.where(ok, 0.0, -jnp.inf)
        bias_ref[1] = jnp.where(ok & (c >= WINDOW), 0.0, -jnp.inf)

    @pl.when(s == 0)
    def _():
        kext_ref[0:WINDOW, :] = jnp.zeros((WINDOW, KV_WIDTH), F32)
        vt_ref[:, 0:WINDOW] = jnp.zeros((KV_WIDTH, WINDOW), F32)
        uext_ref[0:POOL_HEAD, :] = jnp.zeros((POOL_HEAD, POOL_WIDTH), F32)
        ps1_ref[0:8, :] = jnp.zeros((8, POOL_GROUP_WIDTH), F32)
        ps2_ref[0:8, :] = jnp.zeros((8, POOL_GROUP_WIDTH), F32)

    lane = lax.broadcasted_iota(jnp.int32, (WINDOW, LANES), 1)
    lo = lane < HEAD_DIM
    lo2 = jnp.concatenate([lo, lo], axis=0)
    row8 = lax.broadcasted_iota(jnp.int32, (8, GROUP * WINDOW), 0)
    kcol = lax.broadcasted_iota(jnp.int32, (HEAD_DIM, 2 * WINDOW), 1)
    ones = jnp.ones((HEAD_DIM, 2 * WINDOW), F32)
    half = piece

    def project_in(hf):
        hr = slice(hf * half, (hf + 1) * half)
        h = _rms(x_ref[0, hr, :], gpre_ref[...]).astype(BF16)
        z_ref[hr, :] = _dot(h, win_ref[...])
        uext_ref[POOL_HEAD + hf * half:POOL_HEAD + (hf + 1) * half, :] = z_ref[hr, U_OFF:U_OFF + POOL_WIDTH]
        kext_ref[WINDOW + hf * half:WINDOW + (hf + 1) * half, :] = z_ref[hr, K_OFF:K_OFF + KV_WIDTH]
        for n in range(hf * half // WINDOW, (hf + 1) * half // WINDOW):
            r0 = n * WINDOW
            vt_ref[:, WINDOW + r0:2 * WINDOW + r0] = z_ref[r0:r0 + WINDOW, V_OFF:V_OFF + KV_WIDTH].T

    def attend_stages(blocks):
        st8 = {}

        def scores():
            for n in blocks:
                r0 = n * WINDOW
                rows = slice(r0, r0 + WINDOW)
                kcat = kext_ref[r0:r0 + 2 * WINDOW, :]
                kswap = pltpu.roll(kcat, HEAD_DIM, 1)
                bias1 = bias_ref[jnp.where(s == 0, 1, 0)] if n == 0 else bias_ref[0]
                bias = jnp.concatenate([bias1] * GROUP, axis=1)
                for kv in range(N_KV_HEADS):
                    kk = (jnp.where(lo2, kcat, kswap) if kv == 0 else jnp.where(lo2, kswap, kcat)).astype(BF16)
                    pieces, sinks = [], []
                    for g in range(GROUP):
                        hd = kv * GROUP + g
                        slab = hd // 2
                        qs = z_ref[rows, slab * LANES:(slab + 1) * LANES] * (HEAD_DIM ** -0.5 * LOG2E)
                        keep = lo if hd % 2 == 0 else jnp.logical_not(lo)
                        pieces.append(jnp.where(keep, qs, 0.0).astype(BF16))
                        sinks.append(jnp.full((1, WINDOW), sinks_ref[layer, hd] * LOG2E, F32))
                    q4 = jnp.concatenate(pieces, axis=0)
                    st8[n, kv, "sink"] = jnp.concatenate(sinks, axis=1)
                    st8[n, kv, "st"] = _dot_t(kk, q4) + bias

        def softmax():
            for n in blocks:
                for kv in range(N_KV_HEADS):
                    st, sink = st8.pop((n, kv, "st")), st8.pop((n, kv, "sink"))
                    m = jnp.maximum(jnp.max(st, axis=0, keepdims=True), sink)
                    p = jnp.exp2(st - m)
                    top = jnp.where(row8 == 0, jnp.exp2(sink - m), p[0:8])
                    st8[n, kv, "p"] = jnp.concatenate([top, p[8:]], axis=0).astype(BF16)

        def values():
            for n in blocks:
                r0 = n * WINDOW
                for kv in range(N_KV_HEADS):
                    vth = vt_ref[kv * HEAD_DIM:(kv + 1) * HEAD_DIM, r0:r0 + 2 * WINDOW]
                    a = jnp.concatenate([jnp.where(kcol == 0, 0.0, vth), ones], axis=0).astype(BF16)
                    st8[n, kv, "ot"] = _dot(a, st8.pop((n, kv, "p")))

        def store():
            for n in blocks:
                rows = slice(n * WINDOW, (n + 1) * WINDOW)
                for kv in range(N_KV_HEADS):
                    ot = st8.pop((n, kv, "ot"))
                    inv = 1.0 / ot[HEAD_DIM:HEAD_DIM + 8]
                    on = ot[0:HEAD_DIM] * jnp.concatenate([inv] * (HEAD_DIM // 8), axis=0)
                    for j in range(GROUP // 2):
                        c0 = 2 * j * WINDOW
                        pair = jnp.concatenate([on[:, c0:c0 + WINDOW], on[:, c0 + WINDOW:c0 + 2 * WINDOW]], axis=0)
                        slab = kv * (GROUP // 2) + j
                        mix_ref[rows, slab * LANES:(slab + 1) * LANES] = pair.T.astype(BF16)

        return scores, softmax, values, store

    def pool(hf):
        t0 = hf * half
        pos = lax.broadcasted_iota(jnp.int32, (half, 1), 0) + (s * tm + t0)
        first = POOL_HEAD + t0
        ext = slice(first - POOL_PAD, first + half)
        for gi, w in enumerate(POOL_WINDOWS):
            cols = slice(gi * POOL_GROUP_WIDTH, (gi + 1) * POOL_GROUP_WIDTH)
            src, src_cols, span, bufs = uext_ref, cols, 1, [ps1_ref, ps2_ref]
            while 2 * span < w:
                dst = bufs[0]
                dst[ext, :] = src[ext, src_cols] + src[first - POOL_PAD - span:first + half - span, src_cols]
                src, src_cols, span, bufs = dst, slice(None), 2 * span, bufs[::-1]
            tot = src[first:first + half, src_cols] + src[first - span:first - span + half, src_cols]
            cnt = jnp.minimum(w, pos + 1).astype(F32)
            dlt = (tot / cnt - uext_ref[first:first + half, cols]).astype(BF16)
            yp = _dot(dlt, wpool_ref[gi]) * pscale_ref[:, cols]
            mix_ref[t0:t0 + half, ATTN_WIDTH + gi * POOL_GROUP_WIDTH:ATTN_WIDTH + (gi + 1) * POOL_GROUP_WIDTH] = (
                yp.astype(BF16))

    def project_out(hf):
        hr = slice(hf * half, (hf + 1) * half)
        mixed = _dot(mix_ref[hr, :], wout_ref[...])
        y_ref[0, hr, :] = x_ref[0, hr, :] + _rms(mixed, gpost_ref[...])

    per_half = half // WINDOW
    n_piece = tm // half
    project_in(0)
    for hf in range(n_piece):
        scores, softmax, values, store = attend_stages(range(hf * per_half, (hf + 1) * per_half))
        scores()
        if hf + 1 < n_piece:
            project_in(hf + 1)
        if hf > 0:
            project_out(hf - 1)
        softmax()
        values()
        store()
        pool(hf)
    project_out(n_piece - 1)

    @pl.when(s == pl.num_programs(1) - 1)
    def _():
        klast_ref[0] = z_ref[tm - WINDOW:tm, K_OFF:K_OFF + KV_WIDTH]
        vlast_ref[0] = z_ref[tm - WINDOW:tm, V_OFF:V_OFF + KV_WIDTH]
        plast_ref[0] = uext_ref[POOL_HEAD + tm - POOL_PAD:POOL_HEAD + tm, :]

    kext_ref[0:WINDOW, :] = kext_ref[tm:tm + WINDOW, :]
    vt_ref[:, 0:WINDOW] = vt_ref[:, tm:tm + WINDOW]
    uext_ref[POOL_HEAD - POOL_PAD:POOL_HEAD, :] = uext_ref[POOL_HEAD + tm - POOL_PAD:POOL_HEAD + tm, :]


def _row_chunk_specs(shape, n_steps, layer, step_of):
    rows, cols = shape
    chunk = rows // n_steps
    assert chunk * n_steps == rows and chunk % 16 == 0, (shape, n_steps)
    return (pl.BlockSpec((None, chunk, cols), lambda *g: (layer, step_of(*g), 0)),
            pl.BlockSpec((chunk, cols), lambda *g: (step_of(*g), 0)))


def _mix_prompt(x, sinks, gpre, gpost, win, wout, wpool, pscale, wg32, wu32, wd32, wpg32, *, layer, tm, piece):
    b, s, _ = x.shape
    kern = functools.partial(_mix_prompt_kernel, tm=tm, piece=piece, layer=layer)
    n_tiles = s // tm
    cast_shapes = ((D_MODEL, D_FF), (D_MODEL, D_FF), (D_FF, D_MODEL), (D_MODEL, D_MODEL))
    cast_specs = [_row_chunk_specs(shape, b * n_tiles, layer, lambda i, j: i * n_tiles + j) for shape in cast_shapes]
    return pl.pallas_call(
        kern,
        grid=(b, n_tiles),
        in_specs=[
            pl.BlockSpec(memory_space=pltpu.SMEM),
            pl.BlockSpec((1, tm, D_MODEL), lambda i, j: (i, j, 0)),
            _const_spec((1, D_MODEL), layer),
            _const_spec((1, D_MODEL), layer),
            _const_spec((D_MODEL, IN_WIDTH), layer),
            _const_spec((D_MODEL, D_MODEL), layer),
            _const_spec((len(POOL_WINDOWS), POOL_GROUP_WIDTH, POOL_GROUP_WIDTH), layer),
            _const_spec((1, POOL_WIDTH), layer),
        ] + [spec_in for spec_in, _ in cast_specs],
        out_specs=[
            pl.BlockSpec((1, tm, D_MODEL), lambda i, j: (i, j, 0)),
            pl.BlockSpec((1, WINDOW, KV_WIDTH), lambda i, j: (i, 0, 0)),
            pl.BlockSpec((1, WINDOW, KV_WIDTH), lambda i, j: (i, 0, 0)),
            pl.BlockSpec((1, POOL_PAD, POOL_WIDTH), lambda i, j: (i, 0, 0)),
        ] + [spec_out for _, spec_out in cast_specs],
        out_shape=[
            jax.ShapeDtypeStruct((b, s, D_MODEL), F32),
            jax.ShapeDtypeStruct((b, WINDOW, KV_WIDTH), F32),
            jax.ShapeDtypeStruct((b, WINDOW, KV_WIDTH), F32),
            jax.ShapeDtypeStruct((b, POOL_PAD, POOL_WIDTH), F32),
        ] + [jax.ShapeDtypeStruct(shape, BF16) for shape in cast_shapes],
        scratch_shapes=[
            pltpu.VMEM((tm, IN_WIDTH), F32),
            pltpu.VMEM((WINDOW + tm, KV_WIDTH), F32),
            pltpu.VMEM((KV_WIDTH, WINDOW + tm), F32),
            pltpu.VMEM((POOL_HEAD + tm, POOL_WIDTH), F32),
            pltpu.VMEM((POOL_HEAD + tm, POOL_GROUP_WIDTH), F32),
            pltpu.VMEM((POOL_HEAD + tm, POOL_GROUP_WIDTH), F32),
            pltpu.VMEM((tm, D_MODEL), BF16),
            pltpu.VMEM((2, 2 * WINDOW, WINDOW), F32),
        ],
        compiler_params=pltpu.CompilerParams(
            dimension_semantics=("arbitrary", "arbitrary"), vmem_limit_bytes=VMEM_LIMIT),
        name=f"mix_prompt_{layer}",
    )(sinks, x, gpre, gpost, win, wout, wpool, pscale, wg32, wu32, wd32, wpg32)


def _swiglu(f, wg_ref, wu_ref, wd_ref):
    d, pending = None, None
    for c0, c1 in FF_CHUNKS:
        g = _dot(f, wg_ref[:, c0:c1])
        u = _dot(f, wu_ref[:, c0:c1])
        if pending is not None:
            part = _dot(pending[0], wd_ref[pending[1]:pending[2], :])
            d = part if d is None else d + part
        pending = ((g * _sigmoid(g) * u).astype(BF16), c0, c1)
    return d + _dot(pending[0], wd_ref[pending[1]:pending[2], :])


def _ffn_rows(x, p, gpre_ref, gpost_ref, wg_ref, wu_ref, wd_ref, wple_ref, wpg_ref):
    f = _rms(x, gpre_ref[...]).astype(BF16)
    x = x + _rms(_swiglu(f, wg_ref, wu_ref, wd_ref), gpost_ref[...])
    gate = _sigmoid(_dot(x.astype(BF16), wpg_ref[...]))
    return x + gate * _dot(p.astype(BF16), wple_ref[...])


def _ffn_kernel(x_ref, p_ref, gpre_ref, gpost_ref, wg_ref, wu_ref, wd_ref, wple_ref, wpg_ref, o_ref, *, nsub):
    sub = x_ref.shape[0] // nsub
    for i in range(nsub):
        rows = slice(i * sub, (i + 1) * sub)
        o_ref[rows, :] = _ffn_rows(x_ref[rows, :], p_ref[rows, :], gpre_ref, gpost_ref, wg_ref, wu_ref, wd_ref,
                                   wple_ref, wpg_ref)


def _ffn(x, p, gpre, gpost, wg, wu, wd, wple, wpg, *, layer, tm, nsub):
    n = x.shape[0]
    return pl.pallas_call(
        functools.partial(_ffn_kernel, nsub=nsub),
        grid=(n // tm,),
        in_specs=[
            pl.BlockSpec((tm, D_MODEL), lambda i: (i, 0)),
            pl.BlockSpec((None, tm, PLE_DIM), lambda i: (layer, i, 0)),
            _const_spec((1, D_MODEL), layer),
            _const_spec((1, D_MODEL), layer),
            _const_spec((D_MODEL, D_FF)),
            _const_spec((D_MODEL, D_FF)),
            _const_spec((D_FF, D_MODEL)),
            _const_spec((PLE_DIM, D_MODEL), layer),
            _const_spec((D_MODEL, D_MODEL)),
        ],
        out_specs=pl.BlockSpec((tm, D_MODEL), lambda i: (i, 0)),
        out_shape=jax.ShapeDtypeStruct((n, D_MODEL), F32),
        compiler_params=pltpu.CompilerParams(
            dimension_semantics=("arbitrary",), vmem_limit_bytes=VMEM_LIMIT),
        name=f"ffn_prompt_{layer}",
    )(x, p, gpre, gpost, wg, wu, wd, wple, wpg)


def _layer_spec(shape, single=False):
    kwargs = dict(pipeline_mode=pl.Buffered(1)) if single else {}
    return pl.BlockSpec((None,) + shape, lambda l, c: (l,) + (0,) * len(shape), **kwargs)


def _sample_kernel(sinks_ref, xs_ref, ps_ref, ck_ref, cv_ref, st_ref, gmpre_ref, gmpost_ref, gfpre_ref, gfpost_ref,
                   win_ref, wout_ref, wpool_ref, pscale_ref, wg32_ref, wu32_ref, wd32_ref, wple_ref, wpg32_ref,
                   y_ref, ko_ref, vo_ref, po_ref, z_ref, mix_ref, ustage_ref, dlt_ref,
                   wg_ref, wu_ref, wd_ref, wpg_ref, *, bb, dec, nsub):
    l = pl.program_id(0)
    c = pl.program_id(1)
    n_rows = y_ref.shape[0]

    @pl.when((l == 0) & (c == 0))
    def _():
        y_ref[...] = xs_ref[...]

    @pl.when(c == 0)
    def _():
        h = _rms(y_ref[...], gmpre_ref[...]).astype(BF16)
        z_ref[...] = _dot(h, win_ref[...])

    for w32_ref, w16_ref in ((wg32_ref, wg_ref), (wu32_ref, wu_ref), (wd32_ref, wd_ref), (wpg32_ref, wpg_ref)):
        rows_w = w32_ref.shape[0]
        w16_ref[pl.ds(pl.multiple_of(c * rows_w, 16), rows_w), :] = w32_ref[...].astype(BF16)

    per_tile = 8 // dec
    base = pl.multiple_of(c * (bb * dec), 8)
    lane = lax.broadcasted_iota(jnp.int32, (8, LANES), 1)
    lo = lane < HEAD_DIM
    hi = jnp.logical_not(lo)
    rows64 = N_HEADS * 8
    ext_w = 2 * WINDOW
    new0 = ext_w - 8
    trow = lax.broadcasted_iota(jnp.int32, (rows64, ext_w), 0) & 7
    tstep = trow % dec
    tbat = trow // dec
    klane = lax.broadcasted_iota(jnp.int32, (rows64, ext_w), 1)
    fresh = klane - new0
    valid = ((klane < WINDOW) & (klane >= tstep + 1)) | (
        (klane >= new0) & (fresh // dec == tbat) & (fresh % dec <= tstep))
    tbat_o = tbat[:, 0:LANES]
    wlane = lax.broadcasted_iota(jnp.int32, (KV_WIDTH, WINDOW), 1)
    zpad = jnp.zeros((WINDOW - 8, KV_WIDTH), F32)

    tiles = range(bb // per_tile)
    st8 = {}

    for t in tiles:
        rows = pl.ds(base + 8 * t, 8)
        q = z_ref[rows, 0:ATTN_WIDTH] * (HEAD_DIM ** -0.5)
        knew = jnp.concatenate([zpad, z_ref[rows, K_OFF:K_OFF + KV_WIDTH]], axis=0)
        vnew = jnp.concatenate([zpad, z_ref[rows, V_OFF:V_OFF + KV_WIDTH]], axis=0)
        st8[t, "knew"], st8[t, "vnew"] = knew, vnew
        pieces, sinks = [], []
        for hd in range(N_HEADS):
            slab, half, kv = hd // 2, hd % 2, hd // GROUP
            qs = q[:, slab * LANES:(slab + 1) * LANES]
            src = qs if half == kv else pltpu.roll(qs, HEAD_DIM, 1)
            pieces.append(jnp.where(lo if kv == 0 else hi, src, 0.0))
            sinks.append(jnp.full((8, 1), sinks_ref[l, hd], F32))
        lhs_b = jnp.concatenate(pieces, axis=0).astype(BF16)
        st8[t, "sink"] = jnp.concatenate(sinks, axis=0)
        sc = None
        for i in range(per_tile):
            si = _dot(lhs_b, ck_ref[per_tile * t + i].astype(BF16))
            sc = si if sc is None else jnp.where(tbat_o == i, si, sc)
        sc = jnp.concatenate([sc, _dot_t(lhs_b, knew.astype(BF16))], axis=1)
        st8[t, "sc"] = jnp.where(valid, sc, -jnp.inf)

    for t in tiles:
        sc, sink = st8.pop((t, "sc")), st8.pop((t, "sink"))
        m = jnp.maximum(jnp.max(sc, axis=-1, keepdims=True), sink)
        p = jnp.exp(sc - m)
        st8[t, "denom"] = jnp.sum(p, axis=-1, keepdims=True) + jnp.exp(sink - m)
        st8[t, "pb"] = p.astype(BF16)

    for t in tiles:
        pb = st8.pop((t, "pb"))
        o = None
        for i in range(per_tile):
            oi = _dot_t(pb[:, 0:WINDOW], cv_ref[per_tile * t + i].astype(BF16))
            o = oi if o is None else jnp.where(tbat_o == i, oi, o)
        st8[t, "o"] = o + _dot(pb[:, WINDOW:ext_w], st8[t, "vnew"].astype(BF16))

    for t in tiles:
        rows = pl.ds(base + 8 * t, 8)
        o = st8.pop((t, "o")) / st8.pop((t, "denom"))
        for slab in range(N_HEADS // 2):
            kv = (2 * slab) // GROUP
            even = o[16 * slab:16 * slab + 8]
            odd = o[16 * slab + 8:16 * slab + 16]
            if kv == 0:
                res = jnp.where(lo, even, pltpu.roll(odd, HEAD_DIM, 1))
            else:
                res = jnp.where(lo, pltpu.roll(even, HEAD_DIM, 1), odd)
            mix_ref[rows, slab * LANES:(slab + 1) * LANES] = res

    for t in tiles:
        for new, c_ref, o_ref in ((st8.pop((t, "knew")), ck_ref, ko_ref), (st8.pop((t, "vnew")), cv_ref, vo_ref)):
            cols = new.T
            for bi in range(per_tile):
                b = per_tile * t + bi
                shift = dec * (per_tile - 1 - bi)
                mine = cols if shift == 0 else pltpu.roll(cols, shift, 1)
                kept = pltpu.roll(c_ref[b], WINDOW - dec, 1)
                o_ref[b] = jnp.where(wlane < WINDOW - dec, kept, mine)

    chunk = pl.ds(base, bb * dec)
    for r in range(POOL_STATE - dec):
        po_ref[r] = st_ref[r + dec]
    for gi, w in enumerate(POOL_WINDOWS):
        cols = slice(gi * POOL_GROUP_WIDTH, (gi + 1) * POOL_GROUP_WIDTH)
        ustage_ref[gi] = z_ref[chunk, U_OFF + gi * POOL_GROUP_WIDTH:U_OFF + (gi + 1) * POOL_GROUP_WIDTH]
        u = [ustage_ref[gi, pl.ds(j, bb, stride=dec), :] for j in range(dec)]
        suffix, acc = {}, None
        for m in range(1, w):
            row = st_ref[POOL_STATE - m, :, cols]
            acc = row if acc is None else acc + row
            suffix[m] = acc
        for j in range(dec):
            po_ref[POOL_STATE - dec + j, :, cols] = u[j]
            tot = suffix.get(w - 1 - j)
            for i in range(max(0, j - w + 1), j + 1):
                tot = u[i] if tot is None else tot + u[i]
            dlt_ref[gi, pl.ds(j, bb, stride=dec), :] = tot / float(w) - u[j]

    for gi in range(len(POOL_WINDOWS)):
        cols = slice(gi * POOL_GROUP_WIDTH, (gi + 1) * POOL_GROUP_WIDTH)
        yp = _dot(dlt_ref[gi].astype(BF16), wpool_ref[gi]) * pscale_ref[:, cols]
        mix_ref[chunk, ATTN_WIDTH + gi * POOL_GROUP_WIDTH:ATTN_WIDTH + (gi + 1) * POOL_GROUP_WIDTH] = yp

    @pl.when(c == pl.num_programs(1) - 1)
    def _():
        sub = n_rows // nsub
        for i in range(nsub):
            rs = slice(i * sub, (i + 1) * sub)
            mixed = _dot(mix_ref[rs, :].astype(BF16), wout_ref[...])
            x = y_ref[rs, :] + _rms(mixed, gmpost_ref[...])
            y_ref[rs, :] = _ffn_rows(x, ps_ref[rs, :], gfpre_ref, gfpost_ref, wg_ref, wu_ref, wd_ref, wple_ref,
                                     wpg_ref)


def _sample_path(xs, ps, sinks, ck, cv, st, gmpre, gmpost, gfpre, gfpost, win, wout, wpool, pscale,
                 wg, wu, wd, wple, wpg, *, bb, dec, nsub):
    depth, nb = ck.shape[0], ck.shape[1]
    n = nb * dec
    kern = functools.partial(_sample_kernel, bb=bb, dec=dec, nsub=nsub)
    n_chunks = nb // bb

    def _chunk_spec(rows, cols):
        assert rows % n_chunks == 0 and (rows // n_chunks) % 16 == 0
        return pl.BlockSpec((None, rows // n_chunks, cols), lambda l, c: (l, c, 0))

    cache_spec = pl.BlockSpec((None, bb, KV_WIDTH, WINDOW), lambda l, c: (l, c, 0, 0))
    state_spec = pl.BlockSpec((None, POOL_STATE, bb, POOL_WIDTH), lambda l, c: (l, 0, c, 0))
    return pl.pallas_call(
        kern,
        grid=(depth, nb // bb),
        in_specs=[
            pl.BlockSpec(memory_space=pltpu.SMEM),
            pl.BlockSpec((n, D_MODEL), lambda l, c: (0, 0), pipeline_mode=pl.Buffered(1)),
            _layer_spec((n, PLE_DIM)),
            cache_spec, cache_spec, state_spec,
            _layer_spec((1, D_MODEL)), _layer_spec((1, D_MODEL)), _layer_spec((1, D_MODEL)), _layer_spec((1, D_MODEL)),
            _layer_spec((D_MODEL, IN_WIDTH), single=True),
            _layer_spec((D_MODEL, D_MODEL), single=True),
            _layer_spec((len(POOL_WINDOWS), POOL_GROUP_WIDTH, POOL_GROUP_WIDTH)),
            _layer_spec((1, POOL_WIDTH)),
            _chunk_spec(D_MODEL, D_FF),
            _chunk_spec(D_MODEL, D_FF),
            _chunk_spec(D_FF, D_MODEL),
            _layer_spec((PLE_DIM, D_MODEL), single=True),
            _chunk_spec(D_MODEL, D_MODEL),
        ],
        out_specs=[
            pl.BlockSpec((n, D_MODEL), lambda l, c: (0, 0)),
            cache_spec, cache_spec, state_spec,
        ],
        out_shape=[
            jax.ShapeDtypeStruct((n, D_MODEL), F32),
            jax.ShapeDtypeStruct((depth, nb, KV_WIDTH, WINDOW), F32),
            jax.ShapeDtypeStruct((depth, nb, KV_WIDTH, WINDOW), F32),
            jax.ShapeDtypeStruct((depth, POOL_STATE, nb, POOL_WIDTH), F32),
        ],
        scratch_shapes=[
            pltpu.VMEM((n, IN_WIDTH), F32),
            pltpu.VMEM((n, D_MODEL), F32),
            pltpu.VMEM((len(POOL_WINDOWS), bb * dec, POOL_GROUP_WIDTH), F32),
            pltpu.VMEM((len(POOL_WINDOWS), bb * dec, POOL_GROUP_WIDTH), F32),
            pltpu.VMEM((D_MODEL, D_FF), BF16),
            pltpu.VMEM((D_MODEL, D_FF), BF16),
            pltpu.VMEM((D_FF, D_MODEL), BF16),
            pltpu.VMEM((D_MODEL, D_MODEL), BF16),
        ],
        compiler_params=pltpu.CompilerParams(
            dimension_semantics=("arbitrary", "arbitrary"), vmem_limit_bytes=VMEM_LIMIT),
        name="sample_path",
    )(sinks, xs, ps, ck, cv, st, gmpre, gmpost, gfpre, gfpost, win, wout, wpool, pscale, wg, wu, wd, wple, wpg)


def _cache_from_device_layout(c):
    return c.reshape(c.shape[0], c.shape[1], N_KV_HEADS, HEAD_DIM, WINDOW).transpose(0, 1, 4, 2, 3)


def kernel(x_prompt, x_sample, p_prompt, p_sample, cache_k, cache_v, state_pool, norm_mix_pre, norm_mix_post,
           norm_ffn_pre, norm_ffn_post, w_in, w_out, attn_sinks, w_pool, pool_scale, w_gate, w_up, w_down, w_ple,
           w_ple_gate):
    depth, nbat, seq, _ = p_prompt.shape
    dec_b, dec = x_sample.shape[0], x_sample.shape[1]
    assert 8 % dec == 0 and seq % WINDOW == 0

    win, wout, wpool, wple = w_in.astype(BF16), w_out.astype(BF16), w_pool.astype(BF16), w_ple.astype(BF16)
    g_mix_pre = norm_mix_pre.reshape(depth, 1, D_MODEL)
    g_mix_post = norm_mix_post.reshape(depth, 1, D_MODEL)
    g_ffn_pre = norm_ffn_pre.reshape(depth, 1, D_MODEL)
    g_ffn_post = norm_ffn_post.reshape(depth, 1, D_MODEL)
    pscale = pool_scale.reshape(depth, 1, POOL_WIDTH)

    pp = p_prompt.reshape(depth, nbat * seq, PLE_DIM)
    ps = p_sample.reshape(depth, dec_b * dec, PLE_DIM)
    ck = cache_k.transpose(0, 1, 3, 4, 2).reshape(depth, dec_b, KV_WIDTH, WINDOW)
    cv = cache_v.transpose(0, 1, 3, 4, 2).reshape(depth, dec_b, KV_WIDTH, WINDOW)

    yp = x_prompt
    kp_l, vp_l, sp_l = [], [], []
    for i in range(depth):
        yp, kp, vp, sp, wg, wu, wd, wpg = _mix_prompt(yp, attn_sinks, g_mix_pre, g_mix_post, win, wout, wpool, pscale,
                                                      w_gate, w_up, w_down, w_ple_gate, layer=i, tm=1024, piece=256)
        yp = _ffn(yp.reshape(nbat * seq, D_MODEL), pp, g_ffn_pre, g_ffn_post, wg, wu, wd, wple, wpg,
                  layer=i, tm=512, nsub=1).reshape(nbat, seq, D_MODEL)
        kp_l.append(kp)
        vp_l.append(vp)
        sp_l.append(sp[:, POOL_PAD - POOL_STATE:])

    ys, ks, vs, ss = _sample_path(x_sample.reshape(dec_b * dec, D_MODEL), ps, attn_sinks, ck, cv,
                                  state_pool.transpose(0, 2, 1, 3),
                                  g_mix_pre, g_mix_post, g_ffn_pre, g_ffn_post, win, wout, wpool, pscale,
                                  w_gate, w_up, w_down, wple, w_ple_gate, bb=8, dec=dec, nsub=2)

    kv_shape = (depth, -1, WINDOW, N_KV_HEADS, HEAD_DIM)
    return (yp, ys.reshape(dec_b, dec, D_MODEL),
            jnp.stack(kp_l).reshape(kv_shape), jnp.stack(vp_l).reshape(kv_shape), jnp.stack(sp_l),
            _cache_from_device_layout(ks), _cache_from_device_layout(vs), ss.transpose(0, 2, 1, 3))
```

```python
import functools

import jax
import jax.numpy as jnp
from jax import lax
from jax.experimental import pallas as pl
from jax.experimental.pallas import tpu as pltpu

D_MODEL = 1024
DEPTH = 4
ATTN_WIDTH = 512
HEAD_DIM = 64
N_HEADS = 8
N_KV_HEADS = 2
GROUP = 4
KV_WIDTH = 128
WINDOW = 128
POOL_WIDTH = 512
POOL_WINDOWS = (2, 4, 8, 16)
POOL_GROUP_WIDTH = 128
POOL_STATE = 15
IN_WIDTH = 1280
D_FF = 2816
PLE_DIM = 256
EPS = 1e-6

K_OFF = ATTN_WIDTH
V_OFF = ATTN_WIDTH + KV_WIDTH
U_OFF = ATTN_WIDTH + 2 * KV_WIDTH
LANES = 128
POOL_PAD = 16
POOL_HEAD = 8 + POOL_PAD
LOG2E = 1.4426950408889634
FF_CHUNKS = ((0, 1024), (1024, 2048), (2048, D_FF))

BF16 = jnp.bfloat16
F32 = jnp.float32
VMEM_LIMIT = 56 * 1024 * 1024


def _rms(x, g):
    return x * lax.rsqrt(jnp.mean(x * x, axis=-1, keepdims=True) + EPS) * g


def _sigmoid(x):
    return 1.0 / (1.0 + jnp.exp(-x))


def _dot(a, b):
    return jnp.dot(a, b, preferred_element_type=F32)


def _dot_t(a, b):
    return lax.dot_general(a, b, (((1,), (1,)), ((), ())), preferred_element_type=F32)


def _const_spec(shape, layer=None):
    if layer is None:
        return pl.BlockSpec(shape, lambda *_: (0,) * len(shape), pipeline_mode=pl.Buffered(1))
    return pl.BlockSpec((None,) + shape, lambda *_: (layer,) + (0,) * len(shape),
                        pipeline_mode=pl.Buffered(1))


def _mix_prompt_kernel(sinks_ref, x_ref, gpre_ref, gpost_ref, win_ref, wout_ref, wpool_ref, pscale_ref,
                       wg32_ref, wu32_ref, wd32_ref, wpg32_ref,
                       y_ref, klast_ref, vlast_ref, plast_ref, wg16_ref, wu16_ref, wd16_ref, wpg16_ref,
                       z_ref, kext_ref, vt_ref, uext_ref, ps1_ref, ps2_ref, mix_ref, bias_ref, *, tm, piece, layer):
    s = pl.program_id(1)
    nblk = tm // WINDOW

    for w32_ref, w16_ref in ((wg32_ref, wg16_ref), (wu32_ref, wu16_ref), (wd32_ref, wd16_ref),
                             (wpg32_ref, wpg16_ref)):
        w16_ref[...] = w32_ref[...].astype(BF16)

    @pl.when((pl.program_id(0) == 0) & (s == 0))
    def _():
        c = lax.broadcasted_iota(jnp.int32, (2 * WINDOW, WINDOW), 0)
        r = lax.broadcasted_iota(jnp.int32, (2 * WINDOW, WINDOW), 1)
        dist = r + WINDOW - c
        ok = (dist >= 0) & (dist < WINDOW)
        bias_ref[0] = jnp.where(ok, 0.0, -jnp.inf)
        bias_ref[1] = jnp.where(ok & (c >= WINDOW), 0.0, -jnp.inf)

    @pl.when(s == 0)
    def _():
        kext_ref[0:WINDOW, :] = jnp.zeros((WINDOW, KV_WIDTH), F32)
        vt_ref[:, 0:WINDOW] = jnp.zeros((KV_WIDTH, WINDOW), F32)
        uext_ref[0:POOL_HEAD, :] = jnp.zeros((POOL_HEAD, POOL_WIDTH), F32)
        ps1_ref[0:8, :] = jnp.zeros((8, POOL_GROUP_WIDTH), F32)
        ps2_ref[0:8, :] = jnp.zeros((8, POOL_GROUP_WIDTH), F32)

    lane = lax.broadcasted_iota(jnp.int32, (WINDOW, LANES), 1)
    lo = lane < HEAD_DIM
    lo2 = jnp.concatenate([lo, lo], axis=0)
    row8 = lax.broadcasted_iota(jnp.int32, (8, GROUP * WINDOW), 0)
    kcol = lax.broadcasted_iota(jnp.int32, (HEAD_DIM, 2 * WINDOW), 1)
    ones = jnp.ones((HEAD_DIM, 2 * WINDOW), F32)
    half = piece

    def project_in(hf):
        hr = slice(hf * half, (hf + 1) * half)
        h = _rms(x_ref[0, hr, :], gpre_ref[...]).astype(BF16)
        z_ref[hr, :] = _dot(h, win_ref[...])
        uext_ref[POOL_HEAD + hf * half:POOL_HEAD + (hf + 1) * half, :] = z_ref[hr, U_OFF:U_OFF + POOL_WIDTH]
        kext_ref[WINDOW + hf * half:WINDOW + (hf + 1) * half, :] = z_ref[hr, K_OFF:K_OFF + KV_WIDTH]
        for n in range(hf * half // WINDOW, (hf + 1) * half // WINDOW):
            r0 = n * WINDOW
            vt_ref[:, WINDOW + r0:2 * WINDOW + r0] = z_ref[r0:r0 + WINDOW, V_OFF:V_OFF + KV_WIDTH].T

    def attend_stages(blocks):
        st8 = {}

        def scores():
            for n in blocks:
                r0 = n * WINDOW
                rows = slice(r0, r0 + WINDOW)
                kcat = kext_ref[r0:r0 + 2 * WINDOW, :]
                kswap = pltpu.roll(kcat, HEAD_DIM, 1)
                bias1 = bias_ref[jnp.where(s == 0, 1, 0)] if n == 0 else bias_ref[0]
                bias = jnp.concatenate([bias1] * GROUP, axis=1)
                for kv in range(N_KV_HEADS):
                    kk = (jnp.where(lo2, kcat, kswap) if kv == 0 else jnp.where(lo2, kswap, kcat)).astype(BF16)
                    pieces, sinks = [], []
                    for g in range(GROUP):
                        hd = kv * GROUP + g
                        slab = hd // 2
                        qs = z_ref[rows, slab * LANES:(slab + 1) * LANES] * (HEAD_DIM ** -0.5 * LOG2E)
                        keep = lo if hd % 2 == 0 else jnp.logical_not(lo)
                        pieces.append(jnp.where(keep, qs, 0.0).astype(BF16))
                        sinks.append(jnp.full((1, WINDOW), sinks_ref[layer, hd] * LOG2E, F32))
                    q4 = jnp.concatenate(pieces, axis=0)
                    st8[n, kv, "sink"] = jnp.concatenate(sinks, axis=1)
                    st8[n, kv, "st"] = _dot_t(kk, q4) + bias

        def softmax():
            for n in blocks:
                for kv in range(N_KV_HEADS):
                    st, sink = st8.pop((n, kv, "st")), st8.pop((n, kv, "sink"))
                    m = jnp.maximum(jnp.max(st, axis=0, keepdims=True), sink)
                    p = jnp.exp2(st - m)
                    top = jnp.where(row8 == 0, jnp.exp2(sink - m), p[0:8])
                    st8[n, kv, "p"] = jnp.concatenate([top, p[8:]], axis=0).astype(BF16)

        def values():
            for n in blocks:
                r0 = n * WINDOW
                for kv in range(N_KV_HEADS):
                    vth = vt_ref[kv * HEAD_DIM:(kv + 1) * HEAD_DIM, r0:r0 + 2 * WINDOW]
                    a = jnp.concatenate([jnp.where(kcol == 0, 0.0, vth), ones], axis=0).astype(BF16)
                    st8[n, kv, "ot"] = _dot(a, st8.pop((n, kv, "p")))

        def store():
            for n in blocks:
                rows = slice(n * WINDOW, (n + 1) * WINDOW)
                for kv in range(N_KV_HEADS):
                    ot = st8.pop((n, kv, "ot"))
                    inv = 1.0 / ot[HEAD_DIM:HEAD_DIM + 8]
                    on = ot[0:HEAD_DIM] * jnp.concatenate([inv] * (HEAD_DIM // 8), axis=0)
                    for j in range(GROUP // 2):
                        c0 = 2 * j * WINDOW
                        pair = jnp.concatenate([on[:, c0:c0 + WINDOW], on[:, c0 + WINDOW:c0 + 2 * WINDOW]], axis=0)
                        slab = kv * (GROUP // 2) + j
                        mix_ref[rows, slab * LANES:(slab + 1) * LANES] = pair.T.astype(BF16)

        return scores, softmax, values, store

    def pool(hf):
        t0 = hf * half
        pos = lax.broadcasted_iota(jnp.int32, (half, 1), 0) + (s * tm + t0)
        first = POOL_HEAD + t0
        ext = slice(first - POOL_PAD, first + half)
        for gi, w in enumerate(POOL_WINDOWS):
            cols = slice(gi * POOL_GROUP_WIDTH, (gi + 1) * POOL_GROUP_WIDTH)
            src, src_cols, span, bufs = uext_ref, cols, 1, [ps1_ref, ps2_ref]
            while 2 * span < w:
                dst = bufs[0]
                dst[ext, :] = src[ext, src_cols] + src[first - POOL_PAD - span:first + half - span, src_cols]
                src, src_cols, span, bufs = dst, slice(None), 2 * span, bufs[::-1]
            tot = src[first:first + half, src_cols] + src[first - span:first - span + half, src_cols]
            cnt = jnp.minimum(w, pos + 1).astype(F32)
            dlt = (tot / cnt - uext_ref[first:first + half, cols]).astype(BF16)
            yp = _dot(dlt, wpool_ref[gi]) * pscale_ref[:, cols]
            mix_ref[t0:t0 + half, ATTN_WIDTH + gi * POOL_GROUP_WIDTH:ATTN_WIDTH + (gi + 1) * POOL_GROUP_WIDTH] = (
                yp.astype(BF16))

    def project_out(hf):
        hr = slice(hf * half, (hf + 1) * half)
        mixed = _dot(mix_ref[hr, :], wout_ref[...])
        y_ref[0, hr, :] = x_ref[0, hr, :] + _rms(mixed, gpost_ref[...])

    per_half = half // WINDOW
    n_piece = tm // half
    project_in(0)
    for hf in range(n_piece):
        scores, softmax, values, store = attend_stages(range(hf * per_half, (hf + 1) * per_half))
        scores()
        if hf + 1 < n_piece:
            project_in(hf + 1)
        if hf > 0:
            project_out(hf - 1)
        softmax()
        values()
        store()
        pool(hf)
    project_out(n_piece - 1)

    @pl.when(s == pl.num_programs(1) - 1)
    def _():
        klast_ref[0] = z_ref[tm - WINDOW:tm, K_OFF:K_OFF + KV_WIDTH]
        vlast_ref[0] = z_ref[tm - WINDOW:tm, V_OFF:V_OFF + KV_WIDTH]
        plast_ref[0] = uext_ref[POOL_HEAD + tm - POOL_PAD:POOL_HEAD + tm, :]

    kext_ref[0:WINDOW, :] = kext_ref[tm:tm + WINDOW, :]
    vt_ref[:, 0:WINDOW] = vt_ref[:, tm:tm + WINDOW]
    uext_ref[POOL_HEAD - POOL_PAD:POOL_HEAD, :] = uext_ref[POOL_HEAD + tm - POOL_PAD:POOL_HEAD + tm, :]


def _row_chunk_specs(shape, n_steps, layer, step_of):
    rows, cols = shape
    chunk = rows // n_steps
    assert chunk * n_steps == rows and chunk % 16 == 0, (shape, n_steps)
    return (pl.BlockSpec((None, chunk, cols), lambda *g: (layer, step_of(*g), 0)),
            pl.BlockSpec((chunk, cols), lambda *g: (step_of(*g), 0)))


def _mix_prompt(x, sinks, gpre, gpost, win, wout, wpool, pscale, wg32, wu32, wd32, wpg32, *, layer, tm, piece):
    b, s, _ = x.shape
    kern = functools.partial(_mix_prompt_kernel, tm=tm, piece=piece, layer=layer)
    n_tiles = s // tm
    cast_shapes = ((D_MODEL, D_FF), (D_MODEL, D_FF), (D_FF, D_MODEL), (D_MODEL, D_MODEL))
    cast_specs = [_row_chunk_specs(shape, b * n_tiles, layer, lambda i, j: i * n_tiles + j) for shape in cast_shapes]
    return pl.pallas_call(
        kern,
        grid=(b, n_tiles),
        in_specs=[
            pl.BlockSpec(memory_space=pltpu.SMEM),
            pl.BlockSpec((1, tm, D_MODEL), lambda i, j: (i, j, 0)),
            _const_spec((1, D_MODEL), layer),
            _const_spec((1, D_MODEL), layer),
            _const_spec((D_MODEL, IN_WIDTH), layer),
            _const_spec((D_MODEL, D_MODEL), layer),
            _const_spec((len(POOL_WINDOWS), POOL_GROUP_WIDTH, POOL_GROUP_WIDTH), layer),
            _const_spec((1, POOL_WIDTH), layer),
        ] + [spec_in for spec_in, _ in cast_specs],
        out_specs=[
            pl.BlockSpec((1, tm, D_MODEL), lambda i, j: (i, j, 0)),
            pl.BlockSpec((1, WINDOW, KV_WIDTH), lambda i, j: (i, 0, 0)),
            pl.BlockSpec((1, WINDOW, KV_WIDTH), lambda i, j: (i, 0, 0)),
            pl.BlockSpec((1, POOL_PAD, POOL_WIDTH), lambda i, j: (i, 0, 0)),
        ] + [spec_out for _, spec_out in cast_specs],
        out_shape=[
            jax.ShapeDtypeStruct((b, s, D_MODEL), F32),
            jax.ShapeDtypeStruct((b, WINDOW, KV_WIDTH), F32),
            jax.ShapeDtypeStruct((b, WINDOW, KV_WIDTH), F32),
            jax.ShapeDtypeStruct((b, POOL_PAD, POOL_WIDTH), F32),
        ] + [jax.ShapeDtypeStruct(shape, BF16) for shape in cast_shapes],
        scratch_shapes=[
            pltpu.VMEM((tm, IN_WIDTH), F32),
            pltpu.VMEM((WINDOW + tm, KV_WIDTH), F32),
            pltpu.VMEM((KV_WIDTH, WINDOW + tm), F32),
            pltpu.VMEM((POOL_HEAD + tm, POOL_WIDTH), F32),
            pltpu.VMEM((POOL_HEAD + tm, POOL_GROUP_WIDTH), F32),
            pltpu.VMEM((POOL_HEAD + tm, POOL_GROUP_WIDTH), F32),
            pltpu.VMEM((tm, D_MODEL), BF16),
            pltpu.VMEM((2, 2 * WINDOW, WINDOW), F32),
        ],
        compiler_params=pltpu.CompilerParams(
            dimension_semantics=("arbitrary", "arbitrary"), vmem_limit_bytes=VMEM_LIMIT),
        name=f"mix_prompt_{layer}",
    )(sinks, x, gpre, gpost, win, wout, wpool, pscale, wg32, wu32, wd32, wpg32)


def _swiglu(f, wg_ref, wu_ref, wd_ref):
    d, pending = None, None
    for c0, c1 in FF_CHUNKS:
        g = _dot(f, wg_ref[:, c0:c1])
        u = _dot(f, wu_ref[:, c0:c1])
        if pending is not None:
            part = _dot(pending[0], wd_ref[pending[1]:pending[2], :])
            d = part if d is None else d + part
        pending = ((g * _sigmoid(g) * u).astype(BF16), c0, c1)
    return d + _dot(pending[0], wd_ref[pending[1]:pending[2], :])


def _ffn_rows(x, p, gpre_ref, gpost_ref, wg_ref, wu_ref, wd_ref, wple_ref, wpg_ref):
    f = _rms(x, gpre_ref[...]).astype(BF16)
    x = x + _rms(_swiglu(f, wg_ref, wu_ref, wd_ref), gpost_ref[...])
    gate = _sigmoid(_dot(x.astype(BF16), wpg_ref[...]))
    return x + gate * _dot(p.astype(BF16), wple_ref[...])


def _ffn_kernel(x_ref, p_ref, gpre_ref, gpost_ref, wg_ref, wu_ref, wd_ref, wple_ref, wpg_ref, o_ref, *, nsub):
    sub = x_ref.shape[0] // nsub
    for i in range(nsub):
        rows = slice(i * sub, (i + 1) * sub)
        o_ref[rows, :] = _ffn_rows(x_ref[rows, :], p_ref[rows, :], gpre_ref, gpost_ref, wg_ref, wu_ref, wd_ref,
                                   wple_ref, wpg_ref)


def _ffn(x, p, gpre, gpost, wg, wu, wd, wple, wpg, *, layer, tm, nsub):
    n = x.shape[0]
    return pl.pallas_call(
        functools.partial(_ffn_kernel, nsub=nsub),
        grid=(n // tm,),
        in_specs=[
            pl.BlockSpec((tm, D_MODEL), lambda i: (i, 0)),
            pl.BlockSpec((None, tm, PLE_DIM), lambda i: (layer, i, 0)),
            _const_spec((1, D_MODEL), layer),
            _const_spec((1, D_MODEL), layer),
            _const_spec((D_MODEL, D_FF)),
            _const_spec((D_MODEL, D_FF)),
            _const_spec((D_FF, D_MODEL)),
            _const_spec((PLE_DIM, D_MODEL), layer),
            _const_spec((D_MODEL, D_MODEL)),
        ],
        out_specs=pl.BlockSpec((tm, D_MODEL), lambda i: (i, 0)),
        out_shape=jax.ShapeDtypeStruct((n, D_MODEL), F32),
        compiler_params=pltpu.CompilerParams(
            dimension_semantics=("arbitrary",), vmem_limit_bytes=VMEM_LIMIT),
        name=f"ffn_prompt_{layer}",
    )(x, p, gpre, gpost, wg, wu, wd, wple, wpg)


def _layer_spec(shape, single=False):
    kwargs = dict(pipeline_mode=pl.Buffered(1)) if single else {}
    return pl.BlockSpec((None,) + shape, lambda l, c: (l,) + (0,) * len(shape), **kwargs)


def _sample_kernel(sinks_ref, xs_ref, ps_ref, ck_ref, cv_ref, st_ref, gmpre_ref, gmpost_ref, gfpre_ref, gfpost_ref,
                   win_ref, wout_ref, wpool_ref, pscale_ref, wg32_ref, wu32_ref, wd32_ref, wple_ref, wpg32_ref,
                   y_ref, ko_ref, vo_ref, po_ref, z_ref, mix_ref, ustage_ref, dlt_ref,
                   wg_ref, wu_ref, wd_ref, wpg_ref, *, bb, dec, nsub):
    l = pl.program_id(0)
    c = pl.program_id(1)
    n_rows = y_ref.shape[0]

    @pl.when((l == 0) & (c == 0))
    def _():
        y_ref[...] = xs_ref[...]

    @pl.when(c == 0)
    def _():
        h = _rms(y_ref[...], gmpre_ref[...]).astype(BF16)
        z_ref[...] = _dot(h, win_ref[...])

    for w32_ref, w16_ref in ((wg32_ref, wg_ref), (wu32_ref, wu_ref), (wd32_ref, wd_ref), (wpg32_ref, wpg_ref)):
        rows_w = w32_ref.shape[0]
        w16_ref[pl.ds(pl.multiple_of(c * rows_w, 16), rows_w), :] = w32_ref[...].astype(BF16)

    per_tile = 8 // dec
    base = pl.multiple_of(c * (bb * dec), 8)
    lane = lax.broadcasted_iota(jnp.int32, (8, LANES), 1)
    lo = lane < HEAD_DIM
    hi = jnp.logical_not(lo)
    rows64 = N_HEADS * 8
    ext_w = 2 * WINDOW
    new0 = ext_w - 8
    trow = lax.broadcasted_iota(jnp.int32, (rows64, ext_w), 0) & 7
    tstep = trow % dec
    tbat = trow // dec
    klane = lax.broadcasted_iota(jnp.int32, (rows64, ext_w), 1)
    fresh = klane - new0
    valid = ((klane < WINDOW) & (klane >= tstep + 1)) | (
        (klane >= new0) & (fresh // dec == tbat) & (fresh % dec <= tstep))
    tbat_o = tbat[:, 0:LANES]
    wlane = lax.broadcasted_iota(jnp.int32, (KV_WIDTH, WINDOW), 1)
    zpad = jnp.zeros((WINDOW - 8, KV_WIDTH), F32)

    tiles = range(bb // per_tile)
    st8 = {}

    for t in tiles:
        rows = pl.ds(base + 8 * t, 8)
        q = z_ref[rows, 0:ATTN_WIDTH] * (HEAD_DIM ** -0.5)
        knew = jnp.concatenate([zpad, z_ref[rows, K_OFF:K_OFF + KV_WIDTH]], axis=0)
        vnew = jnp.concatenate([zpad, z_ref[rows, V_OFF:V_OFF + KV_WIDTH]], axis=0)
        st8[t, "knew"], st8[t, "vnew"] = knew, vnew
        pieces, sinks = [], []
        for hd in range(N_HEADS):
            slab, half, kv = hd // 2, hd % 2, hd // GROUP
            qs = q[:, slab * LANES:(slab + 1) * LANES]
            src = qs if half == kv else pltpu.roll(qs, HEAD_DIM, 1)
            pieces.append(jnp.where(lo if kv == 0 else hi, src, 0.0))
            sinks.append(jnp.full((8, 1), sinks_ref[l, hd], F32))
        lhs_b = jnp.concatenate(pieces, axis=0).astype(BF16)
        st8[t, "sink"] = jnp.concatenate(sinks, axis=0)
        sc = None
        for i in range(per_tile):
            si = _dot(lhs_b, ck_ref[per_tile * t + i].astype(BF16))
            sc = si if sc is None else jnp.where(tbat_o == i, si, sc)
        sc = jnp.concatenate([sc, _dot_t(lhs_b, knew.astype(BF16))], axis=1)
        st8[t, "sc"] = jnp.where(valid, sc, -jnp.inf)

    for t in tiles:
        sc, sink = st8.pop((t, "sc")), st8.pop((t, "sink"))
        m = jnp.maximum(jnp.max(sc, axis=-1, keepdims=True), sink)
        p = jnp.exp(sc - m)
        st8[t, "denom"] = jnp.sum(p, axis=-1, keepdims=True) + jnp.exp(sink - m)
        st8[t, "pb"] = p.astype(BF16)

    for t in tiles:
        pb = st8.pop((t, "pb"))
        o = None
        for i in range(per_tile):
            oi = _dot_t(pb[:, 0:WINDOW], cv_ref[per_tile * t + i].astype(BF16))
            o = oi if o is None else jnp.where(tbat_o == i, oi, o)
        st8[t, "o"] = o + _dot(pb[:, WINDOW:ext_w], st8[t, "vnew"].astype(BF16))

    for t in tiles:
        rows = pl.ds(base + 8 * t, 8)
        o = st8.pop((t, "o")) / st8.pop((t, "denom"))
        for slab in range(N_HEADS // 2):
            kv = (2 * slab) // GROUP
            even = o[16 * slab:16 * slab + 8]
            odd = o[16 * slab + 8:16 * slab + 16]
            if kv == 0:
                res = jnp.where(lo, even, pltpu.roll(odd, HEAD_DIM, 1))
            else:
                res = jnp.where(lo, pltpu.roll(even, HEAD_DIM, 1), odd)
            mix_ref[rows, slab * LANES:(slab + 1) * LANES] = res

    for t in tiles:
        for new, c_ref, o_ref in ((st8.pop((t, "knew")), ck_ref, ko_ref), (st8.pop((t, "vnew")), cv_ref, vo_ref)):
            cols = new.T
            for bi in range(per_tile):
                b = per_tile * t + bi
                shift = dec * (per_tile - 1 - bi)
                mine = cols if shift == 0 else pltpu.roll(cols, shift, 1)
                kept = pltpu.roll(c_ref[b], WINDOW - dec, 1)
                o_ref[b] = jnp.where(wlane < WINDOW - dec, kept, mine)

    chunk = pl.ds(base, bb * dec)
    for r in range(POOL_STATE - dec):
        po_ref[r] = st_ref[r + dec]
    for gi, w in enumerate(POOL_WINDOWS):
        cols = slice(gi * POOL_GROUP_WIDTH, (gi + 1) * POOL_GROUP_WIDTH)
        ustage_ref[gi] = z_ref[chunk, U_OFF + gi * POOL_GROUP_WIDTH:U_OFF + (gi + 1) * POOL_GROUP_WIDTH]
        u = [ustage_ref[gi, pl.ds(j, bb, stride=dec), :] for j in range(dec)]
        suffix, acc = {}, None
        for m in range(1, w):
            row = st_ref[POOL_STATE - m, :, cols]
            acc = row if acc is None else acc + row
            suffix[m] = acc
        for j in range(dec):
            po_ref[POOL_STATE - dec + j, :, cols] = u[j]
            tot = suffix.get(w - 1 - j)
            for i in range(max(0, j - w + 1), j + 1):
                tot = u[i] if tot is None else tot + u[i]
            dlt_ref[gi, pl.ds(j, bb, stride=dec), :] = tot / float(w) - u[j]

    for gi in range(len(POOL_WINDOWS)):
        cols = slice(gi * POOL_GROUP_WIDTH, (gi + 1) * POOL_GROUP_WIDTH)
        yp = _dot(dlt_ref[gi].astype(BF16), wpool_ref[gi]) * pscale_ref[:, cols]
        mix_ref[chunk, ATTN_WIDTH + gi * POOL_GROUP_WIDTH:ATTN_WIDTH + (gi + 1) * POOL_GROUP_WIDTH] = yp

    @pl.when(c == pl.num_programs(1) - 1)
    def _():
        sub = n_rows // nsub
        for i in range(nsub):
            rs = slice(i * sub, (i + 1) * sub)
            mixed = _dot(mix_ref[rs, :].astype(BF16), wout_ref[...])
            x = y_ref[rs, :] + _rms(mixed, gmpost_ref[...])
            y_ref[rs, :] = _ffn_rows(x, ps_ref[rs, :], gfpre_ref, gfpost_ref, wg_ref, wu_ref, wd_ref, wple_ref,
                                     wpg_ref)


def _sample_path(xs, ps, sinks, ck, cv, st, gmpre, gmpost, gfpre, gfpost, win, wout, wpool, pscale,
                 wg, wu, wd, wple, wpg, *, bb, dec, nsub):
    depth, nb = ck.shape[0], ck.shape[1]
    n = nb * dec
    kern = functools.partial(_sample_kernel, bb=bb, dec=dec, nsub=nsub)
    n_chunks = nb // bb

    def _chunk_spec(rows, cols):
        assert rows % n_chunks == 0 and (rows // n_chunks) % 16 == 0
        return pl.BlockSpec((None, rows // n_chunks, cols), lambda l, c: (l, c, 0))

    cache_spec = pl.BlockSpec((None, bb, KV_WIDTH, WINDOW), lambda l, c: (l, c, 0, 0))
    state_spec = pl.BlockSpec((None, POOL_STATE, bb, POOL_WIDTH), lambda l, c: (l, 0, c, 0))
    return pl.pallas_call(
        kern,
        grid=(depth, nb // bb),
        in_specs=[
            pl.BlockSpec(memory_space=pltpu.SMEM),
            pl.BlockSpec((n, D_MODEL), lambda l, c: (0, 0), pipeline_mode=pl.Buffered(1)),
            _layer_spec((n, PLE_DIM)),
            cache_spec, cache_spec, state_spec,
            _layer_spec((1, D_MODEL)), _layer_spec((1, D_MODEL)), _layer_spec((1, D_MODEL)), _layer_spec((1, D_MODEL)),
            _layer_spec((D_MODEL, IN_WIDTH), single=True),
            _layer_spec((D_MODEL, D_MODEL), single=True),
            _layer_spec((len(POOL_WINDOWS), POOL_GROUP_WIDTH, POOL_GROUP_WIDTH)),
            _layer_spec((1, POOL_WIDTH)),
            _chunk_spec(D_MODEL, D_FF),
            _chunk_spec(D_MODEL, D_FF),
            _chunk_spec(D_FF, D_MODEL),
            _layer_spec((PLE_DIM, D_MODEL), single=True),
            _chunk_spec(D_MODEL, D_MODEL),
        ],
        out_specs=[
            pl.BlockSpec((n, D_MODEL), lambda l, c: (0, 0)),
            cache_spec, cache_spec, state_spec,
        ],
        out_shape=[
            jax.ShapeDtypeStruct((n, D_MODEL), F32),
            jax.ShapeDtypeStruct((depth, nb, KV_WIDTH, WINDOW), F32),
            jax.ShapeDtypeStruct((depth, nb, KV_WIDTH, WINDOW), F32),
            jax.ShapeDtypeStruct((depth, POOL_STATE, nb, POOL_WIDTH), F32),
        ],
        scratch_shapes=[
            pltpu.VMEM((n, IN_WIDTH), F32),
            pltpu.VMEM((n, D_MODEL), F32),
            pltpu.VMEM((len(POOL_WINDOWS), bb * dec, POOL_GROUP_WIDTH), F32),
            pltpu.VMEM((len(POOL_WINDOWS), bb * dec, POOL_GROUP_WIDTH), F32),
            pltpu.VMEM((D_MODEL, D_FF), BF16),
            pltpu.VMEM((D_MODEL, D_FF), BF16),
            pltpu.VMEM((D_FF, D_MODEL), BF16),
            pltpu.VMEM((D_MODEL, D_MODEL), BF16),
        ],
        compiler_params=pltpu.CompilerParams(
            dimension_semantics=("arbitrary", "arbitrary"), vmem_limit_bytes=VMEM_LIMIT),
        name="sample_path",
    )(sinks, xs, ps, ck, cv, st, gmpre, gmpost, gfpre, gfpost, win, wout, wpool, pscale, wg, wu, wd, wple, wpg)


def _cache_from_device_layout(c):
    return c.reshape(c.shape[0], c.shape[1], N_KV_HEADS, HEAD_DIM, WINDOW).transpose(0, 1, 4, 2, 3)


def kernel(x_prompt, x_sample, p_prompt, p_sample, cache_k, cache_v, state_pool, norm_mix_pre, norm_mix_post,
           norm_ffn_pre, norm_ffn_post, w_in, w_out, attn_sinks, w_pool, pool_scale, w_gate, w_up, w_down, w_ple,
           w_ple_gate):
    depth, nbat, seq, _ = p_prompt.shape
    dec_b, dec = x_sample.shape[0], x_sample.shape[1]
    assert 8 % dec == 0 and seq % WINDOW == 0

    win, wout, wpool, wple = w_in.astype(BF16), w_out.astype(BF16), w_pool.astype(BF16), w_ple.astype(BF16)
    g_mix_pre = norm_mix_pre.reshape(depth, 1, D_MODEL)
    g_mix_post = norm_mix_post.reshape(depth, 1, D_MODEL)
    g_ffn_pre = norm_ffn_pre.reshape(depth, 1, D_MODEL)
    g_ffn_post = norm_ffn_post.reshape(depth, 1, D_MODEL)
    pscale = pool_scale.reshape(depth, 1, POOL_WIDTH)

    pp = p_prompt.reshape(depth, nbat * seq, PLE_DIM)
    ps = p_sample.reshape(depth, dec_b * dec, PLE_DIM)
    ck = cache_k.transpose(0, 1, 3, 4, 2).reshape(depth, dec_b, KV_WIDTH, WINDOW)
    cv = cache_v.transpose(0, 1, 3, 4, 2).reshape(depth, dec_b, KV_WIDTH, WINDOW)

    yp = x_prompt
    kp_l, vp_l, sp_l = [], [], []
    for i in range(depth):
        yp, kp, vp, sp, wg, wu, wd, wpg = _mix_prompt(yp, attn_sinks, g_mix_pre, g_mix_post, win, wout, wpool, pscale,
                                                      w_gate, w_up, w_down, w_ple_gate, layer=i, tm=1024, piece=512)
        yp = _ffn(yp.reshape(nbat * seq, D_MODEL), pp, g_ffn_pre, g_ffn_post, wg, wu, wd, wple, wpg,
                  layer=i, tm=1024, nsub=2).reshape(nbat, seq, D_MODEL)
        kp_l.append(kp)
        vp_l.append(vp)
        sp_l.append(sp[:, POOL_PAD - POOL_STATE:])

    ys, ks, vs, ss = _sample_path(x_sample.reshape(dec_b * dec, D_MODEL), ps, attn_sinks, ck, cv,
                                  state_pool.transpose(0, 2, 1, 3),
                                  g_mix_pre, g_mix_post, g_ffn_pre, g_ffn_post, win, wout, wpool, pscale,
                                  w_gate, w_up, w_down, wple, w_ple_gate, bb=8, dec=dec, nsub=2)

    kv_shape = (depth, -1, WINDOW, N_KV_HEADS, HEAD_DIM)
    return (yp, ys.reshape(dec_b, dec, D_MODEL),
            jnp.stack(kp_l).reshape(kv_shape), jnp.stack(vp_l).reshape(kv_shape), jnp.stack(sp_l),
            _cache_from_device_layout(ks), _cache_from_device_layout(vs), ss.transpose(0, 2, 1, 3))
```

```python
import functools

import jax
import jax.numpy as jnp
from jax import lax
from jax.experimental import pallas as pl
from jax.experimental.pallas import tpu as pltpu

D_MODEL = 1024
DEPTH = 4
ATTN_WIDTH = 512
HEAD_DIM = 64
N_HEADS = 8
N_KV_HEADS = 2
GROUP = 4
KV_WIDTH = 128
WINDOW = 128
POOL_WIDTH = 512
POOL_WINDOWS = (2, 4, 8, 16)
POOL_GROUP_WIDTH = 128
POOL_STATE = 15
IN_WIDTH = 1280
D_FF = 2816
PLE_DIM = 256
EPS = 1e-6

K_OFF = ATTN_WIDTH
V_OFF = ATTN_WIDTH + KV_WIDTH
U_OFF = ATTN_WIDTH + 2 * KV_WIDTH
LANES = 128
POOL_PAD = 16
POOL_HEAD = 8 + POOL_PAD
LOG2E = 1.4426950408889634
FF_CHUNKS = ((0, 1024), (1024, 2048), (2048, D_FF))

BF16 = jnp.bfloat16
F32 = jnp.float32
VMEM_LIMIT = 56 * 1024 * 1024


def _rms(x, g):
    return x * lax.rsqrt(jnp.mean(x * x, axis=-1, keepdims=True) + EPS) * g


def _sigmoid(x):
    return 1.0 / (1.0 + jnp.exp(-x))


def _dot(a, b):
    return jnp.dot(a, b, preferred_element_type=F32)


def _dot_t(a, b):
    return lax.dot_general(a, b, (((1,), (1,)), ((), ())), preferred_element_type=F32)


def _const_spec(shape, layer=None):
    if layer is None:
        return pl.BlockSpec(shape, lambda *_: (0,) * len(shape), pipeline_mode=pl.Buffered(1))
    return pl.BlockSpec((None,) + shape, lambda *_: (layer,) + (0,) * len(shape),
                        pipeline_mode=pl.Buffered(1))


def _mix_prompt_kernel(sinks_ref, x_ref, gpre_ref, gpost_ref, win_ref, wout_ref, wpool_ref, pscale_ref,
                       wg32_ref, wu32_ref, wd32_ref, wpg32_ref,
                       y_ref, klast_ref, vlast_ref, plast_ref, wg16_ref, wu16_ref, wd16_ref, wpg16_ref,
                       z_ref, kext_ref, vt_ref, uext_ref, ps1_ref, ps2_ref, mix_ref, bias_ref, *, tm, piece, layer):
    s = pl.program_id(1)
    nblk = tm // WINDOW

    for w32_ref, w16_ref in ((wg32_ref, wg16_ref), (wu32_ref, wu16_ref), (wd32_ref, wd16_ref),
                             (wpg32_ref, wpg16_ref)):
        w16_ref[...] = w32_ref[...].astype(BF16)

    @pl.when((pl.program_id(0) == 0) & (s == 0))
    def _():
        c = lax.broadcasted_iota(jnp.int32, (2 * WINDOW, WINDOW), 0)
        r = lax.broadcasted_iota(jnp.int32, (2 * WINDOW, WINDOW), 1)
        dist = r + WINDOW - c
        ok = (dist >= 0) & (dist < WINDOW)
        bias_ref[0] = jnp.where(ok, 0.0, -jnp.inf)
        bias_ref[1] = jnp.where(ok & (c >= WINDOW), 0.0, -jnp.inf)

    @pl.when(s == 0)
    def _():
        kext_ref[0:WINDOW, :] = jnp.zeros((WINDOW, KV_WIDTH), F32)
        vt_ref[:, 0:WINDOW] = jnp.zeros((KV_WIDTH, WINDOW), F32)
        uext_ref[0:POOL_HEAD, :] = jnp.zeros((POOL_HEAD, POOL_WIDTH), F32)
        ps1_ref[0:8, :] = jnp.zeros((8, POOL_GROUP_WIDTH), F32)
        ps2_ref[0:8, :] = jnp.zeros((8, POOL_GROUP_WIDTH), F32)

    lane = lax.broadcasted_iota(jnp.int32, (WINDOW, LANES), 1)
    lo = lane < HEAD_DIM
    lo2 = jnp.concatenate([lo, lo], axis=0)
    row8 = lax.broadcasted_iota(jnp.int32, (8, GROUP * WINDOW), 0)
    kcol = lax.broadcasted_iota(jnp.int32, (HEAD_DIM, 2 * WINDOW), 1)
    ones = jnp.ones((HEAD_DIM, 2 * WINDOW), F32)
    half = piece

    def project_in(hf):
        hr = slice(hf * half, (hf + 1) * half)
        h = _rms(x_ref[0, hr, :], gpre_ref[...]).astype(BF16)
        z_ref[hr, :] = _dot(h, win_ref[...])
        uext_ref[POOL_HEAD + hf * half:POOL_HEAD + (hf + 1) * half, :] = z_ref[hr, U_OFF:U_OFF + POOL_WIDTH]
        kext_ref[WINDOW + hf * half:WINDOW + (hf + 1) * half, :] = z_ref[hr, K_OFF:K_OFF + KV_WIDTH]
        for n in range(hf * half // WINDOW, (hf + 1) * half // WINDOW):
            r0 = n * WINDOW
            vt_ref[:, WINDOW + r0:2 * WINDOW + r0] = z_ref[r0:r0 + WINDOW, V_OFF:V_OFF + KV_WIDTH].T

    def attend_stages(blocks):
        st8 = {}

        def scores():
            for n in blocks:
                r0 = n * WINDOW
                rows = slice(r0, r0 + WINDOW)
                kcat = kext_ref[r0:r0 + 2 * WINDOW, :]
                kswap = pltpu.roll(kcat, HEAD_DIM, 1)
                bias1 = bias_ref[jnp.where(s == 0, 1, 0)] if n == 0 else bias_ref[0]
                bias = jnp.concatenate([bias1] * GROUP, axis=1)
                for kv in range(N_KV_HEADS):
                    kk = (jnp.where(lo2, kcat, kswap) if kv == 0 else jnp.where(lo2, kswap, kcat)).astype(BF16)
                    pieces, sinks = [], []
                    for g in range(GROUP):
                        hd = kv * GROUP + g
                        slab = hd // 2
                        qs = z_ref[rows, slab * LANES:(slab + 1) * LANES] * (HEAD_DIM ** -0.5 * LOG2E)
                        keep = lo if hd % 2 == 0 else jnp.logical_not(lo)
                        pieces.append(jnp.where(keep, qs, 0.0).astype(BF16))
                        sinks.append(jnp.full((1, WINDOW), sinks_ref[layer, hd] * LOG2E, F32))
                    q4 = jnp.concatenate(pieces, axis=0)
                    st8[n, kv, "sink"] = jnp.concatenate(sinks, axis=1)
                    st8[n, kv, "st"] = _dot_t(kk, q4) + bias

        def softmax():
            for n in blocks:
                for kv in range(N_KV_HEADS):
                    st, sink = st8.pop((n, kv, "st")), st8.pop((n, kv, "sink"))
                    m = jnp.maximum(jnp.max(st, axis=0, keepdims=True), sink)
                    p = jnp.exp2(st - m)
                    top = jnp.where(row8 == 0, jnp.exp2(sink - m), p[0:8])
                    st8[n, kv, "p"] = jnp.concatenate([top, p[8:]], axis=0).astype(BF16)

        def values():
            for n in blocks:
                r0 = n * WINDOW
                for kv in range(N_KV_HEADS):
                    vth = vt_ref[kv * HEAD_DIM:(kv + 1) * HEAD_DIM, r0:r0 + 2 * WINDOW]
                    a = jnp.concatenate([jnp.where(kcol == 0, 0.0, vth), ones], axis=0).astype(BF16)
                    st8[n, kv, "ot"] = _dot(a, st8.pop((n, kv, "p")))

        def store():
            for n in blocks:
                rows = slice(n * WINDOW, (n + 1) * WINDOW)
                for kv in range(N_KV_HEADS):
                    ot = st8.pop((n, kv, "ot"))
                    inv = 1.0 / ot[HEAD_DIM:HEAD_DIM + 8]
                    on = ot[0:HEAD_DIM] * jnp.concatenate([inv] * (HEAD_DIM // 8), axis=0)
                    for j in range(GROUP // 2):
                        c0 = 2 * j * WINDOW
                        pair = jnp.concatenate([on[:, c0:c0 + WINDOW], on[:, c0 + WINDOW:c0 + 2 * WINDOW]], axis=0)
                        slab = kv * (GROUP // 2) + j
                        mix_ref[rows, slab * LANES:(slab + 1) * LANES] = pair.T.astype(BF16)

        return scores, softmax, values, store

    def pool(hf):
        t0 = hf * half
        pos = lax.broadcasted_iota(jnp.int32, (half, 1), 0) + (s * tm + t0)
        first = POOL_HEAD + t0
        ext = slice(first - POOL_PAD, first + half)
        for gi, w in enumerate(POOL_WINDOWS):
            cols = slice(gi * POOL_GROUP_WIDTH, (gi + 1) * POOL_GROUP_WIDTH)
            src, src_cols, span, bufs = uext_ref, cols, 1, [ps1_ref, ps2_ref]
            while 2 * span < w:
                dst = bufs[0]
                dst[ext, :] = src[ext, src_cols] + src[first - POOL_PAD - span:first + half - span, src_cols]
                src, src_cols, span, bufs = dst, slice(None), 2 * span, bufs[::-1]
            tot = src[first:first + half, src_cols] + src[first - span:first - span + half, src_cols]
            cnt = jnp.minimum(w, pos + 1).astype(F32)
            dlt = (tot / cnt - uext_ref[first:first + half, cols]).astype(BF16)
            yp = _dot(dlt, wpool_ref[gi]) * pscale_ref[:, cols]
            mix_ref[t0:t0 + half, ATTN_WIDTH + gi * POOL_GROUP_WIDTH:ATTN_WIDTH + (gi + 1) * POOL_GROUP_WIDTH] = (
                yp.astype(BF16))

    def project_out(hf):
        hr = slice(hf * half, (hf + 1) * half)
        mixed = _dot(mix_ref[hr, :], wout_ref[...])
        y_ref[0, hr, :] = x_ref[0, hr, :] + _rms(mixed, gpost_ref[...])

    per_half = half // WINDOW
    n_piece = tm // half
    project_in(0)
    for hf in range(n_piece):
        scores, softmax, values, store = attend_stages(range(hf * per_half, (hf + 1) * per_half))
        scores()
        if hf + 1 < n_piece:
            project_in(hf + 1)
        if hf > 0:
            project_out(hf - 1)
        softmax()
        values()
        store()
        pool(hf)
    project_out(n_piece - 1)

    @pl.when(s == pl.num_programs(1) - 1)
    def _():
        klast_ref[0] = z_ref[tm - WINDOW:tm, K_OFF:K_OFF + KV_WIDTH]
        vlast_ref[0] = z_ref[tm - WINDOW:tm, V_OFF:V_OFF + KV_WIDTH]
        plast_ref[0] = uext_ref[POOL_HEAD + tm - POOL_PAD:POOL_HEAD + tm, :]

    kext_ref[0:WINDOW, :] = kext_ref[tm:tm + WINDOW, :]
    vt_ref[:, 0:WINDOW] = vt_ref[:, tm:tm + WINDOW]
    uext_ref[POOL_HEAD - POOL_PAD:POOL_HEAD, :] = uext_ref[POOL_HEAD + tm - POOL_PAD:POOL_HEAD + tm, :]


def _row_chunk_specs(shape, n_steps, layer, step_of):
    rows, cols = shape
    chunk = rows // n_steps
    assert chunk * n_steps == rows and chunk % 16 == 0, (shape, n_steps)
    return (pl.BlockSpec((None, chunk, cols), lambda *g: (layer, step_of(*g), 0)),
            pl.BlockSpec((chunk, cols), lambda *g: (step_of(*g), 0)))


def _mix_prompt(x, sinks, gpre, gpost, win, wout, wpool, pscale, wg32, wu32, wd32, wpg32, *, layer, tm, piece):
    b, s, _ = x.shape
    kern = functools.partial(_mix_prompt_kernel, tm=tm, piece=piece, layer=layer)
    n_tiles = s // tm
    cast_shapes = ((D_MODEL, D_FF), (D_MODEL, D_FF), (D_FF, D_MODEL), (D_MODEL, D_MODEL))
    cast_specs = [_row_chunk_specs(shape, b * n_tiles, layer, lambda i, j: i * n_tiles + j) for shape in cast_shapes]
    return pl.pallas_call(
        kern,
        grid=(b, n_tiles),
        in_specs=[
            pl.BlockSpec(memory_space=pltpu.SMEM),
            pl.BlockSpec((1, tm, D_MODEL), lambda i, j: (i, j, 0)),
            _const_spec((1, D_MODEL), layer),
            _const_spec((1, D_MODEL), layer),
            _const_spec((D_MODEL, IN_WIDTH), layer),
            _const_spec((D_MODEL, D_MODEL), layer),
            _const_spec((len(POOL_WINDOWS), POOL_GROUP_WIDTH, POOL_GROUP_WIDTH), layer),
            _const_spec((1, POOL_WIDTH), layer),
        ] + [spec_in for spec_in, _ in cast_specs],
        out_specs=[
            pl.BlockSpec((1, tm, D_MODEL), lambda i, j: (i, j, 0)),
            pl.BlockSpec((1, WINDOW, KV_WIDTH), lambda i, j: (i, 0, 0)),
            pl.BlockSpec((1, WINDOW, KV_WIDTH), lambda i, j: (i, 0, 0)),
            pl.BlockSpec((1, POOL_PAD, POOL_WIDTH), lambda i, j: (i, 0, 0)),
        ] + [spec_out for _, spec_out in cast_specs],
        out_shape=[
            jax.ShapeDtypeStruct((b, s, D_MODEL), F32),
            jax.ShapeDtypeStruct((b, WINDOW, KV_WIDTH), F32),
            jax.ShapeDtypeStruct((b, WINDOW, KV_WIDTH), F32),
            jax.ShapeDtypeStruct((b, POOL_PAD, POOL_WIDTH), F32),
        ] + [jax.ShapeDtypeStruct(shape, BF16) for shape in cast_shapes],
        scratch_shapes=[
            pltpu.VMEM((tm, IN_WIDTH), F32),
            pltpu.VMEM((WINDOW + tm, KV_WIDTH), F32),
            pltpu.VMEM((KV_WIDTH, WINDOW + tm), F32),
            pltpu.VMEM((POOL_HEAD + tm, POOL_WIDTH), F32),
            pltpu.VMEM((POOL_HEAD + tm, POOL_GROUP_WIDTH), F32),
            pltpu.VMEM((POOL_HEAD + tm, POOL_GROUP_WIDTH), F32),
            pltpu.VMEM((tm, D_MODEL), BF16),
            pltpu.VMEM((2, 2 * WINDOW, WINDOW), F32),
        ],
        compiler_params=pltpu.CompilerParams(
            dimension_semantics=("arbitrary", "arbitrary"), vmem_limit_bytes=VMEM_LIMIT),
        name=f"mix_prompt_{layer}",
    )(sinks, x, gpre, gpost, win, wout, wpool, pscale, wg32, wu32, wd32, wpg32)


def _swiglu(f, wg_ref, wu_ref, wd_ref):
    d, pending = None, None
    for c0, c1 in FF_CHUNKS:
        g = _dot(f, wg_ref[:, c0:c1])
        u = _dot(f, wu_ref[:, c0:c1])
        if pending is not None:
            part = _dot(pending[0], wd_ref[pending[1]:pending[2], :])
            d = part if d is None else d + part
        pending = ((g * _sigmoid(g) * u).astype(BF16), c0, c1)
    return d + _dot(pending[0], wd_ref[pending[1]:pending[2], :])


def _ffn_rows(x, p, gpre_ref, gpost_ref, wg_ref, wu_ref, wd_ref, wple_ref, wpg_ref):
    f = _rms(x, gpre_ref[...]).astype(BF16)
    x = x + _rms(_swiglu(f, wg_ref, wu_ref, wd_ref), gpost_ref[...])
    gate = _sigmoid(_dot(x.astype(BF16), wpg_ref[...]))
    return x + gate * _dot(p.astype(BF16), wple_ref[...])


def _ffn_kernel(x_ref, p_ref, gpre_ref, gpost_ref, wg_ref, wu_ref, wd_ref, wple_ref, wpg_ref, o_ref, *, nsub):
    sub = x_ref.shape[0] // nsub
    for i in range(nsub):
        rows = slice(i * sub, (i + 1) * sub)
        o_ref[rows, :] = _ffn_rows(x_ref[rows, :], p_ref[rows, :], gpre_ref, gpost_ref, wg_ref, wu_ref, wd_ref,
                                   wple_ref, wpg_ref)


def _ffn(x, p, gpre, gpost, wg, wu, wd, wple, wpg, *, layer, tm, nsub):
    n = x.shape[0]
    return pl.pallas_call(
        functools.partial(_ffn_kernel, nsub=nsub),
        grid=(n // tm,),
        in_specs=[
            pl.BlockSpec((tm, D_MODEL), lambda i: (i, 0)),
            pl.BlockSpec((None, tm, PLE_DIM), lambda i: (layer, i, 0)),
            _const_spec((1, D_MODEL), layer),
            _const_spec((1, D_MODEL), layer),
            _const_spec((D_MODEL, D_FF)),
            _const_spec((D_MODEL, D_FF)),
            _const_spec((D_FF, D_MODEL)),
            _const_spec((PLE_DIM, D_MODEL), layer),
            _const_spec((D_MODEL, D_MODEL)),
        ],
        out_specs=pl.BlockSpec((tm, D_MODEL), lambda i: (i, 0)),
        out_shape=jax.ShapeDtypeStruct((n, D_MODEL), F32),
        compiler_params=pltpu.CompilerParams(
            dimension_semantics=("arbitrary",), vmem_limit_bytes=VMEM_LIMIT),
        name=f"ffn_prompt_{layer}",
    )(x, p, gpre, gpost, wg, wu, wd, wple, wpg)


def _layer_spec(shape, single=False):
    kwargs = dict(pipeline_mode=pl.Buffered(1)) if single else {}
    return pl.BlockSpec((None,) + shape, lambda l, c: (l,) + (0,) * len(shape), **kwargs)


def _sample_kernel(sinks_ref, xs_ref, ps_ref, ck_ref, cv_ref, st_ref, gmpre_ref, gmpost_ref, gfpre_ref, gfpost_ref,
                   win_ref, wout_ref, wpool_ref, pscale_ref, wple_ref, *refs, bb, dec, nsub, depth):
    chunk_refs = refs[:4 * depth]
    (y_ref, ko_ref, vo_ref, po_ref, z_ref, mix_ref, ustage_ref, dlt_ref,
     wg_ref, wu_ref, wd_ref, wpg_ref) = refs[4 * depth:]
    l = pl.program_id(0)
    c = pl.program_id(1)
    n_rows = y_ref.shape[0]

    @pl.when((l == 0) & (c == 0))
    def _():
        y_ref[...] = xs_ref[...]

    @pl.when(c == 0)
    def _():
        h = _rms(y_ref[...], gmpre_ref[...]).astype(BF16)
        z_ref[...] = _dot(h, win_ref[...])

    for k in range(depth):
        @pl.when(l == k)
        def _(k=k):
            for src_ref, dst_ref in zip(chunk_refs[4 * k:4 * k + 4], (wg_ref, wu_ref, wd_ref, wpg_ref)):
                rows_w = src_ref.shape[0]
                dst_ref[pl.ds(pl.multiple_of(c * rows_w, 16), rows_w), :] = src_ref[...]

    per_tile = 8 // dec
    base = pl.multiple_of(c * (bb * dec), 8)
    lane = lax.broadcasted_iota(jnp.int32, (8, LANES), 1)
    lo = lane < HEAD_DIM
    hi = jnp.logical_not(lo)
    rows64 = N_HEADS * 8
    ext_w = 2 * WINDOW
    new0 = ext_w - 8
    trow = lax.broadcasted_iota(jnp.int32, (rows64, ext_w), 0) & 7
    tstep = trow % dec
    tbat = trow // dec
    klane = lax.broadcasted_iota(jnp.int32, (rows64, ext_w), 1)
    fresh = klane - new0
    valid = ((klane < WINDOW) & (klane >= tstep + 1)) | (
        (klane >= new0) & (fresh // dec == tbat) & (fresh % dec <= tstep))
    tbat_o = tbat[:, 0:LANES]
    wlane = lax.broadcasted_iota(jnp.int32, (KV_WIDTH, WINDOW), 1)
    zpad = jnp.zeros((WINDOW - 8, KV_WIDTH), F32)

    tiles = range(bb // per_tile)
    st8 = {}

    for t in tiles:
        rows = pl.ds(base + 8 * t, 8)
        q = z_ref[rows, 0:ATTN_WIDTH] * (HEAD_DIM ** -0.5)
        knew = jnp.concatenate([zpad, z_ref[rows, K_OFF:K_OFF + KV_WIDTH]], axis=0)
        vnew = jnp.concatenate([zpad, z_ref[rows, V_OFF:V_OFF + KV_WIDTH]], axis=0)
        st8[t, "knew"], st8[t, "vnew"] = knew, vnew
        pieces, sinks = [], []
        for hd in range(N_HEADS):
            slab, half, kv = hd // 2, hd % 2, hd // GROUP
            qs = q[:, slab * LANES:(slab + 1) * LANES]
            src = qs if half == kv else pltpu.roll(qs, HEAD_DIM, 1)
            pieces.append(jnp.where(lo if kv == 0 else hi, src, 0.0))
            sinks.append(jnp.full((8, 1), sinks_ref[l, hd], F32))
        lhs_b = jnp.concatenate(pieces, axis=0).astype(BF16)
        st8[t, "sink"] = jnp.concatenate(sinks, axis=0)
        sc = None
        for i in range(per_tile):
            si = _dot(lhs_b, ck_ref[per_tile * t + i].astype(BF16))
            sc = si if sc is None else jnp.where(tbat_o == i, si, sc)
        sc = jnp.concatenate([sc, _dot_t(lhs_b, knew.astype(BF16))], axis=1)
        st8[t, "sc"] = jnp.where(valid, sc, -jnp.inf)

    for t in tiles:
        sc, sink = st8.pop((t, "sc")), st8.pop((t, "sink"))
        m = jnp.maximum(jnp.max(sc, axis=-1, keepdims=True), sink)
        p = jnp.exp(sc - m)
        st8[t, "denom"] = jnp.sum(p, axis=-1, keepdims=True) + jnp.exp(sink - m)
        st8[t, "pb"] = p.astype(BF16)

    for t in tiles:
        pb = st8.pop((t, "pb"))
        o = None
        for i in range(per_tile):
            oi = _dot_t(pb[:, 0:WINDOW], cv_ref[per_tile * t + i].astype(BF16))
            o = oi if o is None else jnp.where(tbat_o == i, oi, o)
        st8[t, "o"] = o + _dot(pb[:, WINDOW:ext_w], st8[t, "vnew"].astype(BF16))

    for t in tiles:
        rows = pl.ds(base + 8 * t, 8)
        o = st8.pop((t, "o")) / st8.pop((t, "denom"))
        for slab in range(N_HEADS // 2):
            kv = (2 * slab) // GROUP
            even = o[16 * slab:16 * slab + 8]
            odd = o[16 * slab + 8:16 * slab + 16]
            if kv == 0:
                res = jnp.where(lo, even, pltpu.roll(odd, HEAD_DIM, 1))
            else:
                res = jnp.where(lo, pltpu.roll(even, HEAD_DIM, 1), odd)
            mix_ref[rows, slab * LANES:(slab + 1) * LANES] = res

    for t in tiles:
        for new, c_ref, o_ref in ((st8.pop((t, "knew")), ck_ref, ko_ref), (st8.pop((t, "vnew")), cv_ref, vo_ref)):
            cols = new.T
            for bi in range(per_tile):
                b = per_tile * t + bi
                shift = dec * (per_tile - 1 - bi)
                mine = cols if shift == 0 else pltpu.roll(cols, shift, 1)
                kept = pltpu.roll(c_ref[b], WINDOW - dec, 1)
                o_ref[b] = jnp.where(wlane < WINDOW - dec, kept, mine)

    chunk = pl.ds(base, bb * dec)
    for r in range(POOL_STATE - dec):
        po_ref[r] = st_ref[r + dec]
    for gi, w in enumerate(POOL_WINDOWS):
        cols = slice(gi * POOL_GROUP_WIDTH, (gi + 1) * POOL_GROUP_WIDTH)
        ustage_ref[gi] = z_ref[chunk, U_OFF + gi * POOL_GROUP_WIDTH:U_OFF + (gi + 1) * POOL_GROUP_WIDTH]
        u = [ustage_ref[gi, pl.ds(j, bb, stride=dec), :] for j in range(dec)]
        suffix, acc = {}, None
        for m in range(1, w):
            row = st_ref[POOL_STATE - m, :, cols]
            acc = row if acc is None else acc + row
            suffix[m] = acc
        for j in range(dec):
            po_ref[POOL_STATE - dec + j, :, cols] = u[j]
            tot = suffix.get(w - 1 - j)
            for i in range(max(0, j - w + 1), j + 1):
                tot = u[i] if tot is None else tot + u[i]
            dlt_ref[gi, pl.ds(j, bb, stride=dec), :] = tot / float(w) - u[j]

    for gi in range(len(POOL_WINDOWS)):
        cols = slice(gi * POOL_GROUP_WIDTH, (gi + 1) * POOL_GROUP_WIDTH)
        yp = _dot(dlt_ref[gi].astype(BF16), wpool_ref[gi]) * pscale_ref[:, cols]
        mix_ref[chunk, ATTN_WIDTH + gi * POOL_GROUP_WIDTH:ATTN_WIDTH + (gi + 1) * POOL_GROUP_WIDTH] = yp

    @pl.when(c == pl.num_programs(1) - 1)
    def _():
        sub = n_rows // nsub
        for i in range(nsub):
            rs = slice(i * sub, (i + 1) * sub)
            mixed = _dot(mix_ref[rs, :].astype(BF16), wout_ref[...])
            x = y_ref[rs, :] + _rms(mixed, gmpost_ref[...])
            y_ref[rs, :] = _ffn_rows(x, ps_ref[rs, :], gfpre_ref, gfpost_ref, wg_ref, wu_ref, wd_ref, wple_ref,
                                     wpg_ref)


def _sample_path(xs, ps, sinks, ck, cv, st, gmpre, gmpost, gfpre, gfpost, win, wout, wpool, pscale,
                 wple, ff_weights, *, bb, dec, nsub):
    depth, nb = ck.shape[0], ck.shape[1]
    n = nb * dec
    kern = functools.partial(_sample_kernel, bb=bb, dec=dec, nsub=nsub, depth=depth)
    n_chunks = nb // bb

    def _chunk_spec(w, k):
        rows, cols = w.shape
        assert rows % n_chunks == 0 and (rows // n_chunks) % 16 == 0
        return pl.BlockSpec((rows // n_chunks, cols),
                            lambda l, c: (jnp.where(l == k, c, jnp.where(l > k, n_chunks - 1, 0)), 0))

    cache_spec = pl.BlockSpec((None, bb, KV_WIDTH, WINDOW), lambda l, c: (l, c, 0, 0))
    state_spec = pl.BlockSpec((None, POOL_STATE, bb, POOL_WIDTH), lambda l, c: (l, 0, c, 0))
    return pl.pallas_call(
        kern,
        grid=(depth, nb // bb),
        in_specs=[
            pl.BlockSpec(memory_space=pltpu.SMEM),
            pl.BlockSpec((n, D_MODEL), lambda l, c: (0, 0), pipeline_mode=pl.Buffered(1)),
            _layer_spec((n, PLE_DIM)),
            cache_spec, cache_spec, state_spec,
            _layer_spec((1, D_MODEL)), _layer_spec((1, D_MODEL)), _layer_spec((1, D_MODEL)), _layer_spec((1, D_MODEL)),
            _layer_spec((D_MODEL, IN_WIDTH), single=True),
            _layer_spec((D_MODEL, D_MODEL), single=True),
            _layer_spec((len(POOL_WINDOWS), POOL_GROUP_WIDTH, POOL_GROUP_WIDTH)),
            _layer_spec((1, POOL_WIDTH)),
            _layer_spec((PLE_DIM, D_MODEL), single=True),
        ] + [_chunk_spec(w, k) for k, ws in enumerate(ff_weights) for w in ws],
        out_specs=[
            pl.BlockSpec((n, D_MODEL), lambda l, c: (0, 0)),
            cache_spec, cache_spec, state_spec,
        ],
        out_shape=[
            jax.ShapeDtypeStruct((n, D_MODEL), F32),
            jax.ShapeDtypeStruct((depth, nb, KV_WIDTH, WINDOW), F32),
            jax.ShapeDtypeStruct((depth, nb, KV_WIDTH, WINDOW), F32),
            jax.ShapeDtypeStruct((depth, POOL_STATE, nb, POOL_WIDTH), F32),
        ],
        scratch_shapes=[
            pltpu.VMEM((n, IN_WIDTH), F32),
            pltpu.VMEM((n, D_MODEL), F32),
            pltpu.VMEM((len(POOL_WINDOWS), bb * dec, POOL_GROUP_WIDTH), F32),
            pltpu.VMEM((len(POOL_WINDOWS), bb * dec, POOL_GROUP_WIDTH), F32),
            pltpu.VMEM((D_MODEL, D_FF), BF16),
            pltpu.VMEM((D_MODEL, D_FF), BF16),
            pltpu.VMEM((D_FF, D_MODEL), BF16),
            pltpu.VMEM((D_MODEL, D_MODEL), BF16),
        ],
        compiler_params=pltpu.CompilerParams(
            dimension_semantics=("arbitrary", "arbitrary"), vmem_limit_bytes=VMEM_LIMIT),
        name="sample_path",
    )(sinks, xs, ps, ck, cv, st, gmpre, gmpost, gfpre, gfpost, win, wout, wpool, pscale, wple,
      *[w for ws in ff_weights for w in ws])


def _cache_from_device_layout(c):
    return c.reshape(c.shape[0], c.shape[1], N_KV_HEADS, HEAD_DIM, WINDOW).transpose(0, 1, 4, 2, 3)


def kernel(x_prompt, x_sample, p_prompt, p_sample, cache_k, cache_v, state_pool, norm_mix_pre, norm_mix_post,
           norm_ffn_pre, norm_ffn_post, w_in, w_out, attn_sinks, w_pool, pool_scale, w_gate, w_up, w_down, w_ple,
           w_ple_gate):
    depth, nbat, seq, _ = p_prompt.shape
    dec_b, dec = x_sample.shape[0], x_sample.shape[1]
    assert 8 % dec == 0 and seq % WINDOW == 0

    win, wout, wpool, wple = w_in.astype(BF16), w_out.astype(BF16), w_pool.astype(BF16), w_ple.astype(BF16)
    g_mix_pre = norm_mix_pre.reshape(depth, 1, D_MODEL)
    g_mix_post = norm_mix_post.reshape(depth, 1, D_MODEL)
    g_ffn_pre = norm_ffn_pre.reshape(depth, 1, D_MODEL)
    g_ffn_post = norm_ffn_post.reshape(depth, 1, D_MODEL)
    pscale = pool_scale.reshape(depth, 1, POOL_WIDTH)

    pp = p_prompt.reshape(depth, nbat * seq, PLE_DIM)
    ps = p_sample.reshape(depth, dec_b * dec, PLE_DIM)
    ck = cache_k.transpose(0, 1, 3, 4, 2).reshape(depth, dec_b, KV_WIDTH, WINDOW)
    cv = cache_v.transpose(0, 1, 3, 4, 2).reshape(depth, dec_b, KV_WIDTH, WINDOW)

    yp = x_prompt
    kp_l, vp_l, sp_l, ff_weights = [], [], [], []
    for i in range(depth):
        yp, kp, vp, sp, wg, wu, wd, wpg = _mix_prompt(yp, attn_sinks, g_mix_pre, g_mix_post, win, wout, wpool, pscale,
                                                      w_gate, w_up, w_down, w_ple_gate, layer=i, tm=1024, piece=512)
        yp = _ffn(yp.reshape(nbat * seq, D_MODEL), pp, g_ffn_pre, g_ffn_post, wg, wu, wd, wple, wpg,
                  layer=i, tm=1024, nsub=2).reshape(nbat, seq, D_MODEL)
        ff_weights.append((wg, wu, wd, wpg))
        kp_l.append(kp)
        vp_l.append(vp)
        sp_l.append(sp[:, POOL_PAD - POOL_STATE:])

    ys, ks, vs, ss = _sample_path(x_sample.reshape(dec_b * dec, D_MODEL), ps, attn_sinks, ck, cv,
                                  state_pool.transpose(0, 2, 1, 3),
                                  g_mix_pre, g_mix_post, g_ffn_pre, g_ffn_post, win, wout, wpool, pscale,
                                  wple, ff_weights, bb=8, dec=dec, nsub=2)

    kv_shape = (depth, -1, WINDOW, N_KV_HEADS, HEAD_DIM)
    return (yp, ys.reshape(dec_b, dec, D_MODEL),
            jnp.stack(kp_l).reshape(kv_shape), jnp.stack(vp_l).reshape(kv_shape), jnp.stack(sp_l),
            _cache_from_device_layout(ks), _cache_from_device_layout(vs), ss.transpose(0, 2, 1, 3))
```

```python
import functools

import jax
import jax.numpy as jnp
from jax import lax
from jax.experimental import pallas as pl
from jax.experimental.pallas import tpu as pltpu

D_MODEL = 1024
DEPTH = 4
ATTN_WIDTH = 512
HEAD_DIM = 64
N_HEADS = 8
N_KV_HEADS = 2
GROUP = 4
KV_WIDTH = 128
WINDOW = 128
POOL_WIDTH = 512
POOL_WINDOWS = (2, 4, 8, 16)
POOL_GROUP_WIDTH = 128
POOL_STATE = 15
IN_WIDTH = 1280
D_FF = 2816
PLE_DIM = 256
EPS = 1e-6

K_OFF = ATTN_WIDTH
V_OFF = ATTN_WIDTH + KV_WIDTH
U_OFF = ATTN_WIDTH + 2 * KV_WIDTH
LANES = 128
POOL_PAD = 16
POOL_HEAD = 8 + POOL_PAD
LOG2E = 1.4426950408889634
FF_CHUNKS = ((0, 1024), (1024, 2048), (2048, D_FF))

BF16 = jnp.bfloat16
F32 = jnp.float32
V7X_VMEM_BYTES = 64 * 1024 * 1024
VMEM_LIMIT = V7X_VMEM_BYTES - 8 * 1024 * 1024

PROMPT_TILE = 1024
PROMPT_PIECE = 512
FFN_TILE = 1024
FFN_SUBTILES = 2
SAMPLE_BATCHES = 8
SAMPLE_SUBTILES = 1


def _rms(x, g):
    return x * lax.rsqrt(jnp.mean(x * x, axis=-1, keepdims=True) + EPS) * g


def _sigmoid(x):
    return 1.0 / (1.0 + jnp.exp(-x))


def _dot(a, b):
    return jnp.dot(a, b, preferred_element_type=F32)


def _dot_t(a, b):
    return lax.dot_general(a, b, (((1,), (1,)), ((), ())), preferred_element_type=F32)


def _const_spec(shape, layer=None):
    if layer is None:
        return pl.BlockSpec(shape, lambda *_: (0,) * len(shape), pipeline_mode=pl.Buffered(1))
    return pl.BlockSpec((None,) + shape, lambda *_: (layer,) + (0,) * len(shape),
                        pipeline_mode=pl.Buffered(1))


def _mix_prompt_kernel(sinks_ref, x_ref, gpre_ref, gpost_ref, win_ref, wout_ref, wpool_ref, pscale_ref,
                       wg32_ref, wu32_ref, wd32_ref, wpg32_ref,
                       y_ref, klast_ref, vlast_ref, plast_ref, wg16_ref, wu16_ref, wd16_ref, wpg16_ref,
                       z_ref, kext_ref, vt_ref, uext_ref, ps1_ref, ps2_ref, mix_ref, bias_ref, *, tm, piece, layer):
    s = pl.program_id(1)
    nblk = tm // WINDOW

    for w32_ref, w16_ref in ((wg32_ref, wg16_ref), (wu32_ref, wu16_ref), (wd32_ref, wd16_ref),
                             (wpg32_ref, wpg16_ref)):
        w16_ref[...] = w32_ref[...].astype(BF16)

    @pl.when((pl.program_id(0) == 0) & (s == 0))
    def _():
        c = lax.broadcasted_iota(jnp.int32, (2 * WINDOW, WINDOW), 0)
        r = lax.broadcasted_iota(jnp.int32, (2 * WINDOW, WINDOW), 1)
        dist = r + WINDOW - c
        ok = (dist >= 0) & (dist < WINDOW)
        bias_ref[0] = jnp.where(ok, 0.0, -jnp.inf)
        bias_ref[1] = jnp.where(ok & (c >= WINDOW), 0.0, -jnp.inf)

    @pl.when(s == 0)
    def _():
        kext_ref[0:WINDOW, :] = jnp.zeros((WINDOW, KV_WIDTH), F32)
        vt_ref[:, 0:WINDOW] = jnp.zeros((KV_WIDTH, WINDOW), F32)
        uext_ref[0:POOL_HEAD, :] = jnp.zeros((POOL_HEAD, POOL_WIDTH), F32)
        ps1_ref[0:8, :] = jnp.zeros((8, POOL_GROUP_WIDTH), F32)
        ps2_ref[0:8, :] = jnp.zeros((8, POOL_GROUP_WIDTH), F32)

    lane = lax.broadcasted_iota(jnp.int32, (WINDOW, LANES), 1)
    lo = lane < HEAD_DIM
    lo2 = jnp.concatenate([lo, lo], axis=0)
    row8 = lax.broadcasted_iota(jnp.int32, (8, GROUP * WINDOW), 0)
    kcol = lax.broadcasted_iota(jnp.int32, (HEAD_DIM, 2 * WINDOW), 1)
    ones = jnp.ones((HEAD_DIM, 2 * WINDOW), F32)
    half = piece

    def project_in(hf):
        hr = slice(hf * half, (hf + 1) * half)
        h = _rms(x_ref[0, hr, :], gpre_ref[...]).astype(BF16)
        z_ref[hr, :] = _dot(h, win_ref[...])
        uext_ref[POOL_HEAD + hf * half:POOL_HEAD + (hf + 1) * half, :] = z_ref[hr, U_OFF:U_OFF + POOL_WIDTH]
        kext_ref[WINDOW + hf * half:WINDOW + (hf + 1) * half, :] = z_ref[hr, K_OFF:K_OFF + KV_WIDTH]
        for n in range(hf * half // WINDOW, (hf + 1) * half // WINDOW):
            r0 = n * WINDOW
            vt_ref[:, WINDOW + r0:2 * WINDOW + r0] = z_ref[r0:r0 + WINDOW, V_OFF:V_OFF + KV_WIDTH].T

    def attend_stages(blocks):
        st8 = {}

        def scores():
            for n in blocks:
                r0 = n * WINDOW
                rows = slice(r0, r0 + WINDOW)
                kcat = kext_ref[r0:r0 + 2 * WINDOW, :]
                kswap = pltpu.roll(kcat, HEAD_DIM, 1)
                bias1 = bias_ref[jnp.where(s == 0, 1, 0)] if n == 0 else bias_ref[0]
                bias = jnp.concatenate([bias1] * GROUP, axis=1)
                for kv in range(N_KV_HEADS):
                    kk = (jnp.where(lo2, kcat, kswap) if kv == 0 else jnp.where(lo2, kswap, kcat)).astype(BF16)
                    pieces, sinks = [], []
                    for g in range(GROUP):
                        hd = kv * GROUP + g
                        slab = hd // 2
                        qs = z_ref[rows, slab * LANES:(slab + 1) * LANES] * (HEAD_DIM ** -0.5 * LOG2E)
                        keep = lo if hd % 2 == 0 else jnp.logical_not(lo)
                        pieces.append(jnp.where(keep, qs, 0.0).astype(BF16))
                        sinks.append(jnp.full((1, WINDOW), sinks_ref[layer, hd] * LOG2E, F32))
                    q4 = jnp.concatenate(pieces, axis=0)
                    st8[n, kv, "sink"] = jnp.concatenate(sinks, axis=1)
                    st8[n, kv, "st"] = _dot_t(kk, q4) + bias

        def softmax():
            for n in blocks:
                for kv in range(N_KV_HEADS):
                    st, sink = st8.pop((n, kv, "st")), st8.pop((n, kv, "sink"))
                    m = jnp.maximum(jnp.max(st, axis=0, keepdims=True), sink)
                    p = jnp.exp2(st - m)
                    top = jnp.where(row8 == 0, jnp.exp2(sink - m), p[0:8])
                    st8[n, kv, "p"] = jnp.concatenate([top, p[8:]], axis=0).astype(BF16)

        def values():
            for n in blocks:
                r0 = n * WINDOW
                for kv in range(N_KV_HEADS):
                    vth = vt_ref[kv * HEAD_DIM:(kv + 1) * HEAD_DIM, r0:r0 + 2 * WINDOW]
                    a = jnp.concatenate([jnp.where(kcol == 0, 0.0, vth), ones], axis=0).astype(BF16)
                    st8[n, kv, "ot"] = _dot(a, st8.pop((n, kv, "p")))

        def store():
            for n in blocks:
                rows = slice(n * WINDOW, (n + 1) * WINDOW)
                for kv in range(N_KV_HEADS):
                    ot = st8.pop((n, kv, "ot"))
                    inv = 1.0 / ot[HEAD_DIM:HEAD_DIM + 8]
                    on = ot[0:HEAD_DIM] * jnp.concatenate([inv] * (HEAD_DIM // 8), axis=0)
                    for j in range(GROUP // 2):
                        c0 = 2 * j * WINDOW
                        pair = jnp.concatenate([on[:, c0:c0 + WINDOW], on[:, c0 + WINDOW:c0 + 2 * WINDOW]], axis=0)
                        slab = kv * (GROUP // 2) + j
                        mix_ref[rows, slab * LANES:(slab + 1) * LANES] = pair.T.astype(BF16)

        return scores, softmax, values, store

    def pool(hf):
        t0 = hf * half
        pos = lax.broadcasted_iota(jnp.int32, (half, 1), 0) + (s * tm + t0)
        first = POOL_HEAD + t0
        ext = slice(first - POOL_PAD, first + half)
        for gi, w in enumerate(POOL_WINDOWS):
            cols = slice(gi * POOL_GROUP_WIDTH, (gi + 1) * POOL_GROUP_WIDTH)
            src, src_cols, span, bufs = uext_ref, cols, 1, [ps1_ref, ps2_ref]
            while 2 * span < w:
                dst = bufs[0]
                dst[ext, :] = src[ext, src_cols] + src[first - POOL_PAD - span:first + half - span, src_cols]
                src, src_cols, span, bufs = dst, slice(None), 2 * span, bufs[::-1]
            tot = src[first:first + half, src_cols] + src[first - span:first - span + half, src_cols]
            cnt = jnp.minimum(w, pos + 1).astype(F32)
            dlt = (tot / cnt - uext_ref[first:first + half, cols]).astype(BF16)
            yp = _dot(dlt, wpool_ref[gi]) * pscale_ref[:, cols]
            mix_ref[t0:t0 + half, ATTN_WIDTH + gi * POOL_GROUP_WIDTH:ATTN_WIDTH + (gi + 1) * POOL_GROUP_WIDTH] = (
                yp.astype(BF16))

    def project_out(hf):
        hr = slice(hf * half, (hf + 1) * half)
        mixed = _dot(mix_ref[hr, :], wout_ref[...])
        y_ref[0, hr, :] = x_ref[0, hr, :] + _rms(mixed, gpost_ref[...])

    per_half = half // WINDOW
    n_piece = tm // half
    project_in(0)
    for hf in range(n_piece):
        scores, softmax, values, store = attend_stages(range(hf * per_half, (hf + 1) * per_half))
        scores()
        if hf + 1 < n_piece:
            project_in(hf + 1)
        if hf > 0:
            project_out(hf - 1)
        softmax()
        values()
        store()
        pool(hf)
    project_out(n_piece - 1)

    @pl.when(s == pl.num_programs(1) - 1)
    def _():
        klast_ref[0] = z_ref[tm - WINDOW:tm, K_OFF:K_OFF + KV_WIDTH]
        vlast_ref[0] = z_ref[tm - WINDOW:tm, V_OFF:V_OFF + KV_WIDTH]
        plast_ref[0] = uext_ref[POOL_HEAD + tm - POOL_PAD:POOL_HEAD + tm, :]

    kext_ref[0:WINDOW, :] = kext_ref[tm:tm + WINDOW, :]
    vt_ref[:, 0:WINDOW] = vt_ref[:, tm:tm + WINDOW]
    uext_ref[POOL_HEAD - POOL_PAD:POOL_HEAD, :] = uext_ref[POOL_HEAD + tm - POOL_PAD:POOL_HEAD + tm, :]


def _row_chunk_specs(shape, n_steps, layer, step_of):
    rows, cols = shape
    chunk = rows // n_steps
    assert chunk * n_steps == rows and chunk % 16 == 0, (shape, n_steps)
    return (pl.BlockSpec((None, chunk, cols), lambda *g: (layer, step_of(*g), 0)),
            pl.BlockSpec((chunk, cols), lambda *g: (step_of(*g), 0)))


def _mix_prompt(x, sinks, gpre, gpost, win, wout, wpool, pscale, wg32, wu32, wd32, wpg32, *, layer, tm, piece):
    b, s, _ = x.shape
    kern = functools.partial(_mix_prompt_kernel, tm=tm, piece=piece, layer=layer)
    n_tiles = s // tm
    cast_shapes = ((D_MODEL, D_FF), (D_MODEL, D_FF), (D_FF, D_MODEL), (D_MODEL, D_MODEL))
    cast_specs = [_row_chunk_specs(shape, b * n_tiles, layer, lambda i, j: i * n_tiles + j) for shape in cast_shapes]
    return pl.pallas_call(
        kern,
        grid=(b, n_tiles),
        in_specs=[
            pl.BlockSpec(memory_space=pltpu.SMEM),
            pl.BlockSpec((1, tm, D_MODEL), lambda i, j: (i, j, 0)),
            _const_spec((1, D_MODEL), layer),
            _const_spec((1, D_MODEL), layer),
            _const_spec((D_MODEL, IN_WIDTH), layer),
            _const_spec((D_MODEL, D_MODEL), layer),
            _const_spec((len(POOL_WINDOWS), POOL_GROUP_WIDTH, POOL_GROUP_WIDTH), layer),
            _const_spec((1, POOL_WIDTH), layer),
        ] + [spec_in for spec_in, _ in cast_specs],
        out_specs=[
            pl.BlockSpec((1, tm, D_MODEL), lambda i, j: (i, j, 0)),
            pl.BlockSpec((1, WINDOW, KV_WIDTH), lambda i, j: (i, 0, 0)),
            pl.BlockSpec((1, WINDOW, KV_WIDTH), lambda i, j: (i, 0, 0)),
            pl.BlockSpec((1, POOL_PAD, POOL_WIDTH), lambda i, j: (i, 0, 0)),
        ] + [spec_out for _, spec_out in cast_specs],
        out_shape=[
            jax.ShapeDtypeStruct((b, s, D_MODEL), F32),
            jax.ShapeDtypeStruct((b, WINDOW, KV_WIDTH), F32),
            jax.ShapeDtypeStruct((b, WINDOW, KV_WIDTH), F32),
            jax.ShapeDtypeStruct((b, POOL_PAD, POOL_WIDTH), F32),
        ] + [jax.ShapeDtypeStruct(shape, BF16) for shape in cast_shapes],
        scratch_shapes=[
            pltpu.VMEM((tm, IN_WIDTH), F32),
            pltpu.VMEM((WINDOW + tm, KV_WIDTH), F32),
            pltpu.VMEM((KV_WIDTH, WINDOW + tm), F32),
            pltpu.VMEM((POOL_HEAD + tm, POOL_WIDTH), F32),
            pltpu.VMEM((POOL_HEAD + tm, POOL_GROUP_WIDTH), F32),
            pltpu.VMEM((POOL_HEAD + tm, POOL_GROUP_WIDTH), F32),
            pltpu.VMEM((tm, D_MODEL), BF16),
            pltpu.VMEM((2, 2 * WINDOW, WINDOW), F32),
        ],
        compiler_params=pltpu.CompilerParams(
            dimension_semantics=("arbitrary", "arbitrary"), vmem_limit_bytes=VMEM_LIMIT),
        name=f"mix_prompt_{layer}",
    )(sinks, x, gpre, gpost, win, wout, wpool, pscale, wg32, wu32, wd32, wpg32)


def _swiglu(f, wg_ref, wu_ref, wd_ref):
    d, pending = None, None
    for c0, c1 in FF_CHUNKS:
        g = _dot(f, wg_ref[:, c0:c1])
        u = _dot(f, wu_ref[:, c0:c1])
        if pending is not None:
            part = _dot(pending[0], wd_ref[pending[1]:pending[2], :])
            d = part if d is None else d + part
        pending = ((g * _sigmoid(g) * u).astype(BF16), c0, c1)
    return d + _dot(pending[0], wd_ref[pending[1]:pending[2], :])


def _ffn_rows(x, p, gpre_ref, gpost_ref, wg_ref, wu_ref, wd_ref, wple_ref, wpg_ref):
    f = _rms(x, gpre_ref[...]).astype(BF16)
    x = x + _rms(_swiglu(f, wg_ref, wu_ref, wd_ref), gpost_ref[...])
    gate = _sigmoid(_dot(x.astype(BF16), wpg_ref[...]))
    return x + gate * _dot(p.astype(BF16), wple_ref[...])


def _ffn_kernel(x_ref, p_ref, gpre_ref, gpost_ref, wg_ref, wu_ref, wd_ref, wple_ref, wpg_ref, o_ref, *, nsub):
    sub = x_ref.shape[0] // nsub
    for i in range(nsub):
        rows = slice(i * sub, (i + 1) * sub)
        o_ref[rows, :] = _ffn_rows(x_ref[rows, :], p_ref[rows, :], gpre_ref, gpost_ref, wg_ref, wu_ref, wd_ref,
                                   wple_ref, wpg_ref)


def _ffn(x, p, gpre, gpost, wg, wu, wd, wple, wpg, *, layer, tm, nsub):
    n = x.shape[0]
    return pl.pallas_call(
        functools.partial(_ffn_kernel, nsub=nsub),
        grid=(n // tm,),
        in_specs=[
            pl.BlockSpec((tm, D_MODEL), lambda i: (i, 0)),
            pl.BlockSpec((None, tm, PLE_DIM), lambda i: (layer, i, 0)),
            _const_spec((1, D_MODEL), layer),
            _const_spec((1, D_MODEL), layer),
            _const_spec((D_MODEL, D_FF)),
            _const_spec((D_MODEL, D_FF)),
            _const_spec((D_FF, D_MODEL)),
            _const_spec((PLE_DIM, D_MODEL), layer),
            _const_spec((D_MODEL, D_MODEL)),
        ],
        out_specs=pl.BlockSpec((tm, D_MODEL), lambda i: (i, 0)),
        out_shape=jax.ShapeDtypeStruct((n, D_MODEL), F32),
        compiler_params=pltpu.CompilerParams(
            dimension_semantics=("arbitrary",), vmem_limit_bytes=VMEM_LIMIT),
        name=f"ffn_prompt_{layer}",
    )(x, p, gpre, gpost, wg, wu, wd, wple, wpg)


def _layer_spec(shape, single=False):
    kwargs = dict(pipeline_mode=pl.Buffered(1)) if single else {}
    return pl.BlockSpec((None,) + shape, lambda l, c: (l,) + (0,) * len(shape), **kwargs)


def _sample_kernel(sinks_ref, xs_ref, ps_ref, ck_ref, cv_ref, st_ref, gmpre_ref, gmpost_ref, gfpre_ref, gfpost_ref,
                   win_ref, wout_ref, wpool_ref, pscale_ref, wple_ref, *refs, bb, dec, nsub, depth):
    chunk_refs = refs[:4 * depth]
    (y_ref, ko_ref, vo_ref, po_ref, z_ref, mix_ref, ustage_ref, dlt_ref,
     wg_ref, wu_ref, wd_ref, wpg_ref) = refs[4 * depth:]
    l = pl.program_id(0)
    c = pl.program_id(1)
    n_rows = y_ref.shape[0]

    @pl.when((l == 0) & (c == 0))
    def _():
        y_ref[...] = xs_ref[...]

    @pl.when(c == 0)
    def _():
        h = _rms(y_ref[...], gmpre_ref[...]).astype(BF16)
        z_ref[...] = _dot(h, win_ref[...])

    for k in range(depth):
        @pl.when(l == k)
        def _(k=k):
            for src_ref, dst_ref in zip(chunk_refs[4 * k:4 * k + 4], (wg_ref, wu_ref, wd_ref, wpg_ref)):
                rows_w = src_ref.shape[0]
                dst_ref[pl.ds(pl.multiple_of(c * rows_w, 16), rows_w), :] = src_ref[...]

    per_tile = 8 // dec
    base = pl.multiple_of(c * (bb * dec), 8)
    lane = lax.broadcasted_iota(jnp.int32, (8, LANES), 1)
    lo = lane < HEAD_DIM
    hi = jnp.logical_not(lo)
    rows64 = N_HEADS * 8
    ext_w = 2 * WINDOW
    new0 = ext_w - 8
    trow = lax.broadcasted_iota(jnp.int32, (rows64, ext_w), 0) & 7
    tstep = trow % dec
    tbat = trow // dec
    klane = lax.broadcasted_iota(jnp.int32, (rows64, ext_w), 1)
    fresh = klane - new0
    valid = ((klane < WINDOW) & (klane >= tstep + 1)) | (
        (klane >= new0) & (fresh // dec == tbat) & (fresh % dec <= tstep))
    tbat_o = tbat[:, 0:LANES]
    wlane = lax.broadcasted_iota(jnp.int32, (KV_WIDTH, WINDOW), 1)
    zpad = jnp.zeros((WINDOW - 8, KV_WIDTH), F32)

    tiles = range(bb // per_tile)
    st8 = {}

    for t in tiles:
        rows = pl.ds(base + 8 * t, 8)
        q = z_ref[rows, 0:ATTN_WIDTH] * (HEAD_DIM ** -0.5)
        knew = jnp.concatenate([zpad, z_ref[rows, K_OFF:K_OFF + KV_WIDTH]], axis=0)
        vnew = jnp.concatenate([zpad, z_ref[rows, V_OFF:V_OFF + KV_WIDTH]], axis=0)
        st8[t, "knew"], st8[t, "vnew"] = knew, vnew
        pieces, sinks = [], []
        for hd in range(N_HEADS):
            slab, half, kv = hd // 2, hd % 2, hd // GROUP
            qs = q[:, slab * LANES:(slab + 1) * LANES]
            src = qs if half == kv else pltpu.roll(qs, HEAD_DIM, 1)
            pieces.append(jnp.where(lo if kv == 0 else hi, src, 0.0))
            sinks.append(jnp.full((8, 1), sinks_ref[l, hd], F32))
        lhs_b = jnp.concatenate(pieces, axis=0).astype(BF16)
        st8[t, "sink"] = jnp.concatenate(sinks, axis=0)
        sc = None
        for i in range(per_tile):
            si = _dot(lhs_b, ck_ref[per_tile * t + i].astype(BF16))
            sc = si if sc is None else jnp.where(tbat_o == i, si, sc)
        sc = jnp.concatenate([sc, _dot_t(lhs_b, knew.astype(BF16))], axis=1)
        st8[t, "sc"] = jnp.where(valid, sc, -jnp.inf)

    for t in tiles:
        sc, sink = st8.pop((t, "sc")), st8.pop((t, "sink"))
        m = jnp.maximum(jnp.max(sc, axis=-1, keepdims=True), sink)
        p = jnp.exp(sc - m)
        st8[t, "denom"] = jnp.sum(p, axis=-1, keepdims=True) + jnp.exp(sink - m)
        st8[t, "pb"] = p.astype(BF16)

    for t in tiles:
        pb = st8.pop((t, "pb"))
        o = None
        for i in range(per_tile):
            oi = _dot_t(pb[:, 0:WINDOW], cv_ref[per_tile * t + i].astype(BF16))
            o = oi if o is None else jnp.where(tbat_o == i, oi, o)
        st8[t, "o"] = o + _dot(pb[:, WINDOW:ext_w], st8[t, "vnew"].astype(BF16))

    for t in tiles:
        rows = pl.ds(base + 8 * t, 8)
        o = st8.pop((t, "o")) / st8.pop((t, "denom"))
        for slab in range(N_HEADS // 2):
            kv = (2 * slab) // GROUP
            even = o[16 * slab:16 * slab + 8]
            odd = o[16 * slab + 8:16 * slab + 16]
            if kv == 0:
                res = jnp.where(lo, even, pltpu.roll(odd, HEAD_DIM, 1))
            else:
                res = jnp.where(lo, pltpu.roll(even, HEAD_DIM, 1), odd)
            mix_ref[rows, slab * LANES:(slab + 1) * LANES] = res

    for t in tiles:
        for new, c_ref, o_ref in ((st8.pop((t, "knew")), ck_ref, ko_ref), (st8.pop((t, "vnew")), cv_ref, vo_ref)):
            cols = new.T
            for bi in range(per_tile):
                b = per_tile * t + bi
                shift = dec * (per_tile - 1 - bi)
                mine = cols if shift == 0 else pltpu.roll(cols, shift, 1)
                kept = pltpu.roll(c_ref[b], WINDOW - dec, 1)
                o_ref[b] = jnp.where(wlane < WINDOW - dec, kept, mine)

    chunk = pl.ds(base, bb * dec)
    for r in range(POOL_STATE - dec):
        po_ref[r] = st_ref[r + dec]
    for gi, w in enumerate(POOL_WINDOWS):
        cols = slice(gi * POOL_GROUP_WIDTH, (gi + 1) * POOL_GROUP_WIDTH)
        ustage_ref[gi] = z_ref[chunk, U_OFF + gi * POOL_GROUP_WIDTH:U_OFF + (gi + 1) * POOL_GROUP_WIDTH]
        u = [ustage_ref[gi, pl.ds(j, bb, stride=dec), :] for j in range(dec)]
        suffix, acc = {}, None
        for m in range(1, w):
            row = st_ref[POOL_STATE - m, :, cols]
            acc = row if acc is None else acc + row
            suffix[m] = acc
        for j in range(dec):
            po_ref[POOL_STATE - dec + j, :, cols] = u[j]
            tot = suffix.get(w - 1 - j)
            for i in range(max(0, j - w + 1), j + 1):
                tot = u[i] if tot is None else tot + u[i]
            dlt_ref[gi, pl.ds(j, bb, stride=dec), :] = tot / float(w) - u[j]

    for gi in range(len(POOL_WINDOWS)):
        cols = slice(gi * POOL_GROUP_WIDTH, (gi + 1) * POOL_GROUP_WIDTH)
        yp = _dot(dlt_ref[gi].astype(BF16), wpool_ref[gi]) * pscale_ref[:, cols]
        mix_ref[chunk, ATTN_WIDTH + gi * POOL_GROUP_WIDTH:ATTN_WIDTH + (gi + 1) * POOL_GROUP_WIDTH] = yp

    @pl.when(c == pl.num_programs(1) - 1)
    def _():
        sub = n_rows // nsub
        for i in range(nsub):
            rs = slice(i * sub, (i + 1) * sub)
            mixed = _dot(mix_ref[rs, :].astype(BF16), wout_ref[...])
            x = y_ref[rs, :] + _rms(mixed, gmpost_ref[...])
            y_ref[rs, :] = _ffn_rows(x, ps_ref[rs, :], gfpre_ref, gfpost_ref, wg_ref, wu_ref, wd_ref, wple_ref,
                                     wpg_ref)


def _sample_path(xs, ps, sinks, ck, cv, st, gmpre, gmpost, gfpre, gfpost, win, wout, wpool, pscale,
                 wple, ff_weights, *, bb, dec, nsub):
    depth, nb = ck.shape[0], ck.shape[1]
    n = nb * dec
    kern = functools.partial(_sample_kernel, bb=bb, dec=dec, nsub=nsub, depth=depth)
    n_chunks = nb // bb

    def _chunk_spec(w, k):
        rows, cols = w.shape
        assert rows % n_chunks == 0 and (rows // n_chunks) % 16 == 0
        return pl.BlockSpec((rows // n_chunks, cols),
                            lambda l, c: (jnp.where(l == k, c, jnp.where(l > k, n_chunks - 1, 0)), 0))

    cache_spec = pl.BlockSpec((None, bb, KV_WIDTH, WINDOW), lambda l, c: (l, c, 0, 0))
    state_spec = pl.BlockSpec((None, POOL_STATE, bb, POOL_WIDTH), lambda l, c: (l, 0, c, 0))
    return pl.pallas_call(
        kern,
        grid=(depth, nb // bb),
        in_specs=[
            pl.BlockSpec(memory_space=pltpu.SMEM),
            pl.BlockSpec((n, D_MODEL), lambda l, c: (0, 0), pipeline_mode=pl.Buffered(1)),
            _layer_spec((n, PLE_DIM)),
            cache_spec, cache_spec, state_spec,
            _layer_spec((1, D_MODEL)), _layer_spec((1, D_MODEL)), _layer_spec((1, D_MODEL)), _layer_spec((1, D_MODEL)),
            _layer_spec((D_MODEL, IN_WIDTH), single=True),
            _layer_spec((D_MODEL, D_MODEL), single=True),
            _layer_spec((len(POOL_WINDOWS), POOL_GROUP_WIDTH, POOL_GROUP_WIDTH)),
            _layer_spec((1, POOL_WIDTH)),
            _layer_spec((PLE_DIM, D_MODEL), single=True),
        ] + [_chunk_spec(w, k) for k, ws in enumerate(ff_weights) for w in ws],
        out_specs=[
            pl.BlockSpec((n, D_MODEL), lambda l, c: (0, 0)),
            cache_spec, cache_spec, state_spec,
        ],
        out_shape=[
            jax.ShapeDtypeStruct((n, D_MODEL), F32),
            jax.ShapeDtypeStruct((depth, nb, KV_WIDTH, WINDOW), F32),
            jax.ShapeDtypeStruct((depth, nb, KV_WIDTH, WINDOW), F32),
            jax.ShapeDtypeStruct((depth, POOL_STATE, nb, POOL_WIDTH), F32),
        ],
        scratch_shapes=[
            pltpu.VMEM((n, IN_WIDTH), F32),
            pltpu.VMEM((n, D_MODEL), F32),
            pltpu.VMEM((len(POOL_WINDOWS), bb * dec, POOL_GROUP_WIDTH), F32),
            pltpu.VMEM((len(POOL_WINDOWS), bb * dec, POOL_GROUP_WIDTH), F32),
            pltpu.VMEM((D_MODEL, D_FF), BF16),
            pltpu.VMEM((D_MODEL, D_FF), BF16),
            pltpu.VMEM((D_FF, D_MODEL), BF16),
            pltpu.VMEM((D_MODEL, D_MODEL), BF16),
        ],
        compiler_params=pltpu.CompilerParams(
            dimension_semantics=("arbitrary", "arbitrary"), vmem_limit_bytes=VMEM_LIMIT),
        name="sample_path",
    )(sinks, xs, ps, ck, cv, st, gmpre, gmpost, gfpre, gfpost, win, wout, wpool, pscale, wple,
      *[w for ws in ff_weights for w in ws])


def _cache_from_device_layout(c):
    return c.reshape(c.shape[0], c.shape[1], N_KV_HEADS, HEAD_DIM, WINDOW).transpose(0, 1, 4, 2, 3)


def kernel(x_prompt, x_sample, p_prompt, p_sample, cache_k, cache_v, state_pool, norm_mix_pre, norm_mix_post,
           norm_ffn_pre, norm_ffn_post, w_in, w_out, attn_sinks, w_pool, pool_scale, w_gate, w_up, w_down, w_ple,
           w_ple_gate):
    depth, nbat, seq, _ = p_prompt.shape
    dec_b, dec = x_sample.shape[0], x_sample.shape[1]
    assert 8 % dec == 0 and seq % PROMPT_TILE == 0 and (nbat * seq) % FFN_TILE == 0 and dec_b % SAMPLE_BATCHES == 0
    assert cache_k.shape[2:] == (WINDOW, N_KV_HEADS, HEAD_DIM) and state_pool.shape[2:] == (POOL_STATE, POOL_WIDTH)

    win, wout, wpool, wple = w_in.astype(BF16), w_out.astype(BF16), w_pool.astype(BF16), w_ple.astype(BF16)
    g_mix_pre = norm_mix_pre.reshape(depth, 1, D_MODEL)
    g_mix_post = norm_mix_post.reshape(depth, 1, D_MODEL)
    g_ffn_pre = norm_ffn_pre.reshape(depth, 1, D_MODEL)
    g_ffn_post = norm_ffn_post.reshape(depth, 1, D_MODEL)
    pscale = pool_scale.reshape(depth, 1, POOL_WIDTH)

    pp = p_prompt.reshape(depth, nbat * seq, PLE_DIM)
    ps = p_sample.reshape(depth, dec_b * dec, PLE_DIM)
    ck = cache_k.transpose(0, 1, 3, 4, 2).reshape(depth, dec_b, KV_WIDTH, WINDOW)
    cv = cache_v.transpose(0, 1, 3, 4, 2).reshape(depth, dec_b, KV_WIDTH, WINDOW)

    yp = x_prompt
    kp_l, vp_l, sp_l, ff_weights = [], [], [], []
    for i in range(depth):
        yp, kp, vp, sp, wg, wu, wd, wpg = _mix_prompt(yp, attn_sinks, g_mix_pre, g_mix_post, win, wout, wpool, pscale,
                                                      w_gate, w_up, w_down, w_ple_gate, layer=i, tm=PROMPT_TILE,
                                                      piece=PROMPT_PIECE)
        yp = _ffn(yp.reshape(nbat * seq, D_MODEL), pp, g_ffn_pre, g_ffn_post, wg, wu, wd, wple, wpg,
                  layer=i, tm=FFN_TILE, nsub=FFN_SUBTILES).reshape(nbat, seq, D_MODEL)
        ff_weights.append((wg, wu, wd, wpg))
        kp_l.append(kp)
        vp_l.append(vp)
        sp_l.append(sp[:, POOL_PAD - POOL_STATE:])

    ys, ks, vs, ss = _sample_path(x_sample.reshape(dec_b * dec, D_MODEL), ps, attn_sinks, ck, cv,
                                  state_pool.transpose(0, 2, 1, 3),
                                  g_mix_pre, g_mix_post, g_ffn_pre, g_ffn_post, win, wout, wpool, pscale,
                                  wple, ff_weights, bb=SAMPLE_BATCHES, dec=dec, nsub=SAMPLE_SUBTILES)

    kv_shape = (depth, -1, WINDOW, N_KV_HEADS, HEAD_DIM)
    return (yp, ys.reshape(dec_b, dec, D_MODEL),
            jnp.stack(kp_l).reshape(kv_shape), jnp.stack(vp_l).reshape(kv_shape), jnp.stack(sp_l),
            _cache_from_device_layout(ks), _cache_from_device_layout(vs), ss.transpose(0, 2, 1, 3))
```

```python
import functools

import jax
import jax.numpy as jnp
from jax import lax
from jax.experimental import pallas as pl
from jax.experimental.pallas import tpu as pltpu

D_MODEL = 1024
DEPTH = 4
ATTN_WIDTH = 512
HEAD_DIM = 64
N_HEADS = 8
N_KV_HEADS = 2
GROUP = 4
KV_WIDTH = 128
WINDOW = 128
POOL_WIDTH = 512
POOL_WINDOWS = (2, 4, 8, 16)
POOL_GROUP_WIDTH = 128
POOL_STATE = 15
IN_WIDTH = 1280
D_FF = 2816
PLE_DIM = 256
EPS = 1e-6

K_OFF = ATTN_WIDTH
V_OFF = ATTN_WIDTH + KV_WIDTH
U_OFF = ATTN_WIDTH + 2 * KV_WIDTH
LANES = 128
POOL_PAD = 16
POOL_HEAD = 8 + POOL_PAD
LOG2E = 1.4426950408889634
FF_CHUNKS = ((0, 1024), (1024, 2048), (2048, D_FF))

BF16 = jnp.bfloat16
F32 = jnp.float32
V7X_VMEM_BYTES = 64 * 1024 * 1024
VMEM_LIMIT = V7X_VMEM_BYTES - 8 * 1024 * 1024

PROMPT_TILE = 1024
PROMPT_PIECE = 512
ATTN_GROUP = 4
FFN_TILE = 1024
FFN_SUBTILES = 2
SAMPLE_BATCHES = 8
SAMPLE_SUBTILES = 1


def _rms(x, g):
    return x * lax.rsqrt(jnp.mean(x * x, axis=-1, keepdims=True) + EPS) * g


def _sigmoid(x):
    return 1.0 / (1.0 + jnp.exp(-x))


def _dot(a, b):
    return jnp.dot(a, b, preferred_element_type=F32)


def _dot_t(a, b):
    return lax.dot_general(a, b, (((1,), (1,)), ((), ())), preferred_element_type=F32)


def _const_spec(shape, layer=None):
    if layer is None:
        return pl.BlockSpec(shape, lambda *_: (0,) * len(shape), pipeline_mode=pl.Buffered(1))
    return pl.BlockSpec((None,) + shape, lambda *_: (layer,) + (0,) * len(shape),
                        pipeline_mode=pl.Buffered(1))


def _mix_prompt_kernel(sinks_ref, x_ref, gpre_ref, gpost_ref, win_ref, wout_ref, wpool_ref, pscale_ref,
                       wg32_ref, wu32_ref, wd32_ref, wpg32_ref,
                       y_ref, klast_ref, vlast_ref, plast_ref, wg16_ref, wu16_ref, wd16_ref, wpg16_ref,
                       z_ref, kext_ref, vt_ref, uext_ref, ps1_ref, ps2_ref, mix_ref, bias_ref, *, tm, piece, layer):
    s = pl.program_id(1)
    nblk = tm // WINDOW

    for w32_ref, w16_ref in ((wg32_ref, wg16_ref), (wu32_ref, wu16_ref), (wd32_ref, wd16_ref),
                             (wpg32_ref, wpg16_ref)):
        w16_ref[...] = w32_ref[...].astype(BF16)

    @pl.when((pl.program_id(0) == 0) & (s == 0))
    def _():
        c = lax.broadcasted_iota(jnp.int32, (2 * WINDOW, WINDOW), 0)
        r = lax.broadcasted_iota(jnp.int32, (2 * WINDOW, WINDOW), 1)
        dist = r + WINDOW - c
        ok = (dist >= 0) & (dist < WINDOW)
        bias_ref[0] = jnp.where(ok, 0.0, -jnp.inf)
        bias_ref[1] = jnp.where(ok & (c >= WINDOW), 0.0, -jnp.inf)

    @pl.when(s == 0)
    def _():
        kext_ref[0:WINDOW, :] = jnp.zeros((WINDOW, KV_WIDTH), F32)
        vt_ref[:, 0:WINDOW] = jnp.zeros((KV_WIDTH, WINDOW), F32)
        uext_ref[0:POOL_HEAD, :] = jnp.zeros((POOL_HEAD, POOL_WIDTH), F32)
        ps1_ref[0:8, :] = jnp.zeros((8, POOL_GROUP_WIDTH), F32)
        ps2_ref[0:8, :] = jnp.zeros((8, POOL_GROUP_WIDTH), F32)

    lane = lax.broadcasted_iota(jnp.int32, (WINDOW, LANES), 1)
    lo = lane < HEAD_DIM
    lo2 = jnp.concatenate([lo, lo], axis=0)
    row8 = lax.broadcasted_iota(jnp.int32, (8, GROUP * WINDOW), 0)
    kcol = lax.broadcasted_iota(jnp.int32, (HEAD_DIM, 2 * WINDOW), 1)
    ones = jnp.ones((HEAD_DIM, 2 * WINDOW), F32)
    half = piece

    def project_in(hf):
        hr = slice(hf * half, (hf + 1) * half)
        h = _rms(x_ref[0, hr, :], gpre_ref[...]).astype(BF16)
        z_ref[hr, :] = _dot(h, win_ref[...])
        uext_ref[POOL_HEAD + hf * half:POOL_HEAD + (hf + 1) * half, :] = z_ref[hr, U_OFF:U_OFF + POOL_WIDTH]
        kext_ref[WINDOW + hf * half:WINDOW + (hf + 1) * half, :] = z_ref[hr, K_OFF:K_OFF + KV_WIDTH]
        for n in range(hf * half // WINDOW, (hf + 1) * half // WINDOW):
            r0 = n * WINDOW
            vt_ref[:, WINDOW + r0:2 * WINDOW + r0] = z_ref[r0:r0 + WINDOW, V_OFF:V_OFF + KV_WIDTH].T

    def attend_stages(blocks):
        st8 = {}

        def scores():
            for n in blocks:
                r0 = n * WINDOW
                rows = slice(r0, r0 + WINDOW)
                kcat = kext_ref[r0:r0 + 2 * WINDOW, :]
                kswap = pltpu.roll(kcat, HEAD_DIM, 1)
                bias1 = bias_ref[jnp.where(s == 0, 1, 0)] if n == 0 else bias_ref[0]
                bias = jnp.concatenate([bias1] * GROUP, axis=1)
                for kv in range(N_KV_HEADS):
                    kk = (jnp.where(lo2, kcat, kswap) if kv == 0 else jnp.where(lo2, kswap, kcat)).astype(BF16)
                    pieces, sinks = [], []
                    for g in range(GROUP):
                        hd = kv * GROUP + g
                        slab = hd // 2
                        qs = z_ref[rows, slab * LANES:(slab + 1) * LANES] * (HEAD_DIM ** -0.5 * LOG2E)
                        keep = lo if hd % 2 == 0 else jnp.logical_not(lo)
                        pieces.append(jnp.where(keep, qs, 0.0).astype(BF16))
                        sinks.append(jnp.full((1, WINDOW), sinks_ref[layer, hd] * LOG2E, F32))
                    q4 = jnp.concatenate(pieces, axis=0)
                    st8[n, kv, "sink"] = jnp.concatenate(sinks, axis=1)
                    st8[n, kv, "st"] = _dot_t(kk, q4) + bias

        def softmax():
            for n in blocks:
                for kv in range(N_KV_HEADS):
                    st, sink = st8.pop((n, kv, "st")), st8.pop((n, kv, "sink"))
                    m = jnp.maximum(jnp.max(st, axis=0, keepdims=True), sink)
                    p = jnp.exp2(st - m)
                    top = jnp.where(row8 == 0, jnp.exp2(sink - m), p[0:8])
                    st8[n, kv, "p"] = jnp.concatenate([top, p[8:]], axis=0).astype(BF16)

        def values():
            for n in blocks:
                r0 = n * WINDOW
                for kv in range(N_KV_HEADS):
                    vth = vt_ref[kv * HEAD_DIM:(kv + 1) * HEAD_DIM, r0:r0 + 2 * WINDOW]
                    a = jnp.concatenate([jnp.where(kcol == 0, 0.0, vth), ones], axis=0).astype(BF16)
                    st8[n, kv, "ot"] = _dot(a, st8.pop((n, kv, "p")))

        def store():
            for n in blocks:
                rows = slice(n * WINDOW, (n + 1) * WINDOW)
                for kv in range(N_KV_HEADS):
                    ot = st8.pop((n, kv, "ot"))
                    inv = 1.0 / ot[HEAD_DIM:HEAD_DIM + 8]
                    on = ot[0:HEAD_DIM] * jnp.concatenate([inv] * (HEAD_DIM // 8), axis=0)
                    for j in range(GROUP // 2):
                        c0 = 2 * j * WINDOW
                        pair = jnp.concatenate([on[:, c0:c0 + WINDOW], on[:, c0 + WINDOW:c0 + 2 * WINDOW]], axis=0)
                        slab = kv * (GROUP // 2) + j
                        mix_ref[rows, slab * LANES:(slab + 1) * LANES] = pair.T.astype(BF16)

        return scores, softmax, values, store

    def pool(hf):
        t0 = hf * half
        pos = lax.broadcasted_iota(jnp.int32, (half, 1), 0) + (s * tm + t0)
        first = POOL_HEAD + t0
        ext = slice(first - POOL_PAD, first + half)
        for gi, w in enumerate(POOL_WINDOWS):
            cols = slice(gi * POOL_GROUP_WIDTH, (gi + 1) * POOL_GROUP_WIDTH)
            src, src_cols, span, bufs = uext_ref, cols, 1, [ps1_ref, ps2_ref]
            while 2 * span < w:
                dst = bufs[0]
                dst[ext, :] = src[ext, src_cols] + src[first - POOL_PAD - span:first + half - span, src_cols]
                src, src_cols, span, bufs = dst, slice(None), 2 * span, bufs[::-1]
            tot = src[first:first + half, src_cols] + src[first - span:first - span + half, src_cols]
            cnt = jnp.minimum(w, pos + 1).astype(F32)
            dlt = (tot / cnt - uext_ref[first:first + half, cols]).astype(BF16)
            yp = _dot(dlt, wpool_ref[gi]) * pscale_ref[:, cols]
            mix_ref[t0:t0 + half, ATTN_WIDTH + gi * POOL_GROUP_WIDTH:ATTN_WIDTH + (gi + 1) * POOL_GROUP_WIDTH] = (
                yp.astype(BF16))

    def project_out(hf):
        hr = slice(hf * half, (hf + 1) * half)
        mixed = _dot(mix_ref[hr, :], wout_ref[...])
        y_ref[0, hr, :] = x_ref[0, hr, :] + _rms(mixed, gpost_ref[...])

    per_half = half // WINDOW
    n_piece = tm // half
    project_in(0)
    for hf in range(n_piece):
        blocks = list(range(hf * per_half, (hf + 1) * per_half))
        groups = [blocks[i:i + ATTN_GROUP] for i in range(0, len(blocks), ATTN_GROUP)]
        for gi, group in enumerate(groups):
            scores, softmax, values, store = attend_stages(group)
            scores()
            if gi == 0 and hf + 1 < n_piece:
                project_in(hf + 1)
            if gi == len(groups) - 1 and hf > 0:
                project_out(hf - 1)
            softmax()
            values()
            store()
        pool(hf)
    project_out(n_piece - 1)

    @pl.when(s == pl.num_programs(1) - 1)
    def _():
        klast_ref[0] = z_ref[tm - WINDOW:tm, K_OFF:K_OFF + KV_WIDTH]
        vlast_ref[0] = z_ref[tm - WINDOW:tm, V_OFF:V_OFF + KV_WIDTH]
        plast_ref[0] = uext_ref[POOL_HEAD + tm - POOL_PAD:POOL_HEAD + tm, :]

    kext_ref[0:WINDOW, :] = kext_ref[tm:tm + WINDOW, :]
    vt_ref[:, 0:WINDOW] = vt_ref[:, tm:tm + WINDOW]
    uext_ref[POOL_HEAD - POOL_PAD:POOL_HEAD, :] = uext_ref[POOL_HEAD + tm - POOL_PAD:POOL_HEAD + tm, :]


def _row_chunk_specs(shape, n_steps, layer, step_of):
    rows, cols = shape
    chunk = rows // n_steps
    assert chunk * n_steps == rows and chunk % 16 == 0, (shape, n_steps)
    return (pl.BlockSpec((None, chunk, cols), lambda *g: (layer, step_of(*g), 0)),
            pl.BlockSpec((chunk, cols), lambda *g: (step_of(*g), 0)))


def _mix_prompt(x, sinks, gpre, gpost, win, wout, wpool, pscale, wg32, wu32, wd32, wpg32, *, layer, tm, piece):
    b, s, _ = x.shape
    kern = functools.partial(_mix_prompt_kernel, tm=tm, piece=piece, layer=layer)
    n_tiles = s // tm
    cast_shapes = ((D_MODEL, D_FF), (D_MODEL, D_FF), (D_FF, D_MODEL), (D_MODEL, D_MODEL))
    cast_specs = [_row_chunk_specs(shape, b * n_tiles, layer, lambda i, j: i * n_tiles + j) for shape in cast_shapes]
    return pl.pallas_call(
        kern,
        grid=(b, n_tiles),
        in_specs=[
            pl.BlockSpec(memory_space=pltpu.SMEM),
            pl.BlockSpec((1, tm, D_MODEL), lambda i, j: (i, j, 0)),
            _const_spec((1, D_MODEL), layer),
            _const_spec((1, D_MODEL), layer),
            _const_spec((D_MODEL, IN_WIDTH), layer),
            _const_spec((D_MODEL, D_MODEL), layer),
            _const_spec((len(POOL_WINDOWS), POOL_GROUP_WIDTH, POOL_GROUP_WIDTH), layer),
            _const_spec((1, POOL_WIDTH), layer),
        ] + [spec_in for spec_in, _ in cast_specs],
        out_specs=[
            pl.BlockSpec((1, tm, D_MODEL), lambda i, j: (i, j, 0)),
            pl.BlockSpec((1, WINDOW, KV_WIDTH), lambda i, j: (i, 0, 0)),
            pl.BlockSpec((1, WINDOW, KV_WIDTH), lambda i, j: (i, 0, 0)),
            pl.BlockSpec((1, POOL_PAD, POOL_WIDTH), lambda i, j: (i, 0, 0)),
        ] + [spec_out for _, spec_out in cast_specs],
        out_shape=[
            jax.ShapeDtypeStruct((b, s, D_MODEL), F32),
            jax.ShapeDtypeStruct((b, WINDOW, KV_WIDTH), F32),
            jax.ShapeDtypeStruct((b, WINDOW, KV_WIDTH), F32),
            jax.ShapeDtypeStruct((b, POOL_PAD, POOL_WIDTH), F32),
        ] + [jax.ShapeDtypeStruct(shape, BF16) for shape in cast_shapes],
        scratch_shapes=[
            pltpu.VMEM((tm, IN_WIDTH), F32),
            pltpu.VMEM((WINDOW + tm, KV_WIDTH), F32),
            pltpu.VMEM((KV_WIDTH, WINDOW + tm), F32),
            pltpu.VMEM((POOL_HEAD + tm, POOL_WIDTH), F32),
            pltpu.VMEM((POOL_HEAD + tm, POOL_GROUP_WIDTH), F32),
            pltpu.VMEM((POOL_HEAD + tm, POOL_GROUP_WIDTH), F32),
            pltpu.VMEM((tm, D_MODEL), BF16),
            pltpu.VMEM((2, 2 * WINDOW, WINDOW), F32),
        ],
        compiler_params=pltpu.CompilerParams(
            dimension_semantics=("arbitrary", "arbitrary"), vmem_limit_bytes=VMEM_LIMIT),
        name=f"mix_prompt_{layer}",
    )(sinks, x, gpre, gpost, win, wout, wpool, pscale, wg32, wu32, wd32, wpg32)


def _swiglu(f, wg_ref, wu_ref, wd_ref):
    d, pending = None, None
    for c0, c1 in FF_CHUNKS:
        g = _dot(f, wg_ref[:, c0:c1])
        u = _dot(f, wu_ref[:, c0:c1])
        if pending is not None:
            part = _dot(pending[0], wd_ref[pending[1]:pending[2], :])
            d = part if d is None else d + part
        pending = ((g * _sigmoid(g) * u).astype(BF16), c0, c1)
    return d + _dot(pending[0], wd_ref[pending[1]:pending[2], :])


def _ffn_rows(x, p, gpre_ref, gpost_ref, wg_ref, wu_ref, wd_ref, wple_ref, wpg_ref):
    f = _rms(x, gpre_ref[...]).astype(BF16)
    x = x + _rms(_swiglu(f, wg_ref, wu_ref, wd_ref), gpost_ref[...])
    gate = _sigmoid(_dot(x.astype(BF16), wpg_ref[...]))
    return x + gate * _dot(p.astype(BF16), wple_ref[...])


def _ffn_kernel(x_ref, p_ref, gpre_ref, gpost_ref, wg_ref, wu_ref, wd_ref, wple_ref, wpg_ref, o_ref, *, nsub):
    sub = x_ref.shape[0] // nsub
    for i in range(nsub):
        rows = slice(i * sub, (i + 1) * sub)
        o_ref[rows, :] = _ffn_rows(x_ref[rows, :], p_ref[rows, :], gpre_ref, gpost_ref, wg_ref, wu_ref, wd_ref,
                                   wple_ref, wpg_ref)


def _ffn(x, p, gpre, gpost, wg, wu, wd, wple, wpg, *, layer, tm, nsub):
    n = x.shape[0]
    return pl.pallas_call(
        functools.partial(_ffn_kernel, nsub=nsub),
        grid=(n // tm,),
        in_specs=[
            pl.BlockSpec((tm, D_MODEL), lambda i: (i, 0)),
            pl.BlockSpec((None, tm, PLE_DIM), lambda i: (layer, i, 0)),
            _const_spec((1, D_MODEL), layer),
            _const_spec((1, D_MODEL), layer),
            _const_spec((D_MODEL, D_FF)),
            _const_spec((D_MODEL, D_FF)),
            _const_spec((D_FF, D_MODEL)),
            _const_spec((PLE_DIM, D_MODEL), layer),
            _const_spec((D_MODEL, D_MODEL)),
        ],
        out_specs=pl.BlockSpec((tm, D_MODEL), lambda i: (i, 0)),
        out_shape=jax.ShapeDtypeStruct((n, D_MODEL), F32),
        compiler_params=pltpu.CompilerParams(
            dimension_semantics=("arbitrary",), vmem_limit_bytes=VMEM_LIMIT),
        name=f"ffn_prompt_{layer}",
    )(x, p, gpre, gpost, wg, wu, wd, wple, wpg)


def _layer_spec(shape, single=False):
    kwargs = dict(pipeline_mode=pl.Buffered(1)) if single else {}
    return pl.BlockSpec((None,) + shape, lambda l, c: (l,) + (0,) * len(shape), **kwargs)


def _sample_kernel(sinks_ref, xs_ref, ps_ref, ck_ref, cv_ref, st_ref, gmpre_ref, gmpost_ref, gfpre_ref, gfpost_ref,
                   win_ref, wout_ref, wpool_ref, pscale_ref, wple_ref, *refs, bb, dec, nsub, depth):
    chunk_refs = refs[:4 * depth]
    (y_ref, ko_ref, vo_ref, po_ref, z_ref, mix_ref, ustage_ref, dlt_ref,
     wg_ref, wu_ref, wd_ref, wpg_ref) = refs[4 * depth:]
    l = pl.program_id(0)
    c = pl.program_id(1)
    n_rows = y_ref.shape[0]

    @pl.when((l == 0) & (c == 0))
    def _():
        y_ref[...] = xs_ref[...]

    @pl.when(c == 0)
    def _():
        h = _rms(y_ref[...], gmpre_ref[...]).astype(BF16)
        z_ref[...] = _dot(h, win_ref[...])

    for k in range(depth):
        @pl.when(l == k)
        def _(k=k):
            for src_ref, dst_ref in zip(chunk_refs[4 * k:4 * k + 4], (wg_ref, wu_ref, wd_ref, wpg_ref)):
                rows_w = src_ref.shape[0]
                dst_ref[pl.ds(pl.multiple_of(c * rows_w, 16), rows_w), :] = src_ref[...]

    per_tile = 8 // dec
    base = pl.multiple_of(c * (bb * dec), 8)
    lane = lax.broadcasted_iota(jnp.int32, (8, LANES), 1)
    lo = lane < HEAD_DIM
    hi = jnp.logical_not(lo)
    rows64 = N_HEADS * 8
    ext_w = 2 * WINDOW
    new0 = ext_w - 8
    trow = lax.broadcasted_iota(jnp.int32, (rows64, ext_w), 0) & 7
    tstep = trow % dec
    tbat = trow // dec
    klane = lax.broadcasted_iota(jnp.int32, (rows64, ext_w), 1)
    fresh = klane - new0
    valid = ((klane < WINDOW) & (klane >= tstep + 1)) | (
        (klane >= new0) & (fresh // dec == tbat) & (fresh % dec <= tstep))
    tbat_o = tbat[:, 0:LANES]
    wlane = lax.broadcasted_iota(jnp.int32, (KV_WIDTH, WINDOW), 1)
    zpad = jnp.zeros((WINDOW - 8, KV_WIDTH), F32)

    tiles = range(bb // per_tile)
    st8 = {}

    for t in tiles:
        rows = pl.ds(base + 8 * t, 8)
        q = z_ref[rows, 0:ATTN_WIDTH] * (HEAD_DIM ** -0.5)
        knew = jnp.concatenate([zpad, z_ref[rows, K_OFF:K_OFF + KV_WIDTH]], axis=0)
        vnew = jnp.concatenate([zpad, z_ref[rows, V_OFF:V_OFF + KV_WIDTH]], axis=0)
        st8[t, "knew"], st8[t, "vnew"] = knew, vnew
        pieces, sinks = [], []
        for hd in range(N_HEADS):
            slab, half, kv = hd // 2, hd % 2, hd // GROUP
            qs = q[:, slab * LANES:(slab + 1) * LANES]
            src = qs if half == kv else pltpu.roll(qs, HEAD_DIM, 1)
            pieces.append(jnp.where(lo if kv == 0 else hi, src, 0.0))
            sinks.append(jnp.full((8, 1), sinks_ref[l, hd], F32))
        lhs_b = jnp.concatenate(pieces, axis=0).astype(BF16)
        st8[t, "sink"] = jnp.concatenate(sinks, axis=0)
        sc = None
        for i in range(per_tile):
            si = _dot(lhs_b, ck_ref[per_tile * t + i].astype(BF16))
            sc = si if sc is None else jnp.where(tbat_o == i, si, sc)
        sc = jnp.concatenate([sc, _dot_t(lhs_b, knew.astype(BF16))], axis=1)
        st8[t, "sc"] = jnp.where(valid, sc, -jnp.inf)

    for t in tiles:
        sc, sink = st8.pop((t, "sc")), st8.pop((t, "sink"))
        m = jnp.maximum(jnp.max(sc, axis=-1, keepdims=True), sink)
        p = jnp.exp(sc - m)
        st8[t, "denom"] = jnp.sum(p, axis=-1, keepdims=True) + jnp.exp(sink - m)
        st8[t, "pb"] = p.astype(BF16)

    for t in tiles:
        pb = st8.pop((t, "pb"))
        o = None
        for i in range(per_tile):
            oi = _dot_t(pb[:, 0:WINDOW], cv_ref[per_tile * t + i].astype(BF16))
            o = oi if o is None else jnp.where(tbat_o == i, oi, o)
        st8[t, "o"] = o + _dot(pb[:, WINDOW:ext_w], st8[t, "vnew"].astype(BF16))

    for t in tiles:
        rows = pl.ds(base + 8 * t, 8)
        o = st8.pop((t, "o")) / st8.pop((t, "denom"))
        for slab in range(N_HEADS // 2):
            kv = (2 * slab) // GROUP
            even = o[16 * slab:16 * slab + 8]
            odd = o[16 * slab + 8:16 * slab + 16]
            if kv == 0:
                res = jnp.where(lo, even, pltpu.roll(odd, HEAD_DIM, 1))
            else:
                res = jnp.where(lo, pltpu.roll(even, HEAD_DIM, 1), odd)
            mix_ref[rows, slab * LANES:(slab + 1) * LANES] = res

    for t in tiles:
        for new, c_ref, o_ref in ((st8.pop((t, "knew")), ck_ref, ko_ref), (st8.pop((t, "vnew")), cv_ref, vo_ref)):
            cols = new.T
            for bi in range(per_tile):
                b = per_tile * t + bi
                shift = dec * (per_tile - 1 - bi)
                mine = cols if shift == 0 else pltpu.roll(cols, shift, 1)
                kept = pltpu.roll(c_ref[b], WINDOW - dec, 1)
                o_ref[b] = jnp.where(wlane < WINDOW - dec, kept, mine)

    chunk = pl.ds(base, bb * dec)
    for r in range(POOL_STATE - dec):
        po_ref[r] = st_ref[r + dec]
    for gi, w in enumerate(POOL_WINDOWS):
        cols = slice(gi * POOL_GROUP_WIDTH, (gi + 1) * POOL_GROUP_WIDTH)
        ustage_ref[gi] = z_ref[chunk, U_OFF + gi * POOL_GROUP_WIDTH:U_OFF + (gi + 1) * POOL_GROUP_WIDTH]
        u = [ustage_ref[gi, pl.ds(j, bb, stride=dec), :] for j in range(dec)]
        suffix, acc = {}, None
        for m in range(1, w):
            row = st_ref[POOL_STATE - m, :, cols]
            acc = row if acc is None else acc + row
            suffix[m] = acc
        for j in range(dec):
            po_ref[POOL_STATE - dec + j, :, cols] = u[j]
            tot = suffix.get(w - 1 - j)
            for i in range(max(0, j - w + 1), j + 1):
                tot = u[i] if tot is None else tot + u[i]
            dlt_ref[gi, pl.ds(j, bb, stride=dec), :] = tot / float(w) - u[j]

    for gi in range(len(POOL_WINDOWS)):
        cols = slice(gi * POOL_GROUP_WIDTH, (gi + 1) * POOL_GROUP_WIDTH)
        yp = _dot(dlt_ref[gi].astype(BF16), wpool_ref[gi]) * pscale_ref[:, cols]
        mix_ref[chunk, ATTN_WIDTH + gi * POOL_GROUP_WIDTH:ATTN_WIDTH + (gi + 1) * POOL_GROUP_WIDTH] = yp

    @pl.when(c == pl.num_programs(1) - 1)
    def _():
        sub = n_rows // nsub
        for i in range(nsub):
            rs = slice(i * sub, (i + 1) * sub)
            mixed = _dot(mix_ref[rs, :].astype(BF16), wout_ref[...])
            x = y_ref[rs, :] + _rms(mixed, gmpost_ref[...])
            y_ref[rs, :] = _ffn_rows(x, ps_ref[rs, :], gfpre_ref, gfpost_ref, wg_ref, wu_ref, wd_ref, wple_ref,
                                     wpg_ref)


def _sample_path(xs, ps, sinks, ck, cv, st, gmpre, gmpost, gfpre, gfpost, win, wout, wpool, pscale,
                 wple, ff_weights, *, bb, dec, nsub):
    depth, nb = ck.shape[0], ck.shape[1]
    n = nb * dec
    kern = functools.partial(_sample_kernel, bb=bb, dec=dec, nsub=nsub, depth=depth)
    n_chunks = nb // bb

    def _chunk_spec(w, k):
        rows, cols = w.shape
        assert rows % n_chunks == 0 and (rows // n_chunks) % 16 == 0
        return pl.BlockSpec((rows // n_chunks, cols),
                            lambda l, c: (jnp.where(l == k, c, jnp.where(l > k, n_chunks - 1, 0)), 0))

    cache_spec = pl.BlockSpec((None, bb, KV_WIDTH, WINDOW), lambda l, c: (l, c, 0, 0))
    state_spec = pl.BlockSpec((None, POOL_STATE, bb, POOL_WIDTH), lambda l, c: (l, 0, c, 0))
    return pl.pallas_call(
        kern,
        grid=(depth, nb // bb),
        in_specs=[
            pl.BlockSpec(memory_space=pltpu.SMEM),
            pl.BlockSpec((n, D_MODEL), lambda l, c: (0, 0), pipeline_mode=pl.Buffered(1)),
            _layer_spec((n, PLE_DIM)),
            cache_spec, cache_spec, state_spec,
            _layer_spec((1, D_MODEL)), _layer_spec((1, D_MODEL)), _layer_spec((1, D_MODEL)), _layer_spec((1, D_MODEL)),
            _layer_spec((D_MODEL, IN_WIDTH), single=True),
            _layer_spec((D_MODEL, D_MODEL), single=True),
            _layer_spec((len(POOL_WINDOWS), POOL_GROUP_WIDTH, POOL_GROUP_WIDTH)),
            _layer_spec((1, POOL_WIDTH)),
            _layer_spec((PLE_DIM, D_MODEL), single=True),
        ] + [_chunk_spec(w, k) for k, ws in enumerate(ff_weights) for w in ws],
        out_specs=[
            pl.BlockSpec((n, D_MODEL), lambda l, c: (0, 0)),
            cache_spec, cache_spec, state_spec,
        ],
        out_shape=[
            jax.ShapeDtypeStruct((n, D_MODEL), F32),
            jax.ShapeDtypeStruct((depth, nb, KV_WIDTH, WINDOW), F32),
            jax.ShapeDtypeStruct((depth, nb, KV_WIDTH, WINDOW), F32),
            jax.ShapeDtypeStruct((depth, POOL_STATE, nb, POOL_WIDTH), F32),
        ],
        scratch_shapes=[
            pltpu.VMEM((n, IN_WIDTH), F32),
            pltpu.VMEM((n, D_MODEL), F32),
            pltpu.VMEM((len(POOL_WINDOWS), bb * dec, POOL_GROUP_WIDTH), F32),
            pltpu.VMEM((len(POOL_WINDOWS), bb * dec, POOL_GROUP_WIDTH), F32),
            pltpu.VMEM((D_MODEL, D_FF), BF16),
            pltpu.VMEM((D_MODEL, D_FF), BF16),
            pltpu.VMEM((D_FF, D_MODEL), BF16),
            pltpu.VMEM((D_MODEL, D_MODEL), BF16),
        ],
        compiler_params=pltpu.CompilerParams(
            dimension_semantics=("arbitrary", "arbitrary"), vmem_limit_bytes=VMEM_LIMIT),
        name="sample_path",
    )(sinks, xs, ps, ck, cv, st, gmpre, gmpost, gfpre, gfpost, win, wout, wpool, pscale, wple,
      *[w for ws in ff_weights for w in ws])


def _cache_from_device_layout(c):
    return c.reshape(c.shape[0], c.shape[1], N_KV_HEADS, HEAD_DIM, WINDOW).transpose(0, 1, 4, 2, 3)


def kernel(x_prompt, x_sample, p_prompt, p_sample, cache_k, cache_v, state_pool, norm_mix_pre, norm_mix_post,
           norm_ffn_pre, norm_ffn_post, w_in, w_out, attn_sinks, w_pool, pool_scale, w_gate, w_up, w_down, w_ple,
           w_ple_gate):
    depth, nbat, seq, _ = p_prompt.shape
    dec_b, dec = x_sample.shape[0], x_sample.shape[1]
    assert 8 % dec == 0 and seq % PROMPT_TILE == 0 and (nbat * seq) % FFN_TILE == 0 and dec_b % SAMPLE_BATCHES == 0
    assert cache_k.shape[2:] == (WINDOW, N_KV_HEADS, HEAD_DIM) and state_pool.shape[2:] == (POOL_STATE, POOL_WIDTH)

    win, wout, wpool, wple = w_in.astype(BF16), w_out.astype(BF16), w_pool.astype(BF16), w_ple.astype(BF16)
    g_mix_pre = norm_mix_pre.reshape(depth, 1, D_MODEL)
    g_mix_post = norm_mix_post.reshape(depth, 1, D_MODEL)
    g_ffn_pre = norm_ffn_pre.reshape(depth, 1, D_MODEL)
    g_ffn_post = norm_ffn_post.reshape(depth, 1, D_MODEL)
    pscale = pool_scale.reshape(depth, 1, POOL_WIDTH)

    pp = p_prompt.reshape(depth, nbat * seq, PLE_DIM)
    ps = p_sample.reshape(depth, dec_b * dec, PLE_DIM)
    ck = cache_k.transpose(0, 1, 3, 4, 2).reshape(depth, dec_b, KV_WIDTH, WINDOW)
    cv = cache_v.transpose(0, 1, 3, 4, 2).reshape(depth, dec_b, KV_WIDTH, WINDOW)

    yp = x_prompt
    kp_l, vp_l, sp_l, ff_weights = [], [], [], []
    for i in range(depth):
        yp, kp, vp, sp, wg, wu, wd, wpg = _mix_prompt(yp, attn_sinks, g_mix_pre, g_mix_post, win, wout, wpool, pscale,
                                                      w_gate, w_up, w_down, w_ple_gate, layer=i, tm=PROMPT_TILE,
                                                      piece=PROMPT_PIECE)
        yp = _ffn(yp.reshape(nbat * seq, D_MODEL), pp, g_ffn_pre, g_ffn_post, wg, wu, wd, wple, wpg,
                  layer=i, tm=FFN_TILE, nsub=FFN_SUBTILES).reshape(nbat, seq, D_MODEL)
        ff_weights.append((wg, wu, wd, wpg))
        kp_l.append(kp)
        vp_l.append(vp)
        sp_l.append(sp[:, POOL_PAD - POOL_STATE:])

    ys, ks, vs, ss = _sample_path(x_sample.reshape(dec_b * dec, D_MODEL), ps, attn_sinks, ck, cv,
                                  state_pool.transpose(0, 2, 1, 3),
                                  g_mix_pre, g_mix_post, g_ffn_pre, g_ffn_post, win, wout, wpool, pscale,
                                  wple, ff_weights, bb=SAMPLE_BATCHES, dec=dec, nsub=SAMPLE_SUBTILES)

    kv_shape = (depth, -1, WINDOW, N_KV_HEADS, HEAD_DIM)
    return (yp, ys.reshape(dec_b, dec, D_MODEL),
            jnp.stack(kp_l).reshape(kv_shape), jnp.stack(vp_l).reshape(kv_shape), jnp.stack(sp_l),
            _cache_from_device_layout(ks), _cache_from_device_layout(vs), ss.transpose(0, 2, 1, 3))
```

```python
import functools

import jax
import jax.numpy as jnp
from jax import lax
from jax.experimental import pallas as pl
from jax.experimental.pallas import tpu as pltpu

D_MODEL = 1024
DEPTH = 4
ATTN_WIDTH = 512
HEAD_DIM = 64
N_HEADS = 8
N_KV_HEADS = 2
GROUP = 4
KV_WIDTH = 128
WINDOW = 128
POOL_WIDTH = 512
POOL_WINDOWS = (2, 4, 8, 16)
POOL_GROUP_WIDTH = 128
POOL_STATE = 15
IN_WIDTH = 1280
D_FF = 2816
PLE_DIM = 256
EPS = 1e-6

K_OFF = ATTN_WIDTH
V_OFF = ATTN_WIDTH + KV_WIDTH
U_OFF = ATTN_WIDTH + 2 * KV_WIDTH
LANES = 128
POOL_PAD = 16
POOL_HEAD = 8 + POOL_PAD
LOG2E = 1.4426950408889634
FF_CHUNKS = ((0, 1024), (1024, 2048), (2048, D_FF))

BF16 = jnp.bfloat16
F32 = jnp.float32
V7X_VMEM_BYTES = 64 * 1024 * 1024
VMEM_LIMIT = V7X_VMEM_BYTES - 8 * 1024 * 1024

PROMPT_TILE = 1024
PROMPT_PIECE = 512
ATTN_GROUP = 4
FFN_TILE = 1024
FFN_SUBTILES = 2
SAMPLE_BATCHES = 8
SAMPLE_SUBTILES = 1


def _rms(x, g):
    return x * lax.rsqrt(jnp.mean(x * x, axis=-1, keepdims=True) + EPS) * g


def _sigmoid(x):
    return 1.0 / (1.0 + jnp.exp(-x))


def _dot(a, b):
    return jnp.dot(a, b, preferred_element_type=F32)


def _dot_t(a, b):
    return lax.dot_general(a, b, (((1,), (1,)), ((), ())), preferred_element_type=F32)


def _const_spec(shape, layer=None):
    if layer is None:
        return pl.BlockSpec(shape, lambda *_: (0,) * len(shape), pipeline_mode=pl.Buffered(1))
    return pl.BlockSpec((None,) + shape, lambda *_: (layer,) + (0,) * len(shape),
                        pipeline_mode=pl.Buffered(1))


def _mix_prompt_kernel(sinks_ref, x_ref, gpre_ref, gpost_ref, win_ref, wout_ref, wpool_ref, pscale_ref,
                       wg32_ref, wu32_ref, wd32_ref, wpg32_ref,
                       y_ref, klast_ref, vlast_ref, plast_ref, wg16_ref, wu16_ref, wd16_ref, wpg16_ref,
                       z_ref, kext_ref, vt_ref, uext_ref, ps1_ref, ps2_ref, mix_ref, bias_ref, *, tm, piece, layer):
    s = pl.program_id(1)
    nblk = tm // WINDOW

    for w32_ref, w16_ref in ((wg32_ref, wg16_ref), (wu32_ref, wu16_ref), (wd32_ref, wd16_ref),
                             (wpg32_ref, wpg16_ref)):
        w16_ref[...] = w32_ref[...].astype(BF16)

    @pl.when((pl.program_id(0) == 0) & (s == 0))
    def _():
        c = lax.broadcasted_iota(jnp.int32, (2 * WINDOW, WINDOW), 0)
        r = lax.broadcasted_iota(jnp.int32, (2 * WINDOW, WINDOW), 1)
        dist = r + WINDOW - c
        ok = (dist >= 0) & (dist < WINDOW)
        bias_ref[0] = jnp.where(ok, 0.0, -jnp.inf)
        bias_ref[1] = jnp.where(ok & (c >= WINDOW), 0.0, -jnp.inf)

    @pl.when(s == 0)
    def _():
        kext_ref[0:WINDOW, :] = jnp.zeros((WINDOW, KV_WIDTH), F32)
        vt_ref[:, 0:WINDOW] = jnp.zeros((KV_WIDTH, WINDOW), F32)
        uext_ref[0:POOL_HEAD, :] = jnp.zeros((POOL_HEAD, POOL_WIDTH), F32)
        ps1_ref[0:8, :] = jnp.zeros((8, POOL_GROUP_WIDTH), F32)
        ps2_ref[0:8, :] = jnp.zeros((8, POOL_GROUP_WIDTH), F32)

    lane = lax.broadcasted_iota(jnp.int32, (WINDOW, LANES), 1)
    lo = lane < HEAD_DIM
    lo2 = jnp.concatenate([lo, lo], axis=0)
    row8 = lax.broadcasted_iota(jnp.int32, (8, GROUP * WINDOW), 0)
    kcol = lax.broadcasted_iota(jnp.int32, (HEAD_DIM, 2 * WINDOW), 1)
    ones = jnp.ones((HEAD_DIM, 2 * WINDOW), F32)
    half = piece

    def project_in(hf):
        hr = slice(hf * half, (hf + 1) * half)
        h = _rms(x_ref[0, hr, :], gpre_ref[...]).astype(BF16)
        z_ref[hr, :] = _dot(h, win_ref[...])
        uext_ref[POOL_HEAD + hf * half:POOL_HEAD + (hf + 1) * half, :] = z_ref[hr, U_OFF:U_OFF + POOL_WIDTH]
        kext_ref[WINDOW + hf * half:WINDOW + (hf + 1) * half, :] = z_ref[hr, K_OFF:K_OFF + KV_WIDTH]
        for n in range(hf * half // WINDOW, (hf + 1) * half // WINDOW):
            r0 = n * WINDOW
            vt_ref[:, WINDOW + r0:2 * WINDOW + r0] = z_ref[r0:r0 + WINDOW, V_OFF:V_OFF + KV_WIDTH].T

    def attend_stages(blocks):
        st8 = {}

        def scores():
            for n in blocks:
                r0 = n * WINDOW
                rows = slice(r0, r0 + WINDOW)
                kcat = kext_ref[r0:r0 + 2 * WINDOW, :]
                kswap = pltpu.roll(kcat, HEAD_DIM, 1)
                bias1 = bias_ref[jnp.where(s == 0, 1, 0)] if n == 0 else bias_ref[0]
                bias = jnp.concatenate([bias1] * GROUP, axis=1)
                for kv in range(N_KV_HEADS):
                    kk = (jnp.where(lo2, kcat, kswap) if kv == 0 else jnp.where(lo2, kswap, kcat)).astype(BF16)
                    pieces, sinks = [], []
                    for g in range(GROUP):
                        hd = kv * GROUP + g
                        slab = hd // 2
                        qs = z_ref[rows, slab * LANES:(slab + 1) * LANES] * (HEAD_DIM ** -0.5 * LOG2E)
                        keep = lo if hd % 2 == 0 else jnp.logical_not(lo)
                        pieces.append(jnp.where(keep, qs, 0.0).astype(BF16))
                        sinks.append(jnp.full((1, WINDOW), sinks_ref[layer, hd] * LOG2E, F32))
                    q4 = jnp.concatenate(pieces, axis=0)
                    st8[n, kv, "sink"] = jnp.concatenate(sinks, axis=1)
                    st8[n, kv, "st"] = _dot_t(kk, q4) + bias

        def softmax():
            for n in blocks:
                for kv in range(N_KV_HEADS):
                    st, sink = st8.pop((n, kv, "st")), st8.pop((n, kv, "sink"))
                    m = jnp.maximum(jnp.max(st, axis=0, keepdims=True), sink)
                    p = jnp.exp2(st - m)
                    top = jnp.where(row8 == 0, jnp.exp2(sink - m), p[0:8])
                    st8[n, kv, "p"] = jnp.concatenate([top, p[8:]], axis=0).astype(BF16)

        def values():
            for n in blocks:
                r0 = n * WINDOW
                for kv in range(N_KV_HEADS):
                    vth = vt_ref[kv * HEAD_DIM:(kv + 1) * HEAD_DIM, r0:r0 + 2 * WINDOW]
                    a = jnp.concatenate([jnp.where(kcol == 0, 0.0, vth), ones], axis=0).astype(BF16)
                    st8[n, kv, "ot"] = _dot(a, st8.pop((n, kv, "p")))

        def store():
            for n in blocks:
                rows = slice(n * WINDOW, (n + 1) * WINDOW)
                for kv in range(N_KV_HEADS):
                    ot = st8.pop((n, kv, "ot"))
                    inv = 1.0 / ot[HEAD_DIM:HEAD_DIM + 8]
                    on = ot[0:HEAD_DIM] * jnp.concatenate([inv] * (HEAD_DIM // 8), axis=0)
                    for j in range(GROUP // 2):
                        c0 = 2 * j * WINDOW
                        pair = jnp.concatenate([on[:, c0:c0 + WINDOW], on[:, c0 + WINDOW:c0 + 2 * WINDOW]], axis=0)
                        slab = kv * (GROUP // 2) + j
                        mix_ref[rows, slab * LANES:(slab + 1) * LANES] = pair.T.astype(BF16)

        return scores, softmax, values, store

    def pool(hf):
        t0 = hf * half
        pos = lax.broadcasted_iota(jnp.int32, (half, 1), 0) + (s * tm + t0)
        first = POOL_HEAD + t0
        ext = slice(first - POOL_PAD, first + half)
        for gi, w in enumerate(POOL_WINDOWS):
            cols = slice(gi * POOL_GROUP_WIDTH, (gi + 1) * POOL_GROUP_WIDTH)
            src, src_cols, span, bufs = uext_ref, cols, 1, [ps1_ref, ps2_ref]
            while 2 * span < w:
                dst = bufs[0]
                dst[ext, :] = src[ext, src_cols] + src[first - POOL_PAD - span:first + half - span, src_cols]
                src, src_cols, span, bufs = dst, slice(None), 2 * span, bufs[::-1]
            tot = src[first:first + half, src_cols] + src[first - span:first - span + half, src_cols]
            cnt = jnp.minimum(w, pos + 1).astype(F32)
            dlt = (tot / cnt - uext_ref[first:first + half, cols]).astype(BF16)
            yp = _dot(dlt, wpool_ref[gi]) * pscale_ref[:, cols]
            mix_ref[t0:t0 + half, ATTN_WIDTH + gi * POOL_GROUP_WIDTH:ATTN_WIDTH + (gi + 1) * POOL_GROUP_WIDTH] = (
                yp.astype(BF16))

    def project_out(hf):
        hr = slice(hf * half, (hf + 1) * half)
        mixed = _dot(mix_ref[hr, :], wout_ref[...])
        y_ref[0, hr, :] = x_ref[0, hr, :] + _rms(mixed, gpost_ref[...])

    per_half = half // WINDOW
    n_piece = tm // half
    project_in(0)
    for hf in range(n_piece):
        blocks = list(range(hf * per_half, (hf + 1) * per_half))
        groups = [blocks[i:i + ATTN_GROUP] for i in range(0, len(blocks), ATTN_GROUP)]
        for gi, group in enumerate(groups):
            scores, softmax, values, store = attend_stages(group)
            scores()
            if gi == 0 and hf + 1 < n_piece:
                project_in(hf + 1)
            if gi == len(groups) - 1 and hf > 0:
                project_out(hf - 1)
            softmax()
            values()
            store()
        pool(hf)
    project_out(n_piece - 1)

    @pl.when(s == pl.num_programs(1) - 1)
    def _():
        klast_ref[0] = z_ref[tm - WINDOW:tm, K_OFF:K_OFF + KV_WIDTH]
        vlast_ref[0] = z_ref[tm - WINDOW:tm, V_OFF:V_OFF + KV_WIDTH]
        plast_ref[0] = uext_ref[POOL_HEAD + tm - POOL_PAD:POOL_HEAD + tm, :]

    kext_ref[0:WINDOW, :] = kext_ref[tm:tm + WINDOW, :]
    vt_ref[:, 0:WINDOW] = vt_ref[:, tm:tm + WINDOW]
    uext_ref[POOL_HEAD - POOL_PAD:POOL_HEAD, :] = uext_ref[POOL_HEAD + tm - POOL_PAD:POOL_HEAD + tm, :]


def _row_chunk_specs(shape, n_steps, layer, step_of):
    rows, cols = shape
    chunk = rows // n_steps
    assert chunk * n_steps == rows and chunk % 16 == 0, (shape, n_steps)
    return (pl.BlockSpec((None, chunk, cols), lambda *g: (layer, step_of(*g), 0)),
            pl.BlockSpec((chunk, cols), lambda *g: (step_of(*g), 0)))


def _mix_prompt(x, sinks, gpre, gpost, win, wout, wpool, pscale, wg32, wu32, wd32, wpg32, *, layer, tm, piece):
    b, s, _ = x.shape
    kern = functools.partial(_mix_prompt_kernel, tm=tm, piece=piece, layer=layer)
    n_tiles = s // tm
    cast_shapes = ((D_MODEL, D_FF), (D_MODEL, D_FF), (D_FF, D_MODEL), (D_MODEL, D_MODEL))
    cast_specs = [_row_chunk_specs(shape, b * n_tiles, layer, lambda i, j: i * n_tiles + j) for shape in cast_shapes]
    return pl.pallas_call(
        kern,
        grid=(b, n_tiles),
        in_specs=[
            pl.BlockSpec(memory_space=pltpu.SMEM),
            pl.BlockSpec((1, tm, D_MODEL), lambda i, j: (i, j, 0)),
            _const_spec((1, D_MODEL), layer),
            _const_spec((1, D_MODEL), layer),
            _const_spec((D_MODEL, IN_WIDTH), layer),
            _const_spec((D_MODEL, D_MODEL), layer),
            _const_spec((len(POOL_WINDOWS), POOL_GROUP_WIDTH, POOL_GROUP_WIDTH), layer),
            _const_spec((1, POOL_WIDTH), layer),
        ] + [spec_in for spec_in, _ in cast_specs],
        out_specs=[
            pl.BlockSpec((1, tm, D_MODEL), lambda i, j: (i, j, 0)),
            pl.BlockSpec((1, WINDOW, KV_WIDTH), lambda i, j: (i, 0, 0)),
            pl.BlockSpec((1, WINDOW, KV_WIDTH), lambda i, j: (i, 0, 0)),
            pl.BlockSpec((1, POOL_PAD, POOL_WIDTH), lambda i, j: (i, 0, 0)),
        ] + [spec_out for _, spec_out in cast_specs],
        out_shape=[
            jax.ShapeDtypeStruct((b, s, D_MODEL), F32),
            jax.ShapeDtypeStruct((b, WINDOW, KV_WIDTH), F32),
            jax.ShapeDtypeStruct((b, WINDOW, KV_WIDTH), F32),
            jax.ShapeDtypeStruct((b, POOL_PAD, POOL_WIDTH), F32),
        ] + [jax.ShapeDtypeStruct(shape, BF16) for shape in cast_shapes],
        scratch_shapes=[
            pltpu.VMEM((tm, IN_WIDTH), F32),
            pltpu.VMEM((WINDOW + tm, KV_WIDTH), F32),
            pltpu.VMEM((KV_WIDTH, WINDOW + tm), F32),
            pltpu.VMEM((POOL_HEAD + tm, POOL_WIDTH), F32),
            pltpu.VMEM((POOL_HEAD + tm, POOL_GROUP_WIDTH), F32),
            pltpu.VMEM((POOL_HEAD + tm, POOL_GROUP_WIDTH), F32),
            pltpu.VMEM((tm, D_MODEL), BF16),
            pltpu.VMEM((2, 2 * WINDOW, WINDOW), F32),
        ],
        compiler_params=pltpu.CompilerParams(
            dimension_semantics=("arbitrary", "arbitrary"), vmem_limit_bytes=VMEM_LIMIT),
        name=f"mix_prompt_{layer}",
    )(sinks, x, gpre, gpost, win, wout, wpool, pscale, wg32, wu32, wd32, wpg32)


def _swiglu(f, wg_ref, wu_ref, wd_ref):
    d, pending = None, None
    for c0, c1 in FF_CHUNKS:
        g = _dot(f, wg_ref[:, c0:c1])
        u = _dot(f, wu_ref[:, c0:c1])
        if pending is not None:
            part = _dot(pending[0], wd_ref[pending[1]:pending[2], :])
            d = part if d is None else d + part
        pending = ((g * _sigmoid(g) * u).astype(BF16), c0, c1)
    return d + _dot(pending[0], wd_ref[pending[1]:pending[2], :])


def _ffn_rows(x, p, gpre_ref, gpost_ref, wg_ref, wu_ref, wd_ref, wple_ref, wpg_ref):
    f = _rms(x, gpre_ref[...]).astype(BF16)
    x = x + _rms(_swiglu(f, wg_ref, wu_ref, wd_ref), gpost_ref[...])
    gate = _sigmoid(_dot(x.astype(BF16), wpg_ref[...]))
    return x + gate * _dot(p.astype(BF16), wple_ref[...])


def _ffn_kernel(x_ref, p_ref, gpre_ref, gpost_ref, wg_ref, wu_ref, wd_ref, wple_ref, wpg_ref, o_ref, *, nsub):
    sub = x_ref.shape[0] // nsub
    for i in range(nsub):
        rows = slice(i * sub, (i + 1) * sub)
        o_ref[rows, :] = _ffn_rows(x_ref[rows, :], p_ref[rows, :], gpre_ref, gpost_ref, wg_ref, wu_ref, wd_ref,
                                   wple_ref, wpg_ref)


def _ffn(x, p, gpre, gpost, wg, wu, wd, wple, wpg, *, layer, tm, nsub):
    n = x.shape[0]
    return pl.pallas_call(
        functools.partial(_ffn_kernel, nsub=nsub),
        grid=(n // tm,),
        in_specs=[
            pl.BlockSpec((tm, D_MODEL), lambda i: (i, 0)),
            pl.BlockSpec((None, tm, PLE_DIM), lambda i: (layer, i, 0)),
            _const_spec((1, D_MODEL), layer),
            _const_spec((1, D_MODEL), layer),
            _const_spec((D_MODEL, D_FF)),
            _const_spec((D_MODEL, D_FF)),
            _const_spec((D_FF, D_MODEL)),
            _const_spec((PLE_DIM, D_MODEL), layer),
            _const_spec((D_MODEL, D_MODEL)),
        ],
        out_specs=pl.BlockSpec((tm, D_MODEL), lambda i: (i, 0)),
        out_shape=jax.ShapeDtypeStruct((n, D_MODEL), F32),
        compiler_params=pltpu.CompilerParams(
            dimension_semantics=("arbitrary",), vmem_limit_bytes=VMEM_LIMIT),
        name=f"ffn_prompt_{layer}",
    )(x, p, gpre, gpost, wg, wu, wd, wple, wpg)


def _layer_spec(shape, single=False):
    kwargs = dict(pipeline_mode=pl.Buffered(1)) if single else {}
    return pl.BlockSpec((None,) + shape, lambda l, c: (l,) + (0,) * len(shape), **kwargs)


def _sample_kernel(sinks_ref, xs_ref, ps_ref, ck_ref, cv_ref, st_ref, gmpre_ref, gmpost_ref, gfpre_ref, gfpost_ref,
                   win_ref, wout_ref, wpool_ref, pscale_ref, wple_ref, *refs, bb, dec, nsub, depth):
    chunk_refs = refs[:4 * depth]
    (y_ref, ko_ref, vo_ref, po_ref, z_ref, mix_ref, ustage_ref, dlt_ref,
     wg_ref, wu_ref, wd_ref, wpg_ref) = refs[4 * depth:]
    l = pl.program_id(0)
    c = pl.program_id(1)
    n_rows = y_ref.shape[0]

    @pl.when((l == 0) & (c == 0))
    def _():
        y_ref[...] = xs_ref[...]

    @pl.when(c == 0)
    def _():
        h = _rms(y_ref[...], gmpre_ref[...]).astype(BF16)
        z_ref[...] = _dot(h, win_ref[...])

    for k in range(depth):
        @pl.when(l == k)
        def _(k=k):
            for src_ref, dst_ref in zip(chunk_refs[4 * k:4 * k + 4], (wg_ref, wu_ref, wd_ref, wpg_ref)):
                rows_w = src_ref.shape[0]
                dst_ref[pl.ds(pl.multiple_of(c * rows_w, 16), rows_w), :] = src_ref[...]

    per_tile = 8 // dec
    base = pl.multiple_of(c * (bb * dec), 8)
    lane = lax.broadcasted_iota(jnp.int32, (8, LANES), 1)
    lo = lane < HEAD_DIM
    hi = jnp.logical_not(lo)
    rows64 = N_HEADS * 8
    ext_w = 2 * WINDOW
    n_new = bb * dec
    new0 = ext_w - n_new
    chunk = pl.ds(base, n_new)
    tiles = range(bb // per_tile)
    trow = lax.broadcasted_iota(jnp.int32, (rows64, ext_w), 0) & 7
    tstep = trow % dec
    tbat = trow // dec
    klane = lax.broadcasted_iota(jnp.int32, (rows64, ext_w), 1)
    valid = []
    for t in tiles:
        fresh = klane - (new0 + 8 * t)
        valid.append(((klane < WINDOW) & (klane >= tstep + 1)) | (
            (fresh >= 0) & (fresh < 8) & (fresh // dec == tbat) & (fresh % dec <= tstep)))
    tbat_o = tbat[:, 0:LANES]
    wlane = lax.broadcasted_iota(jnp.int32, (KV_WIDTH, WINDOW), 1)
    zpad = jnp.zeros((WINDOW - n_new, KV_WIDTH), F32)
    knew = jnp.concatenate([zpad, z_ref[chunk, K_OFF:K_OFF + KV_WIDTH]], axis=0)
    vnew = jnp.concatenate([zpad, z_ref[chunk, V_OFF:V_OFF + KV_WIDTH]], axis=0)

    st8 = {}

    for t in tiles:
        rows = pl.ds(base + 8 * t, 8)
        q = z_ref[rows, 0:ATTN_WIDTH] * (HEAD_DIM ** -0.5)
        pieces, sinks = [], []
        for hd in range(N_HEADS):
            slab, half, kv = hd // 2, hd % 2, hd // GROUP
            qs = q[:, slab * LANES:(slab + 1) * LANES]
            src = qs if half == kv else pltpu.roll(qs, HEAD_DIM, 1)
            pieces.append(jnp.where(lo if kv == 0 else hi, src, 0.0))
            sinks.append(jnp.full((8, 1), sinks_ref[l, hd], F32))
        st8[t, "lhs"] = jnp.concatenate(pieces, axis=0).astype(BF16)
        st8[t, "sink"] = jnp.concatenate(sinks, axis=0)

    fresh_scores = _dot_t(jnp.concatenate([st8[t, "lhs"] for t in tiles], axis=0), knew.astype(BF16))
    for t in tiles:
        lhs_b = st8.pop((t, "lhs"))
        sc = None
        for i in range(per_tile):
            si = _dot(lhs_b, ck_ref[per_tile * t + i].astype(BF16))
            sc = si if sc is None else jnp.where(tbat_o == i, si, sc)
        sc = jnp.concatenate([sc, fresh_scores[rows64 * t:rows64 * (t + 1)]], axis=1)
        st8[t, "sc"] = jnp.where(valid[t], sc, -jnp.inf)

    for t in tiles:
        sc, sink = st8.pop((t, "sc")), st8.pop((t, "sink"))
        m = jnp.maximum(jnp.max(sc, axis=-1, keepdims=True), sink)
        p = jnp.exp(sc - m)
        st8[t, "denom"] = jnp.sum(p, axis=-1, keepdims=True) + jnp.exp(sink - m)
        st8[t, "pb"] = p.astype(BF16)

    fresh_out = _dot(jnp.concatenate([st8[t, "pb"][:, WINDOW:ext_w] for t in tiles], axis=0), vnew.astype(BF16))
    for t in tiles:
        pb = st8.pop((t, "pb"))
        o = None
        for i in range(per_tile):
            oi = _dot_t(pb[:, 0:WINDOW], cv_ref[per_tile * t + i].astype(BF16))
            o = oi if o is None else jnp.where(tbat_o == i, oi, o)
        st8[t, "o"] = o + fresh_out[rows64 * t:rows64 * (t + 1)]

    for t in tiles:
        rows = pl.ds(base + 8 * t, 8)
        o = st8.pop((t, "o")) / st8.pop((t, "denom"))
        for slab in range(N_HEADS // 2):
            kv = (2 * slab) // GROUP
            even = o[16 * slab:16 * slab + 8]
            odd = o[16 * slab + 8:16 * slab + 16]
            if kv == 0:
                res = jnp.where(lo, even, pltpu.roll(odd, HEAD_DIM, 1))
            else:
                res = jnp.where(lo, pltpu.roll(even, HEAD_DIM, 1), odd)
            mix_ref[rows, slab * LANES:(slab + 1) * LANES] = res

    for new, c_ref, o_ref in ((knew, ck_ref, ko_ref), (vnew, cv_ref, vo_ref)):
        cols = new.T
        for b in range(bb):
            shift = dec * (bb - 1 - b)
            mine = cols if shift == 0 else pltpu.roll(cols, shift, 1)
            kept = pltpu.roll(c_ref[b], WINDOW - dec, 1)
            o_ref[b] = jnp.where(wlane < WINDOW - dec, kept, mine)

    for r in range(POOL_STATE - dec):
        po_ref[r] = st_ref[r + dec]
    for gi, w in enumerate(POOL_WINDOWS):
        cols = slice(gi * POOL_GROUP_WIDTH, (gi + 1) * POOL_GROUP_WIDTH)
        ustage_ref[gi] = z_ref[chunk, U_OFF + gi * POOL_GROUP_WIDTH:U_OFF + (gi + 1) * POOL_GROUP_WIDTH]
        u = [ustage_ref[gi, pl.ds(j, bb, stride=dec), :] for j in range(dec)]
        suffix, acc = {}, None
        for m in range(1, w):
            row = st_ref[POOL_STATE - m, :, cols]
            acc = row if acc is None else acc + row
            suffix[m] = acc
        for j in range(dec):
            po_ref[POOL_STATE - dec + j, :, cols] = u[j]
            tot = suffix.get(w - 1 - j)
            for i in range(max(0, j - w + 1), j + 1):
                tot = u[i] if tot is None else tot + u[i]
            dlt_ref[gi, pl.ds(j, bb, stride=dec), :] = tot / float(w) - u[j]

    for gi in range(len(POOL_WINDOWS)):
        cols = slice(gi * POOL_GROUP_WIDTH, (gi + 1) * POOL_GROUP_WIDTH)
        yp = _dot(dlt_ref[gi].astype(BF16), wpool_ref[gi]) * pscale_ref[:, cols]
        mix_ref[chunk, ATTN_WIDTH + gi * POOL_GROUP_WIDTH:ATTN_WIDTH + (gi + 1) * POOL_GROUP_WIDTH] = yp

    @pl.when(c == pl.num_programs(1) - 1)
    def _():
        sub = n_rows // nsub
        for i in range(nsub):
            rs = slice(i * sub, (i + 1) * sub)
            mixed = _dot(mix_ref[rs, :].astype(BF16), wout_ref[...])
            x = y_ref[rs, :] + _rms(mixed, gmpost_ref[...])
            y_ref[rs, :] = _ffn_rows(x, ps_ref[rs, :], gfpre_ref, gfpost_ref, wg_ref, wu_ref, wd_ref, wple_ref,
                                     wpg_ref)


def _sample_path(xs, ps, sinks, ck, cv, st, gmpre, gmpost, gfpre, gfpost, win, wout, wpool, pscale,
                 wple, ff_weights, *, bb, dec, nsub):
    depth, nb = ck.shape[0], ck.shape[1]
    n = nb * dec
    kern = functools.partial(_sample_kernel, bb=bb, dec=dec, nsub=nsub, depth=depth)
    n_chunks = nb // bb

    def _chunk_spec(w, k):
        rows, cols = w.shape
        assert rows % n_chunks == 0 and (rows // n_chunks) % 16 == 0
        return pl.BlockSpec((rows // n_chunks, cols),
                            lambda l, c: (jnp.where(l == k, c, jnp.where(l > k, n_chunks - 1, 0)), 0))

    cache_spec = pl.BlockSpec((None, bb, KV_WIDTH, WINDOW), lambda l, c: (l, c, 0, 0))
    state_spec = pl.BlockSpec((None, POOL_STATE, bb, POOL_WIDTH), lambda l, c: (l, 0, c, 0))
    return pl.pallas_call(
        kern,
        grid=(depth, nb // bb),
        in_specs=[
            pl.BlockSpec(memory_space=pltpu.SMEM),
            pl.BlockSpec((n, D_MODEL), lambda l, c: (0, 0), pipeline_mode=pl.Buffered(1)),
            _layer_spec((n, PLE_DIM)),
            cache_spec, cache_spec, state_spec,
            _layer_spec((1, D_MODEL)), _layer_spec((1, D_MODEL)), _layer_spec((1, D_MODEL)), _layer_spec((1, D_MODEL)),
            _layer_spec((D_MODEL, IN_WIDTH), single=True),
            _layer_spec((D_MODEL, D_MODEL), single=True),
            _layer_spec((len(POOL_WINDOWS), POOL_GROUP_WIDTH, POOL_GROUP_WIDTH)),
            _layer_spec((1, POOL_WIDTH)),
            _layer_spec((PLE_DIM, D_MODEL), single=True),
        ] + [_chunk_spec(w, k) for k, ws in enumerate(ff_weights) for w in ws],
        out_specs=[
            pl.BlockSpec((n, D_MODEL), lambda l, c: (0, 0)),
            cache_spec, cache_spec, state_spec,
        ],
        out_shape=[
            jax.ShapeDtypeStruct((n, D_MODEL), F32),
            jax.ShapeDtypeStruct((depth, nb, KV_WIDTH, WINDOW), F32),
            jax.ShapeDtypeStruct((depth, nb, KV_WIDTH, WINDOW), F32),
            jax.ShapeDtypeStruct((depth, POOL_STATE, nb, POOL_WIDTH), F32),
        ],
        scratch_shapes=[
            pltpu.VMEM((n, IN_WIDTH), F32),
            pltpu.VMEM((n, D_MODEL), F32),
            pltpu.VMEM((len(POOL_WINDOWS), bb * dec, POOL_GROUP_WIDTH), F32),
            pltpu.VMEM((len(POOL_WINDOWS), bb * dec, POOL_GROUP_WIDTH), F32),
            pltpu.VMEM((D_MODEL, D_FF), BF16),
            pltpu.VMEM((D_MODEL, D_FF), BF16),
            pltpu.VMEM((D_FF, D_MODEL), BF16),
            pltpu.VMEM((D_MODEL, D_MODEL), BF16),
        ],
        compiler_params=pltpu.CompilerParams(
            dimension_semantics=("arbitrary", "arbitrary"), vmem_limit_bytes=VMEM_LIMIT),
        name="sample_path",
    )(sinks, xs, ps, ck, cv, st, gmpre, gmpost, gfpre, gfpost, win, wout, wpool, pscale, wple,
      *[w for ws in ff_weights for w in ws])


def _cache_from_device_layout(c):
    return c.reshape(c.shape[0], c.shape[1], N_KV_HEADS, HEAD_DIM, WINDOW).transpose(0, 1, 4, 2, 3)


def kernel(x_prompt, x_sample, p_prompt, p_sample, cache_k, cache_v, state_pool, norm_mix_pre, norm_mix_post,
           norm_ffn_pre, norm_ffn_post, w_in, w_out, attn_sinks, w_pool, pool_scale, w_gate, w_up, w_down, w_ple,
           w_ple_gate):
    depth, nbat, seq, _ = p_prompt.shape
    dec_b, dec = x_sample.shape[0], x_sample.shape[1]
    assert 8 % dec == 0 and seq % PROMPT_TILE == 0 and (nbat * seq) % FFN_TILE == 0 and dec_b % SAMPLE_BATCHES == 0
    assert cache_k.shape[2:] == (WINDOW, N_KV_HEADS, HEAD_DIM) and state_pool.shape[2:] == (POOL_STATE, POOL_WIDTH)

    win, wout, wpool, wple = w_in.astype(BF16), w_out.astype(BF16), w_pool.astype(BF16), w_ple.astype(BF16)
    g_mix_pre = norm_mix_pre.reshape(depth, 1, D_MODEL)
    g_mix_post = norm_mix_post.reshape(depth, 1, D_MODEL)
    g_ffn_pre = norm_ffn_pre.reshape(depth, 1, D_MODEL)
    g_ffn_post = norm_ffn_post.reshape(depth, 1, D_MODEL)
    pscale = pool_scale.reshape(depth, 1, POOL_WIDTH)

    pp = p_prompt.reshape(depth, nbat * seq, PLE_DIM)
    ps = p_sample.reshape(depth, dec_b * dec, PLE_DIM)
    ck = cache_k.transpose(0, 1, 3, 4, 2).reshape(depth, dec_b, KV_WIDTH, WINDOW)
    cv = cache_v.transpose(0, 1, 3, 4, 2).reshape(depth, dec_b, KV_WIDTH, WINDOW)

    yp = x_prompt
    kp_l, vp_l, sp_l, ff_weights = [], [], [], []
    for i in range(depth):
        yp, kp, vp, sp, wg, wu, wd, wpg = _mix_prompt(yp, attn_sinks, g_mix_pre, g_mix_post, win, wout, wpool, pscale,
                                                      w_gate, w_up, w_down, w_ple_gate, layer=i, tm=PROMPT_TILE,
                                                      piece=PROMPT_PIECE)
        yp = _ffn(yp.reshape(nbat * seq, D_MODEL), pp, g_ffn_pre, g_ffn_post, wg, wu, wd, wple, wpg,
                  layer=i, tm=FFN_TILE, nsub=FFN_SUBTILES).reshape(nbat, seq, D_MODEL)
        ff_weights.append((wg, wu, wd, wpg))
        kp_l.append(kp)
        vp_l.append(vp)
        sp_l.append(sp[:, POOL_PAD - POOL_STATE:])

    ys, ks, vs, ss = _sample_path(x_sample.reshape(dec_b * dec, D_MODEL), ps, attn_sinks, ck, cv,
                                  state_pool.transpose(0, 2, 1, 3),
                                  g_mix_pre, g_mix_post, g_ffn_pre, g_ffn_post, win, wout, wpool, pscale,
                                  wple, ff_weights, bb=SAMPLE_BATCHES, dec=dec, nsub=SAMPLE_SUBTILES)

    kv_shape = (depth, -1, WINDOW, N_KV_HEADS, HEAD_DIM)
    return (yp, ys.reshape(dec_b, dec, D_MODEL),
            jnp.stack(kp_l).reshape(kv_shape), jnp.stack(vp_l).reshape(kv_shape), jnp.stack(sp_l),
            _cache_from_device_layout(ks), _cache_from_device_layout(vs), ss.transpose(0, 2, 1, 3))
```

```python
import functools

import jax
import jax.numpy as jnp
from jax import lax
from jax.experimental import pallas as pl
from jax.experimental.pallas import tpu as pltpu

D_MODEL = 1024
DEPTH = 4
ATTN_WIDTH = 512
HEAD_DIM = 64
N_HEADS = 8
N_KV_HEADS = 2
GROUP = 4
KV_WIDTH = 128
WINDOW = 128
POOL_WIDTH = 512
POOL_WINDOWS = (2, 4, 8, 16)
POOL_GROUP_WIDTH = 128
POOL_STATE = 15
IN_WIDTH = 1280
D_FF = 2816
PLE_DIM = 256
EPS = 1e-6

K_OFF = ATTN_WIDTH
V_OFF = ATTN_WIDTH + KV_WIDTH
U_OFF = ATTN_WIDTH + 2 * KV_WIDTH
LANES = 128
POOL_PAD = 16
POOL_HEAD = 8 + POOL_PAD
LOG2E = 1.4426950408889634
FF_CHUNKS = ((0, 1024), (1024, 2048), (2048, D_FF))

BF16 = jnp.bfloat16
F32 = jnp.float32
V7X_VMEM_BYTES = 64 * 1024 * 1024
VMEM_LIMIT = V7X_VMEM_BYTES - 8 * 1024 * 1024

PROMPT_TILE = 1024
PROMPT_PIECE = 512
ATTN_GROUP = 4
FFN_TILE = 1024
FFN_SUBTILES = 2
SAMPLE_BATCHES = 8
SAMPLE_SUBTILES = 1


def _rms(x, g):
    return x * lax.rsqrt(jnp.mean(x * x, axis=-1, keepdims=True) + EPS) * g


def _sigmoid(x):
    return 1.0 / (1.0 + jnp.exp(-x))


def _dot(a, b):
    return jnp.dot(a, b, preferred_element_type=F32)


def _dot_t(a, b):
    return lax.dot_general(a, b, (((1,), (1,)), ((), ())), preferred_element_type=F32)


def _const_spec(shape, layer=None):
    if layer is None:
        return pl.BlockSpec(shape, lambda *_: (0,) * len(shape), pipeline_mode=pl.Buffered(1))
    return pl.BlockSpec((None,) + shape, lambda *_: (layer,) + (0,) * len(shape),
                        pipeline_mode=pl.Buffered(1))


def _mix_prompt_kernel(sinks_ref, x_ref, gpre_ref, gpost_ref, win_ref, wout_ref, wpool_ref, pscale_ref,
                       wg32_ref, wu32_ref, wd32_ref, wpg32_ref,
                       y_ref, klast_ref, vlast_ref, plast_ref, wg16_ref, wu16_ref, wd16_ref, wpg16_ref,
                       z_ref, kext_ref, vt_ref, uext_ref, ps1_ref, ps2_ref, mix_ref, bias_ref, *, tm, piece, layer):
    s = pl.program_id(1)
    nblk = tm // WINDOW

    def cast_weights():
        for w32_ref, w16_ref in ((wg32_ref, wg16_ref), (wu32_ref, wu16_ref), (wd32_ref, wd16_ref),
                                 (wpg32_ref, wpg16_ref)):
            w16_ref[...] = w32_ref[...].astype(BF16)

    @pl.when((pl.program_id(0) == 0) & (s == 0))
    def _():
        c = lax.broadcasted_iota(jnp.int32, (2 * WINDOW, WINDOW), 0)
        r = lax.broadcasted_iota(jnp.int32, (2 * WINDOW, WINDOW), 1)
        dist = r + WINDOW - c
        ok = (dist >= 0) & (dist < WINDOW)
        bias_ref[0] = jnp.where(ok, 0.0, -jnp.inf)
        bias_ref[1] = jnp.where(ok & (c >= WINDOW), 0.0, -jnp.inf)

    @pl.when(s == 0)
    def _():
        kext_ref[0:WINDOW, :] = jnp.zeros((WINDOW, KV_WIDTH), F32)
        vt_ref[:, 0:WINDOW] = jnp.zeros((KV_WIDTH, WINDOW), F32)
        uext_ref[0:POOL_HEAD, :] = jnp.zeros((POOL_HEAD, POOL_WIDTH), F32)
        ps1_ref[0:8, :] = jnp.zeros((8, POOL_GROUP_WIDTH), F32)
        ps2_ref[0:8, :] = jnp.zeros((8, POOL_GROUP_WIDTH), F32)

    lane = lax.broadcasted_iota(jnp.int32, (WINDOW, LANES), 1)
    lo = lane < HEAD_DIM
    lo2 = jnp.concatenate([lo, lo], axis=0)
    row8 = lax.broadcasted_iota(jnp.int32, (8, GROUP * WINDOW), 0)
    kcol = lax.broadcasted_iota(jnp.int32, (HEAD_DIM, 2 * WINDOW), 1)
    ones = jnp.ones((HEAD_DIM, 2 * WINDOW), F32)
    half = piece

    def project_in(hf):
        hr = slice(hf * half, (hf + 1) * half)
        h = _rms(x_ref[0, hr, :], gpre_ref[...]).astype(BF16)
        z_ref[hr, :] = _dot(h, win_ref[...])
        uext_ref[POOL_HEAD + hf * half:POOL_HEAD + (hf + 1) * half, :] = z_ref[hr, U_OFF:U_OFF + POOL_WIDTH]
        kext_ref[WINDOW + hf * half:WINDOW + (hf + 1) * half, :] = z_ref[hr, K_OFF:K_OFF + KV_WIDTH]
        for n in range(hf * half // WINDOW, (hf + 1) * half // WINDOW):
            r0 = n * WINDOW
            vt_ref[:, WINDOW + r0:2 * WINDOW + r0] = z_ref[r0:r0 + WINDOW, V_OFF:V_OFF + KV_WIDTH].T

    def attend_stages(blocks):
        st8 = {}

        def scores():
            for n in blocks:
                r0 = n * WINDOW
                rows = slice(r0, r0 + WINDOW)
                kcat = kext_ref[r0:r0 + 2 * WINDOW, :]
                kswap = pltpu.roll(kcat, HEAD_DIM, 1)
                bias1 = bias_ref[jnp.where(s == 0, 1, 0)] if n == 0 else bias_ref[0]
                bias = jnp.concatenate([bias1] * GROUP, axis=1)
                for kv in range(N_KV_HEADS):
                    kk = (jnp.where(lo2, kcat, kswap) if kv == 0 else jnp.where(lo2, kswap, kcat)).astype(BF16)
                    pieces, sinks = [], []
                    for g in range(GROUP):
                        hd = kv * GROUP + g
                        slab = hd // 2
                        qs = z_ref[rows, slab * LANES:(slab + 1) * LANES] * (HEAD_DIM ** -0.5 * LOG2E)
                        keep = lo if hd % 2 == 0 else jnp.logical_not(lo)
                        pieces.append(jnp.where(keep, qs, 0.0).astype(BF16))
                        sinks.append(jnp.full((1, WINDOW), sinks_ref[layer, hd] * LOG2E, F32))
                    q4 = jnp.concatenate(pieces, axis=0)
                    st8[n, kv, "sink"] = jnp.concatenate(sinks, axis=1)
                    st8[n, kv, "st"] = _dot_t(kk, q4) + bias

        def softmax():
            for n in blocks:
                for kv in range(N_KV_HEADS):
                    st, sink = st8.pop((n, kv, "st")), st8.pop((n, kv, "sink"))
                    m = jnp.maximum(jnp.max(st, axis=0, keepdims=True), sink)
                    p = jnp.exp2(st - m)
                    top = jnp.where(row8 == 0, jnp.exp2(sink - m), p[0:8])
                    st8[n, kv, "p"] = jnp.concatenate([top, p[8:]], axis=0).astype(BF16)

        def values():
            for n in blocks:
                r0 = n * WINDOW
                for kv in range(N_KV_HEADS):
                    vth = vt_ref[kv * HEAD_DIM:(kv + 1) * HEAD_DIM, r0:r0 + 2 * WINDOW]
                    a = jnp.concatenate([jnp.where(kcol == 0, 0.0, vth), ones], axis=0).astype(BF16)
                    st8[n, kv, "ot"] = _dot(a, st8.pop((n, kv, "p")))

        def store():
            for n in blocks:
                rows = slice(n * WINDOW, (n + 1) * WINDOW)
                for kv in range(N_KV_HEADS):
                    ot = st8.pop((n, kv, "ot"))
                    inv = 1.0 / ot[HEAD_DIM:HEAD_DIM + 8]
                    on = ot[0:HEAD_DIM] * jnp.concatenate([inv] * (HEAD_DIM // 8), axis=0)
                    for j in range(GROUP // 2):
                        c0 = 2 * j * WINDOW
                        pair = jnp.concatenate([on[:, c0:c0 + WINDOW], on[:, c0 + WINDOW:c0 + 2 * WINDOW]], axis=0)
                        slab = kv * (GROUP // 2) + j
                        mix_ref[rows, slab * LANES:(slab + 1) * LANES] = pair.T.astype(BF16)

        return scores, softmax, values, store

    def pool(hf):
        t0 = hf * half
        pos = lax.broadcasted_iota(jnp.int32, (half, 1), 0) + (s * tm + t0)
        first = POOL_HEAD + t0
        ext = slice(first - POOL_PAD, first + half)
        for gi, w in enumerate(POOL_WINDOWS):
            cols = slice(gi * POOL_GROUP_WIDTH, (gi + 1) * POOL_GROUP_WIDTH)
            src, src_cols, span, bufs = uext_ref, cols, 1, [ps1_ref, ps2_ref]
            while 2 * span < w:
                dst = bufs[0]
                dst[ext, :] = src[ext, src_cols] + src[first - POOL_PAD - span:first + half - span, src_cols]
                src, src_cols, span, bufs = dst, slice(None), 2 * span, bufs[::-1]
            tot = src[first:first + half, src_cols] + src[first - span:first - span + half, src_cols]
            cnt = jnp.minimum(w, pos + 1).astype(F32)
            dlt = (tot / cnt - uext_ref[first:first + half, cols]).astype(BF16)
            yp = _dot(dlt, wpool_ref[gi]) * pscale_ref[:, cols]
            mix_ref[t0:t0 + half, ATTN_WIDTH + gi * POOL_GROUP_WIDTH:ATTN_WIDTH + (gi + 1) * POOL_GROUP_WIDTH] = (
                yp.astype(BF16))

    def project_out(hf):
        hr = slice(hf * half, (hf + 1) * half)
        mixed = _dot(mix_ref[hr, :], wout_ref[...])
        y_ref[0, hr, :] = x_ref[0, hr, :] + _rms(mixed, gpost_ref[...])

    per_half = half // WINDOW
    n_piece = tm // half
    project_in(0)
    for hf in range(n_piece):
        blocks = list(range(hf * per_half, (hf + 1) * per_half))
        groups = [blocks[i:i + ATTN_GROUP] for i in range(0, len(blocks), ATTN_GROUP)]
        for gi, group in enumerate(groups):
            scores, softmax, values, store = attend_stages(group)
            scores()
            if gi == 0 and hf + 1 < n_piece:
                project_in(hf + 1)
            if gi == 0 and hf == 0:
                cast_weights()
            if gi == len(groups) - 1 and hf > 0:
                project_out(hf - 1)
            softmax()
            values()
            store()
        pool(hf)
    project_out(n_piece - 1)

    kext_ref[0:WINDOW, :] = kext_ref[tm:tm + WINDOW, :]
    vt_ref[:, 0:WINDOW] = vt_ref[:, tm:tm + WINDOW]
    uext_ref[POOL_HEAD - POOL_PAD:POOL_HEAD, :] = uext_ref[POOL_HEAD + tm - POOL_PAD:POOL_HEAD + tm, :]

    @pl.when(s == pl.num_programs(1) - 1)
    def _():
        klast_ref[0] = z_ref[tm - WINDOW:tm, K_OFF:K_OFF + KV_WIDTH]
        vlast_ref[0] = z_ref[tm - WINDOW:tm, V_OFF:V_OFF + KV_WIDTH]
        plast_ref[0] = uext_ref[POOL_HEAD + tm - POOL_PAD:POOL_HEAD + tm, :]


def _row_chunk_specs(shape, n_steps, layer, step_of):
    rows, cols = shape
    chunk = rows // n_steps
    assert chunk * n_steps == rows and chunk % 16 == 0, (shape, n_steps)
    return (pl.BlockSpec((None, chunk, cols), lambda *g: (layer, step_of(*g), 0)),
            pl.BlockSpec((chunk, cols), lambda *g: (step_of(*g), 0)))


def _mix_prompt(x, sinks, gpre, gpost, win, wout, wpool, pscale, wg32, wu32, wd32, wpg32, *, layer, tm, piece):
    b, s, _ = x.shape
    kern = functools.partial(_mix_prompt_kernel, tm=tm, piece=piece, layer=layer)
    n_tiles = s // tm
    cast_shapes = ((D_MODEL, D_FF), (D_MODEL, D_FF), (D_FF, D_MODEL), (D_MODEL, D_MODEL))
    cast_specs = [_row_chunk_specs(shape, b * n_tiles, layer, lambda i, j: i * n_tiles + j) for shape in cast_shapes]
    return pl.pallas_call(
        kern,
        grid=(b, n_tiles),
        in_specs=[
            pl.BlockSpec(memory_space=pltpu.SMEM),
            pl.BlockSpec((1, tm, D_MODEL), lambda i, j: (i, j, 0)),
            _const_spec((1, D_MODEL), layer),
            _const_spec((1, D_MODEL), layer),
            _const_spec((D_MODEL, IN_WIDTH), layer),
            _const_spec((D_MODEL, D_MODEL), layer),
            _const_spec((len(POOL_WINDOWS), POOL_GROUP_WIDTH, POOL_GROUP_WIDTH), layer),
            _const_spec((1, POOL_WIDTH), layer),
        ] + [spec_in for spec_in, _ in cast_specs],
        out_specs=[
            pl.BlockSpec((1, tm, D_MODEL), lambda i, j: (i, j, 0)),
            pl.BlockSpec((1, WINDOW, KV_WIDTH), lambda i, j: (i, 0, 0)),
            pl.BlockSpec((1, WINDOW, KV_WIDTH), lambda i, j: (i, 0, 0)),
            pl.BlockSpec((1, POOL_PAD, POOL_WIDTH), lambda i, j: (i, 0, 0)),
        ] + [spec_out for _, spec_out in cast_specs],
        out_shape=[
            jax.ShapeDtypeStruct((b, s, D_MODEL), F32),
            jax.ShapeDtypeStruct((b, WINDOW, KV_WIDTH), F32),
            jax.ShapeDtypeStruct((b, WINDOW, KV_WIDTH), F32),
            jax.ShapeDtypeStruct((b, POOL_PAD, POOL_WIDTH), F32),
        ] + [jax.ShapeDtypeStruct(shape, BF16) for shape in cast_shapes],
        scratch_shapes=[
            pltpu.VMEM((tm, IN_WIDTH), F32),
            pltpu.VMEM((WINDOW + tm, KV_WIDTH), F32),
            pltpu.VMEM((KV_WIDTH, WINDOW + tm), F32),
            pltpu.VMEM((POOL_HEAD + tm, POOL_WIDTH), F32),
            pltpu.VMEM((POOL_HEAD + tm, POOL_GROUP_WIDTH), F32),
            pltpu.VMEM((POOL_HEAD + tm, POOL_GROUP_WIDTH), F32),
            pltpu.VMEM((tm, D_MODEL), BF16),
            pltpu.VMEM((2, 2 * WINDOW, WINDOW), F32),
        ],
        compiler_params=pltpu.CompilerParams(
            dimension_semantics=("arbitrary", "arbitrary"), vmem_limit_bytes=VMEM_LIMIT),
        name=f"mix_prompt_{layer}",
    )(sinks, x, gpre, gpost, win, wout, wpool, pscale, wg32, wu32, wd32, wpg32)


def _swiglu(f, wg_ref, wu_ref, wd_ref):
    d, pending = None, None
    for c0, c1 in FF_CHUNKS:
        g = _dot(f, wg_ref[:, c0:c1])
        u = _dot(f, wu_ref[:, c0:c1])
        if pending is not None:
            part = _dot(pending[0], wd_ref[pending[1]:pending[2], :])
            d = part if d is None else d + part
        pending = ((g * _sigmoid(g) * u).astype(BF16), c0, c1)
    return d + _dot(pending[0], wd_ref[pending[1]:pending[2], :])


def _ffn_rows(x, p, gpre_ref, gpost_ref, wg_ref, wu_ref, wd_ref, wple_ref, wpg_ref):
    f = _rms(x, gpre_ref[...]).astype(BF16)
    x = x + _rms(_swiglu(f, wg_ref, wu_ref, wd_ref), gpost_ref[...])
    gate = _sigmoid(_dot(x.astype(BF16), wpg_ref[...]))
    return x + gate * _dot(p.astype(BF16), wple_ref[...])


def _ffn_kernel(x_ref, p_ref, gpre_ref, gpost_ref, wg_ref, wu_ref, wd_ref, wple_ref, wpg_ref, o_ref, *, nsub):
    sub = x_ref.shape[0] // nsub
    for i in range(nsub):
        rows = slice(i * sub, (i + 1) * sub)
        o_ref[rows, :] = _ffn_rows(x_ref[rows, :], p_ref[rows, :], gpre_ref, gpost_ref, wg_ref, wu_ref, wd_ref,
                                   wple_ref, wpg_ref)


def _ffn(x, p, gpre, gpost, wg, wu, wd, wple, wpg, *, layer, tm, nsub):
    n = x.shape[0]
    return pl.pallas_call(
        functools.partial(_ffn_kernel, nsub=nsub),
        grid=(n // tm,),
        in_specs=[
            pl.BlockSpec((tm, D_MODEL), lambda i: (i, 0)),
            pl.BlockSpec((None, tm, PLE_DIM), lambda i: (layer, i, 0)),
            _const_spec((1, D_MODEL), layer),
            _const_spec((1, D_MODEL), layer),
            _const_spec((D_MODEL, D_FF)),
            _const_spec((D_MODEL, D_FF)),
            _const_spec((D_FF, D_MODEL)),
            _const_spec((PLE_DIM, D_MODEL), layer),
            _const_spec((D_MODEL, D_MODEL)),
        ],
        out_specs=pl.BlockSpec((tm, D_MODEL), lambda i: (i, 0)),
        out_shape=jax.ShapeDtypeStruct((n, D_MODEL), F32),
        compiler_params=pltpu.CompilerParams(
            dimension_semantics=("arbitrary",), vmem_limit_bytes=VMEM_LIMIT),
        name=f"ffn_prompt_{layer}",
    )(x, p, gpre, gpost, wg, wu, wd, wple, wpg)


def _layer_spec(shape, single=False):
    kwargs = dict(pipeline_mode=pl.Buffered(1)) if single else {}
    return pl.BlockSpec((None,) + shape, lambda l, c: (l,) + (0,) * len(shape), **kwargs)


def _sample_kernel(sinks_ref, xs_ref, ps_ref, ck_ref, cv_ref, st_ref, gmpre_ref, gmpost_ref, gfpre_ref, gfpost_ref,
                   win_ref, wout_ref, wpool_ref, pscale_ref, wple_ref, *refs, bb, dec, nsub, depth):
    chunk_refs = refs[:4 * depth]
    (y_ref, ko_ref, vo_ref, po_ref, z_ref, mix_ref, ustage_ref, dlt_ref,
     wg_ref, wu_ref, wd_ref, wpg_ref) = refs[4 * depth:]
    l = pl.program_id(0)
    c = pl.program_id(1)
    n_rows = y_ref.shape[0]

    @pl.when((l == 0) & (c == 0))
    def _():
        y_ref[...] = xs_ref[...]

    @pl.when(c == 0)
    def _():
        h = _rms(y_ref[...], gmpre_ref[...]).astype(BF16)
        z_ref[...] = _dot(h, win_ref[...])

    for k in range(depth):
        @pl.when(l == k)
        def _(k=k):
            for src_ref, dst_ref in zip(chunk_refs[4 * k:4 * k + 4], (wg_ref, wu_ref, wd_ref, wpg_ref)):
                rows_w = src_ref.shape[0]
                dst_ref[pl.ds(pl.multiple_of(c * rows_w, 16), rows_w), :] = src_ref[...]

    per_tile = 8 // dec
    base = pl.multiple_of(c * (bb * dec), 8)
    lane = lax.broadcasted_iota(jnp.int32, (8, LANES), 1)
    lo = lane < HEAD_DIM
    hi = jnp.logical_not(lo)
    rows64 = N_HEADS * 8
    ext_w = 2 * WINDOW
    n_new = bb * dec
    new0 = ext_w - n_new
    chunk = pl.ds(base, n_new)
    tiles = range(bb // per_tile)
    trow = lax.broadcasted_iota(jnp.int32, (rows64, ext_w), 0) & 7
    tstep = trow % dec
    tbat = trow // dec
    klane = lax.broadcasted_iota(jnp.int32, (rows64, ext_w), 1)
    valid = []
    for t in tiles:
        fresh = klane - (new0 + 8 * t)
        valid.append(((klane < WINDOW) & (klane >= tstep + 1)) | (
            (fresh >= 0) & (fresh < 8) & (fresh // dec == tbat) & (fresh % dec <= tstep)))
    tbat_o = tbat[:, 0:LANES]
    wlane = lax.broadcasted_iota(jnp.int32, (KV_WIDTH, WINDOW), 1)
    zpad = jnp.zeros((WINDOW - n_new, KV_WIDTH), F32)
    knew = jnp.concatenate([zpad, z_ref[chunk, K_OFF:K_OFF + KV_WIDTH]], axis=0)
    vnew = jnp.concatenate([zpad, z_ref[chunk, V_OFF:V_OFF + KV_WIDTH]], axis=0)

    st8 = {}

    for t in tiles:
        rows = pl.ds(base + 8 * t, 8)
        q = z_ref[rows, 0:ATTN_WIDTH] * (HEAD_DIM ** -0.5)
        pieces, sinks = [], []
        for hd in range(N_HEADS):
            slab, half, kv = hd // 2, hd % 2, hd // GROUP
            qs = q[:, slab * LANES:(slab + 1) * LANES]
            src = qs if half == kv else pltpu.roll(qs, HEAD_DIM, 1)
            pieces.append(jnp.where(lo if kv == 0 else hi, src, 0.0))
            sinks.append(jnp.full((8, 1), sinks_ref[l, hd], F32))
        st8[t, "lhs"] = jnp.concatenate(pieces, axis=0).astype(BF16)
        st8[t, "sink"] = jnp.concatenate(sinks, axis=0)

    fresh_scores = _dot_t(jnp.concatenate([st8[t, "lhs"] for t in tiles], axis=0), knew.astype(BF16))
    for t in tiles:
        lhs_b = st8.pop((t, "lhs"))
        sc = None
        for i in range(per_tile):
            si = _dot(lhs_b, ck_ref[per_tile * t + i].astype(BF16))
            sc = si if sc is None else jnp.where(tbat_o == i, si, sc)
        sc = jnp.concatenate([sc, fresh_scores[rows64 * t:rows64 * (t + 1)]], axis=1)
        st8[t, "sc"] = jnp.where(valid[t], sc, -jnp.inf)

    for t in tiles:
        sc, sink = st8.pop((t, "sc")), st8.pop((t, "sink"))
        m = jnp.maximum(jnp.max(sc, axis=-1, keepdims=True), sink)
        p = jnp.exp(sc - m)
        st8[t, "denom"] = jnp.sum(p, axis=-1, keepdims=True) + jnp.exp(sink - m)
        st8[t, "pb"] = p.astype(BF16)

    fresh_out = _dot(jnp.concatenate([st8[t, "pb"][:, WINDOW:ext_w] for t in tiles], axis=0), vnew.astype(BF16))
    for t in tiles:
        pb = st8.pop((t, "pb"))
        o = None
        for i in range(per_tile):
            oi = _dot_t(pb[:, 0:WINDOW], cv_ref[per_tile * t + i].astype(BF16))
            o = oi if o is None else jnp.where(tbat_o == i, oi, o)
        st8[t, "o"] = o + fresh_out[rows64 * t:rows64 * (t + 1)]

    for t in tiles:
        rows = pl.ds(base + 8 * t, 8)
        o = st8.pop((t, "o")) / st8.pop((t, "denom"))
        for slab in range(N_HEADS // 2):
            kv = (2 * slab) // GROUP
            even = o[16 * slab:16 * slab + 8]
            odd = o[16 * slab + 8:16 * slab + 16]
            if kv == 0:
                res = jnp.where(lo, even, pltpu.roll(odd, HEAD_DIM, 1))
            else:
                res = jnp.where(lo, pltpu.roll(even, HEAD_DIM, 1), odd)
            mix_ref[rows, slab * LANES:(slab + 1) * LANES] = res

    for new, c_ref, o_ref in ((knew, ck_ref, ko_ref), (vnew, cv_ref, vo_ref)):
        cols = new.T
        for b in range(bb):
            shift = dec * (bb - 1 - b)
            mine = cols if shift == 0 else pltpu.roll(cols, shift, 1)
            kept = pltpu.roll(c_ref[b], WINDOW - dec, 1)
            o_ref[b] = jnp.where(wlane < WINDOW - dec, kept, mine)

    for r in range(POOL_STATE - dec):
        po_ref[r] = st_ref[r + dec]
    for gi, w in enumerate(POOL_WINDOWS):
        cols = slice(gi * POOL_GROUP_WIDTH, (gi + 1) * POOL_GROUP_WIDTH)
        ustage_ref[gi] = z_ref[chunk, U_OFF + gi * POOL_GROUP_WIDTH:U_OFF + (gi + 1) * POOL_GROUP_WIDTH]
        u = [ustage_ref[gi, pl.ds(j, bb, stride=dec), :] for j in range(dec)]
        suffix, acc = {}, None
        for m in range(1, w):
            row = st_ref[POOL_STATE - m, :, cols]
            acc = row if acc is None else acc + row
            suffix[m] = acc
        for j in range(dec):
            po_ref[POOL_STATE - dec + j, :, cols] = u[j]
            tot = suffix.get(w - 1 - j)
            for i in range(max(0, j - w + 1), j + 1):
                tot = u[i] if tot is None else tot + u[i]
            dlt_ref[gi, pl.ds(j, bb, stride=dec), :] = tot / float(w) - u[j]

    for gi in range(len(POOL_WINDOWS)):
        cols = slice(gi * POOL_GROUP_WIDTH, (gi + 1) * POOL_GROUP_WIDTH)
        yp = _dot(dlt_ref[gi].astype(BF16), wpool_ref[gi]) * pscale_ref[:, cols]
        mix_ref[chunk, ATTN_WIDTH + gi * POOL_GROUP_WIDTH:ATTN_WIDTH + (gi + 1) * POOL_GROUP_WIDTH] = yp

    @pl.when(c == pl.num_programs(1) - 1)
    def _():
        sub = n_rows // nsub
        for i in range(nsub):
            rs = slice(i * sub, (i + 1) * sub)
            mixed = _dot(mix_ref[rs, :].astype(BF16), wout_ref[...])
            x = y_ref[rs, :] + _rms(mixed, gmpost_ref[...])
            y_ref[rs, :] = _ffn_rows(x, ps_ref[rs, :], gfpre_ref, gfpost_ref, wg_ref, wu_ref, wd_ref, wple_ref,
                                     wpg_ref)


def _sample_path(xs, ps, sinks, ck, cv, st, gmpre, gmpost, gfpre, gfpost, win, wout, wpool, pscale,
                 wple, ff_weights, *, bb, dec, nsub):
    depth, nb = ck.shape[0], ck.shape[1]
    n = nb * dec
    kern = functools.partial(_sample_kernel, bb=bb, dec=dec, nsub=nsub, depth=depth)
    n_chunks = nb // bb

    def _chunk_spec(w, k):
        rows, cols = w.shape
        assert rows % n_chunks == 0 and (rows // n_chunks) % 16 == 0
        return pl.BlockSpec((rows // n_chunks, cols),
                            lambda l, c: (jnp.where(l == k, c, jnp.where(l > k, n_chunks - 1, 0)), 0))

    cache_spec = pl.BlockSpec((None, bb, KV_WIDTH, WINDOW), lambda l, c: (l, c, 0, 0))
    state_spec = pl.BlockSpec((None, POOL_STATE, bb, POOL_WIDTH), lambda l, c: (l, 0, c, 0))
    return pl.pallas_call(
        kern,
        grid=(depth, nb // bb),
        in_specs=[
            pl.BlockSpec(memory_space=pltpu.SMEM),
            pl.BlockSpec((n, D_MODEL), lambda l, c: (0, 0), pipeline_mode=pl.Buffered(1)),
            _layer_spec((n, PLE_DIM)),
            cache_spec, cache_spec, state_spec,
            _layer_spec((1, D_MODEL)), _layer_spec((1, D_MODEL)), _layer_spec((1, D_MODEL)), _layer_spec((1, D_MODEL)),
            _layer_spec((D_MODEL, IN_WIDTH), single=True),
            _layer_spec((D_MODEL, D_MODEL), single=True),
            _layer_spec((len(POOL_WINDOWS), POOL_GROUP_WIDTH, POOL_GROUP_WIDTH)),
            _layer_spec((1, POOL_WIDTH)),
            _layer_spec((PLE_DIM, D_MODEL), single=True),
        ] + [_chunk_spec(w, k) for k, ws in enumerate(ff_weights) for w in ws],
        out_specs=[
            pl.BlockSpec((n, D_MODEL), lambda l, c: (0, 0)),
            cache_spec, cache_spec, state_spec,
        ],
        out_shape=[
            jax.ShapeDtypeStruct((n, D_MODEL), F32),
            jax.ShapeDtypeStruct((depth, nb, KV_WIDTH, WINDOW), F32),
            jax.ShapeDtypeStruct((depth, nb, KV_WIDTH, WINDOW), F32),
            jax.ShapeDtypeStruct((depth, POOL_STATE, nb, POOL_WIDTH), F32),
        ],
        scratch_shapes=[
            pltpu.VMEM((n, IN_WIDTH), F32),
            pltpu.VMEM((n, D_MODEL), F32),
            pltpu.VMEM((len(POOL_WINDOWS), bb * dec, POOL_GROUP_WIDTH), F32),
            pltpu.VMEM((len(POOL_WINDOWS), bb * dec, POOL_GROUP_WIDTH), F32),
            pltpu.VMEM((D_MODEL, D_FF), BF16),
            pltpu.VMEM((D_MODEL, D_FF), BF16),
            pltpu.VMEM((D_FF, D_MODEL), BF16),
            pltpu.VMEM((D_MODEL, D_MODEL), BF16),
        ],
        compiler_params=pltpu.CompilerParams(
            dimension_semantics=("arbitrary", "arbitrary"), vmem_limit_bytes=VMEM_LIMIT),
        name="sample_path",
    )(sinks, xs, ps, ck, cv, st, gmpre, gmpost, gfpre, gfpost, win, wout, wpool, pscale, wple,
      *[w for ws in ff_weights for w in ws])


def _cache_from_device_layout(c):
    return c.reshape(c.shape[0], c.shape[1], N_KV_HEADS, HEAD_DIM, WINDOW).transpose(0, 1, 4, 2, 3)


def kernel(x_prompt, x_sample, p_prompt, p_sample, cache_k, cache_v, state_pool, norm_mix_pre, norm_mix_post,
           norm_ffn_pre, norm_ffn_post, w_in, w_out, attn_sinks, w_pool, pool_scale, w_gate, w_up, w_down, w_ple,
           w_ple_gate):
    depth, nbat, seq, _ = p_prompt.shape
    dec_b, dec = x_sample.shape[0], x_sample.shape[1]
    assert 8 % dec == 0 and seq % PROMPT_TILE == 0 and (nbat * seq) % FFN_TILE == 0 and dec_b % SAMPLE_BATCHES == 0
    assert cache_k.shape[2:] == (WINDOW, N_KV_HEADS, HEAD_DIM) and state_pool.shape[2:] == (POOL_STATE, POOL_WIDTH)

    win, wout, wpool, wple = w_in.astype(BF16), w_out.astype(BF16), w_pool.astype(BF16), w_ple.astype(BF16)
    g_mix_pre = norm_mix_pre.reshape(depth, 1, D_MODEL)
    g_mix_post = norm_mix_post.reshape(depth, 1, D_MODEL)
    g_ffn_pre = norm_ffn_pre.reshape(depth, 1, D_MODEL)
    g_ffn_post = norm_ffn_post.reshape(depth, 1, D_MODEL)
    pscale = pool_scale.reshape(depth, 1, POOL_WIDTH)

    pp = p_prompt.reshape(depth, nbat * seq, PLE_DIM)
    ps = p_sample.reshape(depth, dec_b * dec, PLE_DIM)
    ck = cache_k.transpose(0, 1, 3, 4, 2).reshape(depth, dec_b, KV_WIDTH, WINDOW)
    cv = cache_v.transpose(0, 1, 3, 4, 2).reshape(depth, dec_b, KV_WIDTH, WINDOW)

    yp = x_prompt
    kp_l, vp_l, sp_l, ff_weights = [], [], [], []
    for i in range(depth):
        yp, kp, vp, sp, wg, wu, wd, wpg = _mix_prompt(yp, attn_sinks, g_mix_pre, g_mix_post, win, wout, wpool, pscale,
                                                      w_gate, w_up, w_down, w_ple_gate, layer=i, tm=PROMPT_TILE,
                                                      piece=PROMPT_PIECE)
        yp = _ffn(yp.reshape(nbat * seq, D_MODEL), pp, g_ffn_pre, g_ffn_post, wg, wu, wd, wple, wpg,
                  layer=i, tm=FFN_TILE, nsub=FFN_SUBTILES).reshape(nbat, seq, D_MODEL)
        ff_weights.append((wg, wu, wd, wpg))
        kp_l.append(kp)
        vp_l.append(vp)
        sp_l.append(sp[:, POOL_PAD - POOL_STATE:])

    ys, ks, vs, ss = _sample_path(x_sample.reshape(dec_b * dec, D_MODEL), ps, attn_sinks, ck, cv,
                                  state_pool.transpose(0, 2, 1, 3),
                                  g_mix_pre, g_mix_post, g_ffn_pre, g_ffn_post, win, wout, wpool, pscale,
                                  wple, ff_weights, bb=SAMPLE_BATCHES, dec=dec, nsub=SAMPLE_SUBTILES)

    kv_shape = (depth, -1, WINDOW, N_KV_HEADS, HEAD_DIM)
    return (yp, ys.reshape(dec_b, dec, D_MODEL),
            jnp.stack(kp_l).reshape(kv_shape), jnp.stack(vp_l).reshape(kv_shape), jnp.stack(sp_l),
            _cache_from_device_layout(ks), _cache_from_device_layout(vs), ss.transpose(0, 2, 1, 3))
```

```python
import functools

import jax
import jax.numpy as jnp
from jax import lax
from jax.experimental import pallas as pl
from jax.experimental.pallas import tpu as pltpu

D_MODEL = 1024
DEPTH = 4
ATTN_WIDTH = 512
HEAD_DIM = 64
N_HEADS = 8
N_KV_HEADS = 2
GROUP = 4
KV_WIDTH = 128
WINDOW = 128
POOL_WIDTH = 512
POOL_WINDOWS = (2, 4, 8, 16)
POOL_GROUP_WIDTH = 128
POOL_STATE = 15
IN_WIDTH = 1280
D_FF = 2816
PLE_DIM = 256
EPS = 1e-6

K_OFF = ATTN_WIDTH
V_OFF = ATTN_WIDTH + KV_WIDTH
U_OFF = ATTN_WIDTH + 2 * KV_WIDTH
LANES = 128
POOL_PAD = 16
POOL_HEAD = 8 + POOL_PAD
LOG2E = 1.4426950408889634
FF_CHUNKS = ((0, 1024), (1024, 2048), (2048, D_FF))

BF16 = jnp.bfloat16
F32 = jnp.float32
V7X_VMEM_BYTES = 64 * 1024 * 1024
VMEM_LIMIT = V7X_VMEM_BYTES - 8 * 1024 * 1024

PROMPT_TILE = 1024
PROMPT_PIECE = 512
ATTN_GROUP = 4
FFN_TILE = 1024
FFN_SUBTILES = 2
SAMPLE_BATCHES = 8
SAMPLE_SUBTILES = 1


def _rms(x, g):
    return x * lax.rsqrt(jnp.mean(x * x, axis=-1, keepdims=True) + EPS) * g


def _sigmoid(x):
    return 1.0 / (1.0 + jnp.exp(-x))


def _dot(a, b):
    return jnp.dot(a, b, preferred_element_type=F32)


def _dot_t(a, b):
    return lax.dot_general(a, b, (((1,), (1,)), ((), ())), preferred_element_type=F32)


def _const_spec(shape, layer=None):
    if layer is None:
        return pl.BlockSpec(shape, lambda *_: (0,) * len(shape), pipeline_mode=pl.Buffered(1))
    return pl.BlockSpec((None,) + shape, lambda *_: (layer,) + (0,) * len(shape),
                        pipeline_mode=pl.Buffered(1))


def _mix_prompt_kernel(sinks_ref, x_ref, gpre_ref, gpost_ref, win_ref, wout_ref, wpool_ref, pscale_ref,
                       wg32_ref, wu32_ref, wd32_ref, wpg32_ref,
                       y_ref, klast_ref, vlast_ref, plast_ref, wg16_ref, wu16_ref, wd16_ref, wpg16_ref,
                       z_ref, kext_ref, vt_ref, uext_ref, ps1_ref, ps2_ref, mix_ref, bias_ref, *, tm, piece, layer):
    s = pl.program_id(1)
    nblk = tm // WINDOW

    def cast_weights():
        for w32_ref, w16_ref in ((wg32_ref, wg16_ref), (wu32_ref, wu16_ref), (wd32_ref, wd16_ref),
                                 (wpg32_ref, wpg16_ref)):
            w16_ref[...] = w32_ref[...].astype(BF16)

    @pl.when((pl.program_id(0) == 0) & (s == 0))
    def _():
        c = lax.broadcasted_iota(jnp.int32, (2 * WINDOW, WINDOW), 0)
        r = lax.broadcasted_iota(jnp.int32, (2 * WINDOW, WINDOW), 1)
        dist = r + WINDOW - c
        ok = (dist >= 0) & (dist < WINDOW)
        bias_ref[0] = jnp.where(ok, 0.0, -jnp.inf)
        bias_ref[1] = jnp.where(ok & (c >= WINDOW), 0.0, -jnp.inf)

    @pl.when(s == 0)
    def _():
        kext_ref[0:WINDOW, :] = jnp.zeros((WINDOW, KV_WIDTH), F32)
        vt_ref[:, 0:WINDOW] = jnp.zeros((KV_WIDTH, WINDOW), F32)
        uext_ref[0:POOL_HEAD, :] = jnp.zeros((POOL_HEAD, POOL_WIDTH), F32)
        ps1_ref[0:8, :] = jnp.zeros((8, POOL_GROUP_WIDTH), F32)
        ps2_ref[0:8, :] = jnp.zeros((8, POOL_GROUP_WIDTH), F32)

    lane = lax.broadcasted_iota(jnp.int32, (WINDOW, LANES), 1)
    lo = lane < HEAD_DIM
    lo2 = jnp.concatenate([lo, lo], axis=0)
    row8 = lax.broadcasted_iota(jnp.int32, (8, GROUP * WINDOW), 0)
    kcol = lax.broadcasted_iota(jnp.int32, (HEAD_DIM, 2 * WINDOW), 1)
    ones = jnp.ones((HEAD_DIM, 2 * WINDOW), F32)
    half = piece

    def project_in(hf):
        hr = slice(hf * half, (hf + 1) * half)
        h = _rms(x_ref[0, hr, :], gpre_ref[...]).astype(BF16)
        z_ref[hr, :] = _dot(h, win_ref[...])
        uext_ref[POOL_HEAD + hf * half:POOL_HEAD + (hf + 1) * half, :] = z_ref[hr, U_OFF:U_OFF + POOL_WIDTH]
        kext_ref[WINDOW + hf * half:WINDOW + (hf + 1) * half, :] = z_ref[hr, K_OFF:K_OFF + KV_WIDTH]
        for n in range(hf * half // WINDOW, (hf + 1) * half // WINDOW):
            r0 = n * WINDOW
            vt_ref[:, WINDOW + r0:2 * WINDOW + r0] = z_ref[r0:r0 + WINDOW, V_OFF:V_OFF + KV_WIDTH].T

    def attend_stages(blocks):
        st8 = {}

        def scores():
            for n in blocks:
                r0 = n * WINDOW
                rows = slice(r0, r0 + WINDOW)
                kcat = kext_ref[r0:r0 + 2 * WINDOW, :]
                kswap = pltpu.roll(kcat, HEAD_DIM, 1)
                bias1 = bias_ref[jnp.where(s == 0, 1, 0)] if n == 0 else bias_ref[0]
                bias = jnp.concatenate([bias1] * GROUP, axis=1)
                for kv in range(N_KV_HEADS):
                    kk = (jnp.where(lo2, kcat, kswap) if kv == 0 else jnp.where(lo2, kswap, kcat)).astype(BF16)
                    pieces, sinks = [], []
                    for g in range(GROUP):
                        hd = kv * GROUP + g
                        slab = hd // 2
                        qs = z_ref[rows, slab * LANES:(slab + 1) * LANES] * (HEAD_DIM ** -0.5 * LOG2E)
                        keep = lo if hd % 2 == 0 else jnp.logical_not(lo)
                        pieces.append(jnp.where(keep, qs, 0.0).astype(BF16))
                        sinks.append(jnp.full((1, WINDOW), sinks_ref[layer, hd] * LOG2E, F32))
                    q4 = jnp.concatenate(pieces, axis=0)
                    st8[n, kv, "sink"] = jnp.concatenate(sinks, axis=1)
                    st8[n, kv, "st"] = _dot_t(kk, q4) + bias

        def softmax():
            for n in blocks:
                for kv in range(N_KV_HEADS):
                    st, sink = st8.pop((n, kv, "st")), st8.pop((n, kv, "sink"))
                    m = jnp.maximum(jnp.max(st, axis=0, keepdims=True), sink)
                    p = jnp.exp2(st - m)
                    top = jnp.where(row8 == 0, jnp.exp2(sink - m), p[0:8])
                    st8[n, kv, "p"] = jnp.concatenate([top, p[8:]], axis=0).astype(BF16)

        def values():
            for n in blocks:
                r0 = n * WINDOW
                for kv in range(N_KV_HEADS):
                    vth = vt_ref[kv * HEAD_DIM:(kv + 1) * HEAD_DIM, r0:r0 + 2 * WINDOW]
                    a = jnp.concatenate([jnp.where(kcol == 0, 0.0, vth), ones], axis=0).astype(BF16)
                    st8[n, kv, "ot"] = _dot(a, st8.pop((n, kv, "p")))

        def store():
            for n in blocks:
                rows = slice(n * WINDOW, (n + 1) * WINDOW)
                for kv in range(N_KV_HEADS):
                    ot = st8.pop((n, kv, "ot"))
                    inv = 1.0 / ot[HEAD_DIM:HEAD_DIM + 8]
                    on = ot[0:HEAD_DIM] * jnp.concatenate([inv] * (HEAD_DIM // 8), axis=0)
                    for j in range(GROUP // 2):
                        c0 = 2 * j * WINDOW
                        pair = jnp.concatenate([on[:, c0:c0 + WINDOW], on[:, c0 + WINDOW:c0 + 2 * WINDOW]], axis=0)
                        slab = kv * (GROUP // 2) + j
                        mix_ref[rows, slab * LANES:(slab + 1) * LANES] = pair.T.astype(BF16)

        return scores, softmax, values, store

    def pool(hf):
        t0 = hf * half
        pos = lax.broadcasted_iota(jnp.int32, (half, 1), 0) + (s * tm + t0)
        first = POOL_HEAD + t0
        ext = slice(first - POOL_PAD, first + half)
        for gi, w in enumerate(POOL_WINDOWS):
            cols = slice(gi * POOL_GROUP_WIDTH, (gi + 1) * POOL_GROUP_WIDTH)
            src, src_cols, span, bufs = uext_ref, cols, 1, [ps1_ref, ps2_ref]
            while 2 * span < w:
                dst = bufs[0]
                dst[ext, :] = src[ext, src_cols] + src[first - POOL_PAD - span:first + half - span, src_cols]
                src, src_cols, span, bufs = dst, slice(None), 2 * span, bufs[::-1]
            tot = src[first:first + half, src_cols] + src[first - span:first - span + half, src_cols]
            cnt = jnp.minimum(w, pos + 1).astype(F32)
            dlt = (tot / cnt - uext_ref[first:first + half, cols]).astype(BF16)
            yp = _dot(dlt, wpool_ref[gi]) * pscale_ref[:, cols]
            mix_ref[t0:t0 + half, ATTN_WIDTH + gi * POOL_GROUP_WIDTH:ATTN_WIDTH + (gi + 1) * POOL_GROUP_WIDTH] = (
                yp.astype(BF16))

    def project_out(hf):
        hr = slice(hf * half, (hf + 1) * half)
        mixed = _dot(mix_ref[hr, :], wout_ref[...])
        y_ref[0, hr, :] = x_ref[0, hr, :] + _rms(mixed, gpost_ref[...])

    per_half = half // WINDOW
    n_piece = tm // half
    project_in(0)
    for hf in range(n_piece):
        blocks = list(range(hf * per_half, (hf + 1) * per_half))
        groups = [blocks[i:i + ATTN_GROUP] for i in range(0, len(blocks), ATTN_GROUP)]
        for gi, group in enumerate(groups):
            scores, softmax, values, store = attend_stages(group)
            scores()
            if gi == 0 and hf + 1 < n_piece:
                project_in(hf + 1)
            if gi == 0 and hf == 0:
                cast_weights()
            if gi == len(groups) - 1 and hf > 0:
                project_out(hf - 1)
            softmax()
            values()
            store()
        pool(hf)
    project_out(n_piece - 1)

    kext_ref[0:WINDOW, :] = kext_ref[tm:tm + WINDOW, :]
    vt_ref[:, 0:WINDOW] = vt_ref[:, tm:tm + WINDOW]
    uext_ref[POOL_HEAD - POOL_PAD:POOL_HEAD, :] = uext_ref[POOL_HEAD + tm - POOL_PAD:POOL_HEAD + tm, :]

    @pl.when(s == pl.num_programs(1) - 1)
    def _():
        klast_ref[0] = z_ref[tm - WINDOW:tm, K_OFF:K_OFF + KV_WIDTH]
        vlast_ref[0] = z_ref[tm - WINDOW:tm, V_OFF:V_OFF + KV_WIDTH]
        plast_ref[0] = uext_ref[POOL_HEAD + tm - POOL_PAD:POOL_HEAD + tm, :]


def _row_chunk_specs(shape, n_steps, layer, step_of):
    rows, cols = shape
    chunk = rows // n_steps
    assert chunk * n_steps == rows and chunk % 16 == 0, (shape, n_steps)
    return (pl.BlockSpec((None, chunk, cols), lambda *g: (layer, step_of(*g), 0)),
            pl.BlockSpec((chunk, cols), lambda *g: (step_of(*g), 0)))


def _mix_prompt(x, sinks, gpre, gpost, win, wout, wpool, pscale, wg32, wu32, wd32, wpg32, *, layer, tm, piece):
    b, s, _ = x.shape
    kern = functools.partial(_mix_prompt_kernel, tm=tm, piece=piece, layer=layer)
    n_tiles = s // tm
    cast_shapes = ((D_MODEL, D_FF), (D_MODEL, D_FF), (D_FF, D_MODEL), (D_MODEL, D_MODEL))
    cast_specs = [_row_chunk_specs(shape, b * n_tiles, layer, lambda i, j: i * n_tiles + j) for shape in cast_shapes]
    return pl.pallas_call(
        kern,
        grid=(b, n_tiles),
        in_specs=[
            pl.BlockSpec(memory_space=pltpu.SMEM),
            pl.BlockSpec((1, tm, D_MODEL), lambda i, j: (i, j, 0)),
            _const_spec((1, D_MODEL), layer),
            _const_spec((1, D_MODEL), layer),
            _const_spec((D_MODEL, IN_WIDTH), layer),
            _const_spec((D_MODEL, D_MODEL), layer),
            _const_spec((len(POOL_WINDOWS), POOL_GROUP_WIDTH, POOL_GROUP_WIDTH), layer),
            _const_spec((1, POOL_WIDTH), layer),
        ] + [spec_in for spec_in, _ in cast_specs],
        out_specs=[
            pl.BlockSpec((1, tm, D_MODEL), lambda i, j: (i, j, 0)),
            pl.BlockSpec((1, WINDOW, KV_WIDTH), lambda i, j: (i, 0, 0)),
            pl.BlockSpec((1, WINDOW, KV_WIDTH), lambda i, j: (i, 0, 0)),
            pl.BlockSpec((1, POOL_PAD, POOL_WIDTH), lambda i, j: (i, 0, 0)),
        ] + [spec_out for _, spec_out in cast_specs],
        out_shape=[
            jax.ShapeDtypeStruct((b, s, D_MODEL), F32),
            jax.ShapeDtypeStruct((b, WINDOW, KV_WIDTH), F32),
            jax.ShapeDtypeStruct((b, WINDOW, KV_WIDTH), F32),
            jax.ShapeDtypeStruct((b, POOL_PAD, POOL_WIDTH), F32),
        ] + [jax.ShapeDtypeStruct(shape, BF16) for shape in cast_shapes],
        scratch_shapes=[
            pltpu.VMEM((tm, IN_WIDTH), F32),
            pltpu.VMEM((WINDOW + tm, KV_WIDTH), F32),
            pltpu.VMEM((KV_WIDTH, WINDOW + tm), F32),
            pltpu.VMEM((POOL_HEAD + tm, POOL_WIDTH), F32),
            pltpu.VMEM((POOL_HEAD + tm, POOL_GROUP_WIDTH), F32),
            pltpu.VMEM((POOL_HEAD + tm, POOL_GROUP_WIDTH), F32),
            pltpu.VMEM((tm, D_MODEL), BF16),
            pltpu.VMEM((2, 2 * WINDOW, WINDOW), F32),
        ],
        compiler_params=pltpu.CompilerParams(
            dimension_semantics=("arbitrary", "arbitrary"), vmem_limit_bytes=VMEM_LIMIT),
        name=f"mix_prompt_{layer}",
    )(sinks, x, gpre, gpost, win, wout, wpool, pscale, wg32, wu32, wd32, wpg32)


def _swiglu(f, wg_ref, wu_ref, wd_ref):
    d, pending = None, None
    for c0, c1 in FF_CHUNKS:
        g = _dot(f, wg_ref[:, c0:c1])
        u = _dot(f, wu_ref[:, c0:c1])
        if pending is not None:
            part = _dot(pending[0], wd_ref[pending[1]:pending[2], :])
            d = part if d is None else d + part
        pending = ((g * _sigmoid(g) * u).astype(BF16), c0, c1)
    return d + _dot(pending[0], wd_ref[pending[1]:pending[2], :])


def _ffn_rows(x, p, gpre_ref, gpost_ref, wg_ref, wu_ref, wd_ref, wple_ref, wpg_ref):
    f = _rms(x, gpre_ref[...]).astype(BF16)
    x = x + _rms(_swiglu(f, wg_ref, wu_ref, wd_ref), gpost_ref[...])
    gate = _sigmoid(_dot(x.astype(BF16), wpg_ref[...]))
    return x + gate * _dot(p.astype(BF16), wple_ref[...])


def _ffn_kernel(x_ref, p_ref, gpre_ref, gpost_ref, wg_ref, wu_ref, wd_ref, wple_ref, wpg_ref, o_ref, *, nsub):
    sub = x_ref.shape[0] // nsub
    for i in range(nsub):
        rows = slice(i * sub, (i + 1) * sub)
        o_ref[rows, :] = _ffn_rows(x_ref[rows, :], p_ref[rows, :], gpre_ref, gpost_ref, wg_ref, wu_ref, wd_ref,
                                   wple_ref, wpg_ref)


def _ffn(x, p, gpre, gpost, wg, wu, wd, wple, wpg, *, layer, tm, nsub):
    n = x.shape[0]
    return pl.pallas_call(
        functools.partial(_ffn_kernel, nsub=nsub),
        grid=(n // tm,),
        in_specs=[
            pl.BlockSpec((tm, D_MODEL), lambda i: (i, 0)),
            pl.BlockSpec((None, tm, PLE_DIM), lambda i: (layer, i, 0)),
            _const_spec((1, D_MODEL), layer),
            _const_spec((1, D_MODEL), layer),
            _const_spec((D_MODEL, D_FF)),
            _const_spec((D_MODEL, D_FF)),
            _const_spec((D_FF, D_MODEL)),
            _const_spec((PLE_DIM, D_MODEL), layer),
            _const_spec((D_MODEL, D_MODEL)),
        ],
        out_specs=pl.BlockSpec((tm, D_MODEL), lambda i: (i, 0)),
        out_shape=jax.ShapeDtypeStruct((n, D_MODEL), F32),
        compiler_params=pltpu.CompilerParams(
            dimension_semantics=("arbitrary",), vmem_limit_bytes=VMEM_LIMIT),
        name=f"ffn_prompt_{layer}",
    )(x, p, gpre, gpost, wg, wu, wd, wple, wpg)


def _layer_spec(shape, single=False):
    kwargs = dict(pipeline_mode=pl.Buffered(1)) if single else {}
    return pl.BlockSpec((None,) + shape, lambda l, c: (l,) + (0,) * len(shape), **kwargs)


def _sample_kernel(sinks_ref, xs_ref, ps_ref, ck_ref, cv_ref, st_ref, gmpre_ref, gmpost_ref, gfpre_ref, gfpost_ref,
                   win_ref, wout_ref, wpool_ref, pscale_ref, wple_ref, *refs, bb, dec, nsub, depth):
    chunk_refs = refs[:4 * depth]
    (y_ref, ko_ref, vo_ref, po_ref, z_ref, mix_ref, ustage_ref, dlt_ref,
     wg_ref, wu_ref, wd_ref, wpg_ref) = refs[4 * depth:]
    l = pl.program_id(0)
    c = pl.program_id(1)
    n_rows = y_ref.shape[0]

    @pl.when((l == 0) & (c == 0))
    def _():
        y_ref[...] = xs_ref[...]

    @pl.when(c == 0)
    def _():
        h = _rms(y_ref[...], gmpre_ref[...]).astype(BF16)
        z_ref[...] = _dot(h, win_ref[...])

    def gather_weights():
        for j, dst_ref in enumerate((wg_ref, wu_ref, wd_ref, wpg_ref)):
            rows_w = chunk_refs[j].shape[0]
            val = chunk_refs[4 * (depth - 1) + j][...]
            for k in range(depth - 2, -1, -1):
                val = jnp.where(l == k, chunk_refs[4 * k + j][...], val)
            dst_ref[pl.ds(pl.multiple_of(c * rows_w, 16), rows_w), :] = val

    per_tile = 8 // dec
    base = pl.multiple_of(c * (bb * dec), 8)
    lane = lax.broadcasted_iota(jnp.int32, (8, LANES), 1)
    lo = lane < HEAD_DIM
    hi = jnp.logical_not(lo)
    rows64 = N_HEADS * 8
    ext_w = 2 * WINDOW
    n_new = bb * dec
    new0 = ext_w - n_new
    chunk = pl.ds(base, n_new)
    tiles = range(bb // per_tile)
    trow = lax.broadcasted_iota(jnp.int32, (rows64, ext_w), 0) & 7
    tstep = trow % dec
    tbat = trow // dec
    klane = lax.broadcasted_iota(jnp.int32, (rows64, ext_w), 1)
    valid = []
    for t in tiles:
        fresh = klane - (new0 + 8 * t)
        valid.append(((klane < WINDOW) & (klane >= tstep + 1)) | (
            (fresh >= 0) & (fresh < 8) & (fresh // dec == tbat) & (fresh % dec <= tstep)))
    tbat_o = tbat[:, 0:LANES]
    wlane = lax.broadcasted_iota(jnp.int32, (KV_WIDTH, WINDOW), 1)
    zpad = jnp.zeros((WINDOW - n_new, KV_WIDTH), F32)
    knew = jnp.concatenate([zpad, z_ref[chunk, K_OFF:K_OFF + KV_WIDTH]], axis=0)
    vnew = jnp.concatenate([zpad, z_ref[chunk, V_OFF:V_OFF + KV_WIDTH]], axis=0)

    st8 = {}

    for t in tiles:
        rows = pl.ds(base + 8 * t, 8)
        q = z_ref[rows, 0:ATTN_WIDTH] * (HEAD_DIM ** -0.5)
        pieces, sinks = [], []
        for hd in range(N_HEADS):
            slab, half, kv = hd // 2, hd % 2, hd // GROUP
            qs = q[:, slab * LANES:(slab + 1) * LANES]
            src = qs if half == kv else pltpu.roll(qs, HEAD_DIM, 1)
            pieces.append(jnp.where(lo if kv == 0 else hi, src, 0.0))
            sinks.append(jnp.full((8, 1), sinks_ref[l, hd], F32))
        st8[t, "lhs"] = jnp.concatenate(pieces, axis=0).astype(BF16)
        st8[t, "sink"] = jnp.concatenate(sinks, axis=0)

    fresh_scores = _dot_t(jnp.concatenate([st8[t, "lhs"] for t in tiles], axis=0), knew.astype(BF16))
    for t in tiles:
        lhs_b = st8.pop((t, "lhs"))
        sc = None
        for i in range(per_tile):
            si = _dot(lhs_b, ck_ref[per_tile * t + i].astype(BF16))
            sc = si if sc is None else jnp.where(tbat_o == i, si, sc)
        sc = jnp.concatenate([sc, fresh_scores[rows64 * t:rows64 * (t + 1)]], axis=1)
        st8[t, "sc"] = jnp.where(valid[t], sc, -jnp.inf)

    gather_weights()

    for t in tiles:
        sc, sink = st8.pop((t, "sc")), st8.pop((t, "sink"))
        m = jnp.maximum(jnp.max(sc, axis=-1, keepdims=True), sink)
        p = jnp.exp(sc - m)
        st8[t, "denom"] = jnp.sum(p, axis=-1, keepdims=True) + jnp.exp(sink - m)
        st8[t, "pb"] = p.astype(BF16)

    fresh_out = _dot(jnp.concatenate([st8[t, "pb"][:, WINDOW:ext_w] for t in tiles], axis=0), vnew.astype(BF16))
    for t in tiles:
        pb = st8.pop((t, "pb"))
        o = None
        for i in range(per_tile):
            oi = _dot_t(pb[:, 0:WINDOW], cv_ref[per_tile * t + i].astype(BF16))
            o = oi if o is None else jnp.where(tbat_o == i, oi, o)
        st8[t, "o"] = o + fresh_out[rows64 * t:rows64 * (t + 1)]

    for t in tiles:
        rows = pl.ds(base + 8 * t, 8)
        o = st8.pop((t, "o")) / st8.pop((t, "denom"))
        for slab in range(N_HEADS // 2):
            kv = (2 * slab) // GROUP
            even = o[16 * slab:16 * slab + 8]
            odd = o[16 * slab + 8:16 * slab + 16]
            if kv == 0:
                res = jnp.where(lo, even, pltpu.roll(odd, HEAD_DIM, 1))
            else:
                res = jnp.where(lo, pltpu.roll(even, HEAD_DIM, 1), odd)
            mix_ref[rows, slab * LANES:(slab + 1) * LANES] = res

    for new, c_ref, o_ref in ((knew, ck_ref, ko_ref), (vnew, cv_ref, vo_ref)):
        cols = new.T
        for b in range(bb):
            shift = dec * (bb - 1 - b)
            mine = cols if shift == 0 else pltpu.roll(cols, shift, 1)
            kept = pltpu.roll(c_ref[b], WINDOW - dec, 1)
            o_ref[b] = jnp.where(wlane < WINDOW - dec, kept, mine)

    for r in range(POOL_STATE - dec):
        po_ref[r] = st_ref[r + dec]
    for gi, w in enumerate(POOL_WINDOWS):
        cols = slice(gi * POOL_GROUP_WIDTH, (gi + 1) * POOL_GROUP_WIDTH)
        ustage_ref[gi] = z_ref[chunk, U_OFF + gi * POOL_GROUP_WIDTH:U_OFF + (gi + 1) * POOL_GROUP_WIDTH]
        u = [ustage_ref[gi, pl.ds(j, bb, stride=dec), :] for j in range(dec)]
        suffix, acc = {}, None
        for m in range(1, w):
            row = st_ref[POOL_STATE - m, :, cols]
            acc = row if acc is None else acc + row
            suffix[m] = acc
        for j in range(dec):
            po_ref[POOL_STATE - dec + j, :, cols] = u[j]
            tot = suffix.get(w - 1 - j)
            for i in range(max(0, j - w + 1), j + 1):
                tot = u[i] if tot is None else tot + u[i]
            dlt_ref[gi, pl.ds(j, bb, stride=dec), :] = tot / float(w) - u[j]

    for gi in range(len(POOL_WINDOWS)):
        cols = slice(gi * POOL_GROUP_WIDTH, (gi + 1) * POOL_GROUP_WIDTH)
        yp = _dot(dlt_ref[gi].astype(BF16), wpool_ref[gi]) * pscale_ref[:, cols]
        mix_ref[chunk, ATTN_WIDTH + gi * POOL_GROUP_WIDTH:ATTN_WIDTH + (gi + 1) * POOL_GROUP_WIDTH] = yp

    @pl.when(c == pl.num_programs(1) - 1)
    def _():
        sub = n_rows // nsub
        for i in range(nsub):
            rs = slice(i * sub, (i + 1) * sub)
            mixed = _dot(mix_ref[rs, :].astype(BF16), wout_ref[...])
            x = y_ref[rs, :] + _rms(mixed, gmpost_ref[...])
            y_ref[rs, :] = _ffn_rows(x, ps_ref[rs, :], gfpre_ref, gfpost_ref, wg_ref, wu_ref, wd_ref, wple_ref,
                                     wpg_ref)


def _sample_path(xs, ps, sinks, ck, cv, st, gmpre, gmpost, gfpre, gfpost, win, wout, wpool, pscale,
                 wple, ff_weights, *, bb, dec, nsub):
    depth, nb = ck.shape[0], ck.shape[1]
    n = nb * dec
    kern = functools.partial(_sample_kernel, bb=bb, dec=dec, nsub=nsub, depth=depth)
    n_chunks = nb // bb

    def _chunk_spec(w, k):
        rows, cols = w.shape
        assert rows % n_chunks == 0 and (rows // n_chunks) % 16 == 0
        return pl.BlockSpec((rows // n_chunks, cols),
                            lambda l, c: (jnp.where(l == k, c, jnp.where(l > k, n_chunks - 1, 0)), 0))

    cache_spec = pl.BlockSpec((None, bb, KV_WIDTH, WINDOW), lambda l, c: (l, c, 0, 0))
    state_spec = pl.BlockSpec((None, POOL_STATE, bb, POOL_WIDTH), lambda l, c: (l, 0, c, 0))
    return pl.pallas_call(
        kern,
        grid=(depth, nb // bb),
        in_specs=[
            pl.BlockSpec(memory_space=pltpu.SMEM),
            pl.BlockSpec((n, D_MODEL), lambda l, c: (0, 0), pipeline_mode=pl.Buffered(1)),
            _layer_spec((n, PLE_DIM)),
            cache_spec, cache_spec, state_spec,
            _layer_spec((1, D_MODEL)), _layer_spec((1, D_MODEL)), _layer_spec((1, D_MODEL)), _layer_spec((1, D_MODEL)),
            _layer_spec((D_MODEL, IN_WIDTH), single=True),
            _layer_spec((D_MODEL, D_MODEL), single=True),
            _layer_spec((len(POOL_WINDOWS), POOL_GROUP_WIDTH, POOL_GROUP_WIDTH)),
            _layer_spec((1, POOL_WIDTH)),
            _layer_spec((PLE_DIM, D_MODEL), single=True),
        ] + [_chunk_spec(w, k) for k, ws in enumerate(ff_weights) for w in ws],
        out_specs=[
            pl.BlockSpec((n, D_MODEL), lambda l, c: (0, 0)),
            cache_spec, cache_spec, state_spec,
        ],
        out_shape=[
            jax.ShapeDtypeStruct((n, D_MODEL), F32),
            jax.ShapeDtypeStruct((depth, nb, KV_WIDTH, WINDOW), F32),
            jax.ShapeDtypeStruct((depth, nb, KV_WIDTH, WINDOW), F32),
            jax.ShapeDtypeStruct((depth, POOL_STATE, nb, POOL_WIDTH), F32),
        ],
        scratch_shapes=[
            pltpu.VMEM((n, IN_WIDTH), F32),
            pltpu.VMEM((n, D_MODEL), F32),
            pltpu.VMEM((len(POOL_WINDOWS), bb * dec, POOL_GROUP_WIDTH), F32),
            pltpu.VMEM((len(POOL_WINDOWS), bb * dec, POOL_GROUP_WIDTH), F32),
            pltpu.VMEM((D_MODEL, D_FF), BF16),
            pltpu.VMEM((D_MODEL, D_FF), BF16),
            pltpu.VMEM((D_FF, D_MODEL), BF16),
            pltpu.VMEM((D_MODEL, D_MODEL), BF16),
        ],
        compiler_params=pltpu.CompilerParams(
            dimension_semantics=("arbitrary", "arbitrary"), vmem_limit_bytes=VMEM_LIMIT),
        name="sample_path",
    )(sinks, xs, ps, ck, cv, st, gmpre, gmpost, gfpre, gfpost, win, wout, wpool, pscale, wple,
      *[w for ws in ff_weights for w in ws])


def _cache_from_device_layout(c):
    return c.reshape(c.shape[0], c.shape[1], N_KV_HEADS, HEAD_DIM, WINDOW).transpose(0, 1, 4, 2, 3)


def kernel(x_prompt, x_sample, p_prompt, p_sample, cache_k, cache_v, state_pool, norm_mix_pre, norm_mix_post,
           norm_ffn_pre, norm_ffn_post, w_in, w_out, attn_sinks, w_pool, pool_scale, w_gate, w_up, w_down, w_ple,
           w_ple_gate):
    depth, nbat, seq, _ = p_prompt.shape
    dec_b, dec = x_sample.shape[0], x_sample.shape[1]
    assert 8 % dec == 0 and seq % PROMPT_TILE == 0 and (nbat * seq) % FFN_TILE == 0 and dec_b % SAMPLE_BATCHES == 0
    assert cache_k.shape[2:] == (WINDOW, N_KV_HEADS, HEAD_DIM) and state_pool.shape[2:] == (POOL_STATE, POOL_WIDTH)

    win, wout, wpool, wple = w_in.astype(BF16), w_out.astype(BF16), w_pool.astype(BF16), w_ple.astype(BF16)
    g_mix_pre = norm_mix_pre.reshape(depth, 1, D_MODEL)
    g_mix_post = norm_mix_post.reshape(depth, 1, D_MODEL)
    g_ffn_pre = norm_ffn_pre.reshape(depth, 1, D_MODEL)
    g_ffn_post = norm_ffn_post.reshape(depth, 1, D_MODEL)
    pscale = pool_scale.reshape(depth, 1, POOL_WIDTH)

    pp = p_prompt.reshape(depth, nbat * seq, PLE_DIM)
    ps = p_sample.reshape(depth, dec_b * dec, PLE_DIM)
    ck = cache_k.transpose(0, 1, 3, 4, 2).reshape(depth, dec_b, KV_WIDTH, WINDOW)
    cv = cache_v.transpose(0, 1, 3, 4, 2).reshape(depth, dec_b, KV_WIDTH, WINDOW)

    yp = x_prompt
    kp_l, vp_l, sp_l, ff_weights = [], [], [], []
    for i in range(depth):
        yp, kp, vp, sp, wg, wu, wd, wpg = _mix_prompt(yp, attn_sinks, g_mix_pre, g_mix_post, win, wout, wpool, pscale,
                                                      w_gate, w_up, w_down, w_ple_gate, layer=i, tm=PROMPT_TILE,
                                                      piece=PROMPT_PIECE)
        yp = _ffn(yp.reshape(nbat * seq, D_MODEL), pp, g_ffn_pre, g_ffn_post, wg, wu, wd, wple, wpg,
                  layer=i, tm=FFN_TILE, nsub=FFN_SUBTILES).reshape(nbat, seq, D_MODEL)
        ff_weights.append((wg, wu, wd, wpg))
        kp_l.append(kp)
        vp_l.append(vp)
        sp_l.append(sp[:, POOL_PAD - POOL_STATE:])

    ys, ks, vs, ss = _sample_path(x_sample.reshape(dec_b * dec, D_MODEL), ps, attn_sinks, ck, cv,
                                  state_pool.transpose(0, 2, 1, 3),
                                  g_mix_pre, g_mix_post, g_ffn_pre, g_ffn_post, win, wout, wpool, pscale,
                                  wple, ff_weights, bb=SAMPLE_BATCHES, dec=dec, nsub=SAMPLE_SUBTILES)

    kv_shape = (depth, -1, WINDOW, N_KV_HEADS, HEAD_DIM)
    return (yp, ys.reshape(dec_b, dec, D_MODEL),
            jnp.stack(kp_l).reshape(kv_shape), jnp.stack(vp_l).reshape(kv_shape), jnp.stack(sp_l),
            _cache_from_device_layout(ks), _cache_from_device_layout(vs), ss.transpose(0, 2, 1, 3))
```

```python
import functools

import jax
import jax.numpy as jnp
from jax import lax
from jax.experimental import pallas as pl
from jax.experimental.pallas import tpu as pltpu

D_MODEL = 1024
DEPTH = 4
ATTN_WIDTH = 512
HEAD_DIM = 64
N_HEADS = 8
N_KV_HEADS = 2
GROUP = 4
KV_WIDTH = 128
WINDOW = 128
POOL_WIDTH = 512
POOL_WINDOWS = (2, 4, 8, 16)
POOL_GROUP_WIDTH = 128
POOL_STATE = 15
IN_WIDTH = 1280
D_FF = 2816
PLE_DIM = 256
EPS = 1e-6

K_OFF = ATTN_WIDTH
V_OFF = ATTN_WIDTH + KV_WIDTH
U_OFF = ATTN_WIDTH + 2 * KV_WIDTH
LANES = 128
POOL_PAD = 16
POOL_HEAD = 8 + POOL_PAD
LOG2E = 1.4426950408889634
FF_CHUNKS = ((0, 1024), (1024, 2048), (2048, D_FF))

BF16 = jnp.bfloat16
F32 = jnp.float32
V7X_VMEM_BYTES = 64 * 1024 * 1024
VMEM_LIMIT = V7X_VMEM_BYTES - 8 * 1024 * 1024

PROMPT_TILE = 1024
PROMPT_PIECE = 512
ATTN_GROUP = 4
FFN_TILE = 1024
FFN_SUBTILES = 2
SAMPLE_BATCHES = 8
SAMPLE_SUBTILES = 1
CACHE_SLOTS = 3


def _rms(x, g):
    return x * lax.rsqrt(jnp.mean(x * x, axis=-1, keepdims=True) + EPS) * g


def _sigmoid(x):
    return 1.0 / (1.0 + jnp.exp(-x))


def _dot(a, b):
    return jnp.dot(a, b, preferred_element_type=F32)


def _dot_t(a, b):
    return lax.dot_general(a, b, (((1,), (1,)), ((), ())), preferred_element_type=F32)


def _const_spec(shape, layer=None):
    if layer is None:
        return pl.BlockSpec(shape, lambda *_: (0,) * len(shape), pipeline_mode=pl.Buffered(1))
    return pl.BlockSpec((None,) + shape, lambda *_: (layer,) + (0,) * len(shape),
                        pipeline_mode=pl.Buffered(1))


def _mix_prompt_kernel(sinks_ref, x_ref, gpre_ref, gpost_ref, win_ref, wout_ref, wpool_ref, pscale_ref,
                       wg32_ref, wu32_ref, wd32_ref, wpg32_ref,
                       y_ref, klast_ref, vlast_ref, plast_ref, wg16_ref, wu16_ref, wd16_ref, wpg16_ref,
                       z_ref, kext_ref, vt_ref, uext_ref, ps1_ref, ps2_ref, mix_ref, bias_ref, *, tm, piece, layer):
    s = pl.program_id(1)
    nblk = tm // WINDOW

    def cast_weights():
        for w32_ref, w16_ref in ((wg32_ref, wg16_ref), (wu32_ref, wu16_ref), (wd32_ref, wd16_ref),
                                 (wpg32_ref, wpg16_ref)):
            w16_ref[...] = w32_ref[...].astype(BF16)

    @pl.when((pl.program_id(0) == 0) & (s == 0))
    def _():
        c = lax.broadcasted_iota(jnp.int32, (2 * WINDOW, WINDOW), 0)
        r = lax.broadcasted_iota(jnp.int32, (2 * WINDOW, WINDOW), 1)
        dist = r + WINDOW - c
        ok = (dist >= 0) & (dist < WINDOW)
        bias_ref[0] = jnp.where(ok, 0.0, -jnp.inf)
        bias_ref[1] = jnp.where(ok & (c >= WINDOW), 0.0, -jnp.inf)

    @pl.when(s == 0)
    def _():
        kext_ref[0:WINDOW, :] = jnp.zeros((WINDOW, KV_WIDTH), F32)
        vt_ref[:, 0:WINDOW] = jnp.zeros((KV_WIDTH, WINDOW), F32)
        uext_ref[0:POOL_HEAD, :] = jnp.zeros((POOL_HEAD, POOL_WIDTH), F32)
        ps1_ref[0:8, :] = jnp.zeros((8, POOL_GROUP_WIDTH), F32)
        ps2_ref[0:8, :] = jnp.zeros((8, POOL_GROUP_WIDTH), F32)

    lane = lax.broadcasted_iota(jnp.int32, (WINDOW, LANES), 1)
    lo = lane < HEAD_DIM
    lo2 = jnp.concatenate([lo, lo], axis=0)
    row8 = lax.broadcasted_iota(jnp.int32, (8, GROUP * WINDOW), 0)
    kcol = lax.broadcasted_iota(jnp.int32, (HEAD_DIM, 2 * WINDOW), 1)
    ones = jnp.ones((HEAD_DIM, 2 * WINDOW), F32)
    half = piece

    def project_in(hf):
        hr = slice(hf * half, (hf + 1) * half)
        h = _rms(x_ref[0, hr, :], gpre_ref[...]).astype(BF16)
        z_ref[hr, :] = _dot(h, win_ref[...])
        uext_ref[POOL_HEAD + hf * half:POOL_HEAD + (hf + 1) * half, :] = z_ref[hr, U_OFF:U_OFF + POOL_WIDTH]
        kext_ref[WINDOW + hf * half:WINDOW + (hf + 1) * half, :] = z_ref[hr, K_OFF:K_OFF + KV_WIDTH]
        for n in range(hf * half // WINDOW, (hf + 1) * half // WINDOW):
            r0 = n * WINDOW
            vt_ref[:, WINDOW + r0:2 * WINDOW + r0] = z_ref[r0:r0 + WINDOW, V_OFF:V_OFF + KV_WIDTH].T

    def attend_stages(blocks):
        st8 = {}

        def scores():
            for n in blocks:
                r0 = n * WINDOW
                rows = slice(r0, r0 + WINDOW)
                kcat = kext_ref[r0:r0 + 2 * WINDOW, :]
                kswap = pltpu.roll(kcat, HEAD_DIM, 1)
                bias1 = bias_ref[jnp.where(s == 0, 1, 0)] if n == 0 else bias_ref[0]
                bias = jnp.concatenate([bias1] * GROUP, axis=1)
                for kv in range(N_KV_HEADS):
                    kk = (jnp.where(lo2, kcat, kswap) if kv == 0 else jnp.where(lo2, kswap, kcat)).astype(BF16)
                    pieces, sinks = [], []
                    for g in range(GROUP):
                        hd = kv * GROUP + g
                        slab = hd // 2
                        qs = z_ref[rows, slab * LANES:(slab + 1) * LANES] * (HEAD_DIM ** -0.5 * LOG2E)
                        keep = lo if hd % 2 == 0 else jnp.logical_not(lo)
                        pieces.append(jnp.where(keep, qs, 0.0).astype(BF16))
                        sinks.append(jnp.full((1, WINDOW), sinks_ref[layer, hd] * LOG2E, F32))
                    q4 = jnp.concatenate(pieces, axis=0)
                    st8[n, kv, "sink"] = jnp.concatenate(sinks, axis=1)
                    st8[n, kv, "st"] = _dot_t(kk, q4) + bias

        def softmax():
            for n in blocks:
                for kv in range(N_KV_HEADS):
                    st, sink = st8.pop((n, kv, "st")), st8.pop((n, kv, "sink"))
                    m = jnp.maximum(jnp.max(st, axis=0, keepdims=True), sink)
                    p = jnp.exp2(st - m)
                    top = jnp.where(row8 == 0, jnp.exp2(sink - m), p[0:8])
                    st8[n, kv, "p"] = jnp.concatenate([top, p[8:]], axis=0).astype(BF16)

        def values():
            for n in blocks:
                r0 = n * WINDOW
                for kv in range(N_KV_HEADS):
                    vth = vt_ref[kv * HEAD_DIM:(kv + 1) * HEAD_DIM, r0:r0 + 2 * WINDOW]
                    a = jnp.concatenate([jnp.where(kcol == 0, 0.0, vth), ones], axis=0).astype(BF16)
                    st8[n, kv, "ot"] = _dot(a, st8.pop((n, kv, "p")))

        def store():
            for n in blocks:
                rows = slice(n * WINDOW, (n + 1) * WINDOW)
                for kv in range(N_KV_HEADS):
                    ot = st8.pop((n, kv, "ot"))
                    inv = 1.0 / ot[HEAD_DIM:HEAD_DIM + 8]
                    on = ot[0:HEAD_DIM] * jnp.concatenate([inv] * (HEAD_DIM // 8), axis=0)
                    for j in range(GROUP // 2):
                        c0 = 2 * j * WINDOW
                        pair = jnp.concatenate([on[:, c0:c0 + WINDOW], on[:, c0 + WINDOW:c0 + 2 * WINDOW]], axis=0)
                        slab = kv * (GROUP // 2) + j
                        mix_ref[rows, slab * LANES:(slab + 1) * LANES] = pair.T.astype(BF16)

        return scores, softmax, values, store

    def pool(hf):
        t0 = hf * half
        pos = lax.broadcasted_iota(jnp.int32, (half, 1), 0) + (s * tm + t0)
        first = POOL_HEAD + t0
        ext = slice(first - POOL_PAD, first + half)
        for gi, w in enumerate(POOL_WINDOWS):
            cols = slice(gi * POOL_GROUP_WIDTH, (gi + 1) * POOL_GROUP_WIDTH)
            src, src_cols, span, bufs = uext_ref, cols, 1, [ps1_ref, ps2_ref]
            while 2 * span < w:
                dst = bufs[0]
                dst[ext, :] = src[ext, src_cols] + src[first - POOL_PAD - span:first + half - span, src_cols]
                src, src_cols, span, bufs = dst, slice(None), 2 * span, bufs[::-1]
            tot = src[first:first + half, src_cols] + src[first - span:first - span + half, src_cols]
            cnt = jnp.minimum(w, pos + 1).astype(F32)
            dlt = (tot / cnt - uext_ref[first:first + half, cols]).astype(BF16)
            yp = _dot(dlt, wpool_ref[gi]) * pscale_ref[:, cols]
            mix_ref[t0:t0 + half, ATTN_WIDTH + gi * POOL_GROUP_WIDTH:ATTN_WIDTH + (gi + 1) * POOL_GROUP_WIDTH] = (
                yp.astype(BF16))

    def project_out(hf):
        hr = slice(hf * half, (hf + 1) * half)
        mixed = _dot(mix_ref[hr, :], wout_ref[...])
        y_ref[0, hr, :] = x_ref[0, hr, :] + _rms(mixed, gpost_ref[...])

    per_half = half // WINDOW
    n_piece = tm // half
    project_in(0)
    for hf in range(n_piece):
        blocks = list(range(hf * per_half, (hf + 1) * per_half))
        groups = [blocks[i:i + ATTN_GROUP] for i in range(0, len(blocks), ATTN_GROUP)]
        for gi, group in enumerate(groups):
            scores, softmax, values, store = attend_stages(group)
            scores()
            if gi == 0 and hf + 1 < n_piece:
                project_in(hf + 1)
            if gi == 0 and hf == 0:
                cast_weights()
            if gi == len(groups) - 1 and hf > 0:
                project_out(hf - 1)
            softmax()
            values()
            store()
        pool(hf)
    project_out(n_piece - 1)

    kext_ref[0:WINDOW, :] = kext_ref[tm:tm + WINDOW, :]
    vt_ref[:, 0:WINDOW] = vt_ref[:, tm:tm + WINDOW]
    uext_ref[POOL_HEAD - POOL_PAD:POOL_HEAD, :] = uext_ref[POOL_HEAD + tm - POOL_PAD:POOL_HEAD + tm, :]

    @pl.when(s == pl.num_programs(1) - 1)
    def _():
        klast_ref[0] = z_ref[tm - WINDOW:tm, K_OFF:K_OFF + KV_WIDTH]
        vlast_ref[0] = z_ref[tm - WINDOW:tm, V_OFF:V_OFF + KV_WIDTH]
        plast_ref[0] = uext_ref[POOL_HEAD + tm - POOL_PAD:POOL_HEAD + tm, :]


def _row_chunk_specs(shape, n_steps, layer, step_of):
    rows, cols = shape
    chunk = rows // n_steps
    assert chunk * n_steps == rows and chunk % 16 == 0, (shape, n_steps)
    return (pl.BlockSpec((None, chunk, cols), lambda *g: (layer, step_of(*g), 0)),
            pl.BlockSpec((chunk, cols), lambda *g: (step_of(*g), 0)))


def _mix_prompt(x, sinks, gpre, gpost, win, wout, wpool, pscale, wg32, wu32, wd32, wpg32, *, layer, tm, piece):
    b, s, _ = x.shape
    kern = functools.partial(_mix_prompt_kernel, tm=tm, piece=piece, layer=layer)
    n_tiles = s // tm
    cast_shapes = ((D_MODEL, D_FF), (D_MODEL, D_FF), (D_FF, D_MODEL), (D_MODEL, D_MODEL))
    cast_specs = [_row_chunk_specs(shape, b * n_tiles, layer, lambda i, j: i * n_tiles + j) for shape in cast_shapes]
    return pl.pallas_call(
        kern,
        grid=(b, n_tiles),
        in_specs=[
            pl.BlockSpec(memory_space=pltpu.SMEM),
            pl.BlockSpec((1, tm, D_MODEL), lambda i, j: (i, j, 0)),
            _const_spec((1, D_MODEL), layer),
            _const_spec((1, D_MODEL), layer),
            _const_spec((D_MODEL, IN_WIDTH), layer),
            _const_spec((D_MODEL, D_MODEL), layer),
            _const_spec((len(POOL_WINDOWS), POOL_GROUP_WIDTH, POOL_GROUP_WIDTH), layer),
            _const_spec((1, POOL_WIDTH), layer),
        ] + [spec_in for spec_in, _ in cast_specs],
        out_specs=[
            pl.BlockSpec((1, tm, D_MODEL), lambda i, j: (i, j, 0)),
            pl.BlockSpec((1, WINDOW, KV_WIDTH), lambda i, j: (i, 0, 0)),
            pl.BlockSpec((1, WINDOW, KV_WIDTH), lambda i, j: (i, 0, 0)),
            pl.BlockSpec((1, POOL_PAD, POOL_WIDTH), lambda i, j: (i, 0, 0)),
        ] + [spec_out for _, spec_out in cast_specs],
        out_shape=[
            jax.ShapeDtypeStruct((b, s, D_MODEL), F32),
            jax.ShapeDtypeStruct((b, WINDOW, KV_WIDTH), F32),
            jax.ShapeDtypeStruct((b, WINDOW, KV_WIDTH), F32),
            jax.ShapeDtypeStruct((b, POOL_PAD, POOL_WIDTH), F32),
        ] + [jax.ShapeDtypeStruct(shape, BF16) for shape in cast_shapes],
        scratch_shapes=[
            pltpu.VMEM((tm, IN_WIDTH), F32),
            pltpu.VMEM((WINDOW + tm, KV_WIDTH), F32),
            pltpu.VMEM((KV_WIDTH, WINDOW + tm), F32),
            pltpu.VMEM((POOL_HEAD + tm, POOL_WIDTH), F32),
            pltpu.VMEM((POOL_HEAD + tm, POOL_GROUP_WIDTH), F32),
            pltpu.VMEM((POOL_HEAD + tm, POOL_GROUP_WIDTH), F32),
            pltpu.VMEM((tm, D_MODEL), BF16),
            pltpu.VMEM((2, 2 * WINDOW, WINDOW), F32),
        ],
        compiler_params=pltpu.CompilerParams(
            dimension_semantics=("arbitrary", "arbitrary"), vmem_limit_bytes=VMEM_LIMIT),
        name=f"mix_prompt_{layer}",
    )(sinks, x, gpre, gpost, win, wout, wpool, pscale, wg32, wu32, wd32, wpg32)


def _swiglu(f, wg_ref, wu_ref, wd_ref):
    d, pending = None, None
    for c0, c1 in FF_CHUNKS:
        g = _dot(f, wg_ref[:, c0:c1])
        u = _dot(f, wu_ref[:, c0:c1])
        if pending is not None:
            part = _dot(pending[0], wd_ref[pending[1]:pending[2], :])
            d = part if d is None else d + part
        pending = ((g * _sigmoid(g) * u).astype(BF16), c0, c1)
    return d + _dot(pending[0], wd_ref[pending[1]:pending[2], :])


def _ffn_rows(x, p, gpre_ref, gpost_ref, wg_ref, wu_ref, wd_ref, wple_ref, wpg_ref):
    f = _rms(x, gpre_ref[...]).astype(BF16)
    x = x + _rms(_swiglu(f, wg_ref, wu_ref, wd_ref), gpost_ref[...])
    gate = _sigmoid(_dot(x.astype(BF16), wpg_ref[...]))
    return x + gate * _dot(p.astype(BF16), wple_ref[...])


def _ffn_kernel(x_ref, p_ref, gpre_ref, gpost_ref, wg_ref, wu_ref, wd_ref, wple_ref, wpg_ref, o_ref, *, nsub):
    sub = x_ref.shape[0] // nsub
    for i in range(nsub):
        rows = slice(i * sub, (i + 1) * sub)
        o_ref[rows, :] = _ffn_rows(x_ref[rows, :], p_ref[rows, :], gpre_ref, gpost_ref, wg_ref, wu_ref, wd_ref,
                                   wple_ref, wpg_ref)


def _ffn(x, p, gpre, gpost, wg, wu, wd, wple, wpg, *, layer, tm, nsub):
    n = x.shape[0]
    return pl.pallas_call(
        functools.partial(_ffn_kernel, nsub=nsub),
        grid=(n // tm,),
        in_specs=[
            pl.BlockSpec((tm, D_MODEL), lambda i: (i, 0)),
            pl.BlockSpec((None, tm, PLE_DIM), lambda i: (layer, i, 0)),
            _const_spec((1, D_MODEL), layer),
            _const_spec((1, D_MODEL), layer),
            _const_spec((D_MODEL, D_FF)),
            _const_spec((D_MODEL, D_FF)),
            _const_spec((D_FF, D_MODEL)),
            _const_spec((PLE_DIM, D_MODEL), layer),
            _const_spec((D_MODEL, D_MODEL)),
        ],
        out_specs=pl.BlockSpec((tm, D_MODEL), lambda i: (i, 0)),
        out_shape=jax.ShapeDtypeStruct((n, D_MODEL), F32),
        compiler_params=pltpu.CompilerParams(
            dimension_semantics=("arbitrary",), vmem_limit_bytes=VMEM_LIMIT),
        name=f"ffn_prompt_{layer}",
    )(x, p, gpre, gpost, wg, wu, wd, wple, wpg)


def _layer_spec(shape, single=False):
    kwargs = dict(pipeline_mode=pl.Buffered(1)) if single else {}
    return pl.BlockSpec((None,) + shape, lambda l, c: (l,) + (0,) * len(shape), **kwargs)


def _sample_kernel(sinks_ref, xs_ref, ps_ref, ck_hbm, cv_hbm, st_ref, gmpre_ref, gmpost_ref, gfpre_ref, gfpost_ref,
                   win_ref, wout_ref, wpool_ref, pscale_ref, wple_ref, *refs, bb, dec, nsub, depth):
    chunk_refs = refs[:4 * depth]
    (y_ref, ko_ref, vo_ref, po_ref, z_ref, mix_ref, ustage_ref, dlt_ref,
     wg_ref, wu_ref, wd_ref, wpg_ref, ckbuf_ref, cvbuf_ref, cache_sem) = refs[4 * depth:]
    l = pl.program_id(0)
    c = pl.program_id(1)
    n_rows = y_ref.shape[0]

    n_chunk = pl.num_programs(1)
    g = l * n_chunk + c
    n_steps = pl.num_programs(0) * n_chunk

    def cache_copies(step):
        slot = step % CACHE_SLOTS
        src = (step // n_chunk, pl.ds((step % n_chunk) * bb, bb))
        return (pltpu.make_async_copy(ck_hbm.at[src], ckbuf_ref.at[slot], cache_sem.at[0, slot]),
                pltpu.make_async_copy(cv_hbm.at[src], cvbuf_ref.at[slot], cache_sem.at[1, slot]))

    @pl.when(g == 0)
    def _():
        for ahead in range(CACHE_SLOTS - 1):
            for copy in cache_copies(g + ahead):
                copy.start()

    @pl.when(g + (CACHE_SLOTS - 1) < n_steps)
    def _():
        for copy in cache_copies(g + (CACHE_SLOTS - 1)):
            copy.start()

    for copy in cache_copies(g):
        copy.wait()
    ck_ref = ckbuf_ref.at[g % CACHE_SLOTS]
    cv_ref = cvbuf_ref.at[g % CACHE_SLOTS]

    @pl.when((l == 0) & (c == 0))
    def _():
        y_ref[...] = xs_ref[...]

    @pl.when(c == 0)
    def _():
        h = _rms(y_ref[...], gmpre_ref[...]).astype(BF16)
        z_ref[...] = _dot(h, win_ref[...])

    for k in range(depth):
        @pl.when(l == k)
        def _(k=k):
            for src_ref, dst_ref in zip(chunk_refs[4 * k:4 * k + 4], (wg_ref, wu_ref, wd_ref, wpg_ref)):
                rows_w = src_ref.shape[0]
                dst_ref[pl.ds(pl.multiple_of(c * rows_w, 16), rows_w), :] = src_ref[...]

    per_tile = 8 // dec
    base = pl.multiple_of(c * (bb * dec), 8)
    lane = lax.broadcasted_iota(jnp.int32, (8, LANES), 1)
    lo = lane < HEAD_DIM
    hi = jnp.logical_not(lo)
    rows64 = N_HEADS * 8
    ext_w = 2 * WINDOW
    n_new = bb * dec
    new0 = ext_w - n_new
    chunk = pl.ds(base, n_new)
    tiles = range(bb // per_tile)
    trow = lax.broadcasted_iota(jnp.int32, (rows64, ext_w), 0) & 7
    tstep = trow % dec
    tbat = trow // dec
    klane = lax.broadcasted_iota(jnp.int32, (rows64, ext_w), 1)
    valid = []
    for t in tiles:
        fresh = klane - (new0 + 8 * t)
        valid.append(((klane < WINDOW) & (klane >= tstep + 1)) | (
            (fresh >= 0) & (fresh < 8) & (fresh // dec == tbat) & (fresh % dec <= tstep)))
    tbat_o = tbat[:, 0:LANES]
    wlane = lax.broadcasted_iota(jnp.int32, (KV_WIDTH, WINDOW), 1)
    zpad = jnp.zeros((WINDOW - n_new, KV_WIDTH), F32)
    knew = jnp.concatenate([zpad, z_ref[chunk, K_OFF:K_OFF + KV_WIDTH]], axis=0)
    vnew = jnp.concatenate([zpad, z_ref[chunk, V_OFF:V_OFF + KV_WIDTH]], axis=0)

    st8 = {}

    for t in tiles:
        rows = pl.ds(base + 8 * t, 8)
        q = z_ref[rows, 0:ATTN_WIDTH] * (HEAD_DIM ** -0.5)
        pieces, sinks = [], []
        for hd in range(N_HEADS):
            slab, half, kv = hd // 2, hd % 2, hd // GROUP
            qs = q[:, slab * LANES:(slab + 1) * LANES]
            src = qs if half == kv else pltpu.roll(qs, HEAD_DIM, 1)
            pieces.append(jnp.where(lo if kv == 0 else hi, src, 0.0))
            sinks.append(jnp.full((8, 1), sinks_ref[l, hd], F32))
        st8[t, "lhs"] = jnp.concatenate(pieces, axis=0).astype(BF16)
        st8[t, "sink"] = jnp.concatenate(sinks, axis=0)

    fresh_scores = _dot_t(jnp.concatenate([st8[t, "lhs"] for t in tiles], axis=0), knew.astype(BF16))
    for t in tiles:
        lhs_b = st8.pop((t, "lhs"))
        sc = None
        for i in range(per_tile):
            si = _dot(lhs_b, ck_ref[per_tile * t + i].astype(BF16))
            sc = si if sc is None else jnp.where(tbat_o == i, si, sc)
        sc = jnp.concatenate([sc, fresh_scores[rows64 * t:rows64 * (t + 1)]], axis=1)
        st8[t, "sc"] = jnp.where(valid[t], sc, -jnp.inf)

    for t in tiles:
        sc, sink = st8.pop((t, "sc")), st8.pop((t, "sink"))
        m = jnp.maximum(jnp.max(sc, axis=-1, keepdims=True), sink)
        p = jnp.exp(sc - m)
        st8[t, "denom"] = jnp.sum(p, axis=-1, keepdims=True) + jnp.exp(sink - m)
        st8[t, "pb"] = p.astype(BF16)

    fresh_out = _dot(jnp.concatenate([st8[t, "pb"][:, WINDOW:ext_w] for t in tiles], axis=0), vnew.astype(BF16))
    for t in tiles:
        pb = st8.pop((t, "pb"))
        o = None
        for i in range(per_tile):
            oi = _dot_t(pb[:, 0:WINDOW], cv_ref[per_tile * t + i].astype(BF16))
            o = oi if o is None else jnp.where(tbat_o == i, oi, o)
        st8[t, "o"] = o + fresh_out[rows64 * t:rows64 * (t + 1)]

    for t in tiles:
        rows = pl.ds(base + 8 * t, 8)
        o = st8.pop((t, "o")) / st8.pop((t, "denom"))
        for slab in range(N_HEADS // 2):
            kv = (2 * slab) // GROUP
            even = o[16 * slab:16 * slab + 8]
            odd = o[16 * slab + 8:16 * slab + 16]
            if kv == 0:
                res = jnp.where(lo, even, pltpu.roll(odd, HEAD_DIM, 1))
            else:
                res = jnp.where(lo, pltpu.roll(even, HEAD_DIM, 1), odd)
            mix_ref[rows, slab * LANES:(slab + 1) * LANES] = res

    for new, c_ref, o_ref in ((knew, ck_ref, ko_ref), (vnew, cv_ref, vo_ref)):
        cols = new.T
        for b in range(bb):
            shift = dec * (bb - 1 - b)
            mine = cols if shift == 0 else pltpu.roll(cols, shift, 1)
            kept = pltpu.roll(c_ref[b], WINDOW - dec, 1)
            o_ref[b] = jnp.where(wlane < WINDOW - dec, kept, mine)

    for r in range(POOL_STATE - dec):
        po_ref[r] = st_ref[r + dec]
    for gi, w in enumerate(POOL_WINDOWS):
        cols = slice(gi * POOL_GROUP_WIDTH, (gi + 1) * POOL_GROUP_WIDTH)
        ustage_ref[gi] = z_ref[chunk, U_OFF + gi * POOL_GROUP_WIDTH:U_OFF + (gi + 1) * POOL_GROUP_WIDTH]
        u = [ustage_ref[gi, pl.ds(j, bb, stride=dec), :] for j in range(dec)]
        suffix, acc = {}, None
        for m in range(1, w):
            row = st_ref[POOL_STATE - m, :, cols]
            acc = row if acc is None else acc + row
            suffix[m] = acc
        for j in range(dec):
            po_ref[POOL_STATE - dec + j, :, cols] = u[j]
            tot = suffix.get(w - 1 - j)
            for i in range(max(0, j - w + 1), j + 1):
                tot = u[i] if tot is None else tot + u[i]
            dlt_ref[gi, pl.ds(j, bb, stride=dec), :] = tot / float(w) - u[j]

    for gi in range(len(POOL_WINDOWS)):
        cols = slice(gi * POOL_GROUP_WIDTH, (gi + 1) * POOL_GROUP_WIDTH)
        yp = _dot(dlt_ref[gi].astype(BF16), wpool_ref[gi]) * pscale_ref[:, cols]
        mix_ref[chunk, ATTN_WIDTH + gi * POOL_GROUP_WIDTH:ATTN_WIDTH + (gi + 1) * POOL_GROUP_WIDTH] = yp

    @pl.when(c == pl.num_programs(1) - 1)
    def _():
        sub = n_rows // nsub
        for i in range(nsub):
            rs = slice(i * sub, (i + 1) * sub)
            mixed = _dot(mix_ref[rs, :].astype(BF16), wout_ref[...])
            x = y_ref[rs, :] + _rms(mixed, gmpost_ref[...])
            y_ref[rs, :] = _ffn_rows(x, ps_ref[rs, :], gfpre_ref, gfpost_ref, wg_ref, wu_ref, wd_ref, wple_ref,
                                     wpg_ref)


def _sample_path(xs, ps, sinks, ck, cv, st, gmpre, gmpost, gfpre, gfpost, win, wout, wpool, pscale,
                 wple, ff_weights, *, bb, dec, nsub):
    depth, nb = ck.shape[0], ck.shape[1]
    n = nb * dec
    kern = functools.partial(_sample_kernel, bb=bb, dec=dec, nsub=nsub, depth=depth)
    n_chunks = nb // bb

    def _chunk_spec(w, k):
        rows, cols = w.shape
        assert rows % n_chunks == 0 and (rows // n_chunks) % 16 == 0
        return pl.BlockSpec((rows // n_chunks, cols),
                            lambda l, c: (jnp.where(l == k, c, jnp.where(l > k, n_chunks - 1, 0)), 0))

    cache_spec = pl.BlockSpec((None, bb, KV_WIDTH, WINDOW), lambda l, c: (l, c, 0, 0))
    cache_in_spec = pl.BlockSpec(memory_space=pl.ANY)
    state_spec = pl.BlockSpec((None, POOL_STATE, bb, POOL_WIDTH), lambda l, c: (l, 0, c, 0))
    return pl.pallas_call(
        kern,
        grid=(depth, nb // bb),
        in_specs=[
            pl.BlockSpec(memory_space=pltpu.SMEM),
            pl.BlockSpec((n, D_MODEL), lambda l, c: (0, 0), pipeline_mode=pl.Buffered(1)),
            _layer_spec((n, PLE_DIM)),
            cache_in_spec, cache_in_spec, state_spec,
            _layer_spec((1, D_MODEL)), _layer_spec((1, D_MODEL)), _layer_spec((1, D_MODEL)), _layer_spec((1, D_MODEL)),
            _layer_spec((D_MODEL, IN_WIDTH), single=True),
            _layer_spec((D_MODEL, D_MODEL), single=True),
            _layer_spec((len(POOL_WINDOWS), POOL_GROUP_WIDTH, POOL_GROUP_WIDTH)),
            _layer_spec((1, POOL_WIDTH)),
            _layer_spec((PLE_DIM, D_MODEL), single=True),
        ] + [_chunk_spec(w, k) for k, ws in enumerate(ff_weights) for w in ws],
        out_specs=[
            pl.BlockSpec((n, D_MODEL), lambda l, c: (0, 0)),
            cache_spec, cache_spec, state_spec,
        ],
        out_shape=[
            jax.ShapeDtypeStruct((n, D_MODEL), F32),
            jax.ShapeDtypeStruct((depth, nb, KV_WIDTH, WINDOW), F32),
            jax.ShapeDtypeStruct((depth, nb, KV_WIDTH, WINDOW), F32),
            jax.ShapeDtypeStruct((depth, POOL_STATE, nb, POOL_WIDTH), F32),
        ],
        scratch_shapes=[
            pltpu.VMEM((n, IN_WIDTH), F32),
            pltpu.VMEM((n, D_MODEL), F32),
            pltpu.VMEM((len(POOL_WINDOWS), bb * dec, POOL_GROUP_WIDTH), F32),
            pltpu.VMEM((len(POOL_WINDOWS), bb * dec, POOL_GROUP_WIDTH), F32),
            pltpu.VMEM((D_MODEL, D_FF), BF16),
            pltpu.VMEM((D_MODEL, D_FF), BF16),
            pltpu.VMEM((D_FF, D_MODEL), BF16),
            pltpu.VMEM((D_MODEL, D_MODEL), BF16),
            pltpu.VMEM((CACHE_SLOTS, bb, KV_WIDTH, WINDOW), F32),
            pltpu.VMEM((CACHE_SLOTS, bb, KV_WIDTH, WINDOW), F32),
            pltpu.SemaphoreType.DMA((2, CACHE_SLOTS)),
        ],
        compiler_params=pltpu.CompilerParams(
            dimension_semantics=("arbitrary", "arbitrary"), vmem_limit_bytes=VMEM_LIMIT),
        name="sample_path",
    )(sinks, xs, ps, ck, cv, st, gmpre, gmpost, gfpre, gfpost, win, wout, wpool, pscale, wple,
      *[w for ws in ff_weights for w in ws])


def _cache_from_device_layout(c):
    return c.reshape(c.shape[0], c.shape[1], N_KV_HEADS, HEAD_DIM, WINDOW).transpose(0, 1, 4, 2, 3)


def kernel(x_prompt, x_sample, p_prompt, p_sample, cache_k, cache_v, state_pool, norm_mix_pre, norm_mix_post,
           norm_ffn_pre, norm_ffn_post, w_in, w_out, attn_sinks, w_pool, pool_scale, w_gate, w_up, w_down, w_ple,
           w_ple_gate):
    depth, nbat, seq, _ = p_prompt.shape
    dec_b, dec = x_sample.shape[0], x_sample.shape[1]
    assert 8 % dec == 0 and seq % PROMPT_TILE == 0 and (nbat * seq) % FFN_TILE == 0 and dec_b % SAMPLE_BATCHES == 0
    assert cache_k.shape[2:] == (WINDOW, N_KV_HEADS, HEAD_DIM) and state_pool.shape[2:] == (POOL_STATE, POOL_WIDTH)

    win, wout, wpool, wple = w_in.astype(BF16), w_out.astype(BF16), w_pool.astype(BF16), w_ple.astype(BF16)
    g_mix_pre = norm_mix_pre.reshape(depth, 1, D_MODEL)
    g_mix_post = norm_mix_post.reshape(depth, 1, D_MODEL)
    g_ffn_pre = norm_ffn_pre.reshape(depth, 1, D_MODEL)
    g_ffn_post = norm_ffn_post.reshape(depth, 1, D_MODEL)
    pscale = pool_scale.reshape(depth, 1, POOL_WIDTH)

    pp = p_prompt.reshape(depth, nbat * seq, PLE_DIM)
    ps = p_sample.reshape(depth, dec_b * dec, PLE_DIM)
    ck = cache_k.transpose(0, 1, 3, 4, 2).reshape(depth, dec_b, KV_WIDTH, WINDOW)
    cv = cache_v.transpose(0, 1, 3, 4, 2).reshape(depth, dec_b, KV_WIDTH, WINDOW)

    yp = x_prompt
    kp_l, vp_l, sp_l, ff_weights = [], [], [], []
    for i in range(depth):
        yp, kp, vp, sp, wg, wu, wd, wpg = _mix_prompt(yp, attn_sinks, g_mix_pre, g_mix_post, win, wout, wpool, pscale,
                                                      w_gate, w_up, w_down, w_ple_gate, layer=i, tm=PROMPT_TILE,
                                                      piece=PROMPT_PIECE)
        yp = _ffn(yp.reshape(nbat * seq, D_MODEL), pp, g_ffn_pre, g_ffn_post, wg, wu, wd, wple, wpg,
                  layer=i, tm=FFN_TILE, nsub=FFN_SUBTILES).reshape(nbat, seq, D_MODEL)
        ff_weights.append((wg, wu, wd, wpg))
        kp_l.append(kp)
        vp_l.append(vp)
        sp_l.append(sp[:, POOL_PAD - POOL_STATE:])

    ys, ks, vs, ss = _sample_path(x_sample.reshape(dec_b * dec, D_MODEL), ps, attn_sinks, ck, cv,
                                  state_pool.transpose(0, 2, 1, 3),
                                  g_mix_pre, g_mix_post, g_ffn_pre, g_ffn_post, win, wout, wpool, pscale,
                                  wple, ff_weights, bb=SAMPLE_BATCHES, dec=dec, nsub=SAMPLE_SUBTILES)

    kv_shape = (depth, -1, WINDOW, N_KV_HEADS, HEAD_DIM)
    return (yp, ys.reshape(dec_b, dec, D_MODEL),
            jnp.stack(kp_l).reshape(kv_shape), jnp.stack(vp_l).reshape(kv_shape), jnp.stack(sp_l),
            _cache_from_device_layout(ks), _cache_from_device_layout(vs), ss.transpose(0, 2, 1, 3))
```

```python
import functools

import jax
import jax.numpy as jnp
from jax import lax
from jax.experimental import pallas as pl
from jax.experimental.pallas import tpu as pltpu

D_MODEL = 1024
DEPTH = 4
ATTN_WIDTH = 512
HEAD_DIM = 64
N_HEADS = 8
N_KV_HEADS = 2
GROUP = 4
KV_WIDTH = 128
WINDOW = 128
POOL_WIDTH = 512
POOL_WINDOWS = (2, 4, 8, 16)
POOL_GROUP_WIDTH = 128
POOL_STATE = 15
IN_WIDTH = 1280
D_FF = 2816
PLE_DIM = 256
EPS = 1e-6

K_OFF = ATTN_WIDTH
V_OFF = ATTN_WIDTH + KV_WIDTH
U_OFF = ATTN_WIDTH + 2 * KV_WIDTH
LANES = 128
POOL_PAD = 16
POOL_HEAD = 8 + POOL_PAD
LOG2E = 1.4426950408889634
FF_CHUNKS = ((0, 1024), (1024, 2048), (2048, D_FF))

BF16 = jnp.bfloat16
F32 = jnp.float32
V7X_VMEM_BYTES = 64 * 1024 * 1024
VMEM_LIMIT = V7X_VMEM_BYTES - 8 * 1024 * 1024

PROMPT_TILE = 1024
PROMPT_PIECE = 512
ATTN_GROUP = 4
FFN_TILE = 1024
FFN_SUBTILES = 2
SAMPLE_BATCHES = 8
SAMPLE_SUBTILES = 1
CACHE_SLOTS = 3


def _rms(x, g):
    return x * lax.rsqrt(jnp.mean(x * x, axis=-1, keepdims=True) + EPS) * g


def _sigmoid(x):
    return 1.0 / (1.0 + jnp.exp(-x))


def _dot(a, b):
    return jnp.dot(a, b, preferred_element_type=F32)


def _dot_t(a, b):
    return lax.dot_general(a, b, (((1,), (1,)), ((), ())), preferred_element_type=F32)


def _const_spec(shape, layer=None):
    if layer is None:
        return pl.BlockSpec(shape, lambda *_: (0,) * len(shape), pipeline_mode=pl.Buffered(1))
    return pl.BlockSpec((None,) + shape, lambda *_: (layer,) + (0,) * len(shape),
                        pipeline_mode=pl.Buffered(1))


def _mix_prompt_kernel(sinks_ref, x_ref, gpre_ref, gpost_ref, win_ref, wout_ref, wpool_ref, pscale_ref,
                       wg32_ref, wu32_ref, wd32_ref, wpg32_ref,
                       y_ref, klast_ref, vlast_ref, plast_ref, wg16_ref, wu16_ref, wd16_ref, wpg16_ref,
                       z_ref, kext_ref, vt_ref, uext_ref, ps1_ref, ps2_ref, mix_ref, bias_ref, *, tm, piece, layer):
    s = pl.program_id(1)
    nblk = tm // WINDOW

    def cast_weights():
        for w32_ref, w16_ref in ((wg32_ref, wg16_ref), (wu32_ref, wu16_ref), (wd32_ref, wd16_ref),
                                 (wpg32_ref, wpg16_ref)):
            w16_ref[...] = w32_ref[...].astype(BF16)

    @pl.when((pl.program_id(0) == 0) & (s == 0))
    def _():
        c = lax.broadcasted_iota(jnp.int32, (2 * WINDOW, WINDOW), 0)
        r = lax.broadcasted_iota(jnp.int32, (2 * WINDOW, WINDOW), 1)
        dist = r + WINDOW - c
        ok = (dist >= 0) & (dist < WINDOW)
        bias_ref[0] = jnp.where(ok, 0.0, -jnp.inf)
        bias_ref[1] = jnp.where(ok & (c >= WINDOW), 0.0, -jnp.inf)

    @pl.when(s == 0)
    def _():
        kext_ref[0:WINDOW, :] = jnp.zeros((WINDOW, KV_WIDTH), F32)
        vt_ref[:, 0:WINDOW] = jnp.zeros((KV_WIDTH, WINDOW), F32)
        uext_ref[0:POOL_HEAD, :] = jnp.zeros((POOL_HEAD, POOL_WIDTH), F32)
        ps1_ref[0:8, :] = jnp.zeros((8, POOL_GROUP_WIDTH), F32)
        ps2_ref[0:8, :] = jnp.zeros((8, POOL_GROUP_WIDTH), F32)

    lane = lax.broadcasted_iota(jnp.int32, (WINDOW, LANES), 1)
    lo = lane < HEAD_DIM
    lo2 = jnp.concatenate([lo, lo], axis=0)
    row8 = lax.broadcasted_iota(jnp.int32, (8, GROUP * WINDOW), 0)
    kcol = lax.broadcasted_iota(jnp.int32, (HEAD_DIM, 2 * WINDOW), 1)
    ones = jnp.ones((HEAD_DIM, 2 * WINDOW), F32)
    half = piece

    def project_in(hf):
        hr = slice(hf * half, (hf + 1) * half)
        h = _rms(x_ref[0, hr, :], gpre_ref[...]).astype(BF16)
        z_ref[hr, :] = _dot(h, win_ref[...])
        uext_ref[POOL_HEAD + hf * half:POOL_HEAD + (hf + 1) * half, :] = z_ref[hr, U_OFF:U_OFF + POOL_WIDTH]
        kext_ref[WINDOW + hf * half:WINDOW + (hf + 1) * half, :] = z_ref[hr, K_OFF:K_OFF + KV_WIDTH]
        for n in range(hf * half // WINDOW, (hf + 1) * half // WINDOW):
            r0 = n * WINDOW
            vt_ref[:, WINDOW + r0:2 * WINDOW + r0] = z_ref[r0:r0 + WINDOW, V_OFF:V_OFF + KV_WIDTH].T

    def attend_stages(blocks):
        st8 = {}

        def scores():
            for n in blocks:
                r0 = n * WINDOW
                rows = slice(r0, r0 + WINDOW)
                kcat = kext_ref[r0:r0 + 2 * WINDOW, :]
                kswap = pltpu.roll(kcat, HEAD_DIM, 1)
                bias1 = bias_ref[jnp.where(s == 0, 1, 0)] if n == 0 else bias_ref[0]
                bias = jnp.concatenate([bias1] * GROUP, axis=1)
                for kv in range(N_KV_HEADS):
                    kk = (jnp.where(lo2, kcat, kswap) if kv == 0 else jnp.where(lo2, kswap, kcat)).astype(BF16)
                    pieces, sinks = [], []
                    for g in range(GROUP):
                        hd = kv * GROUP + g
                        slab = hd // 2
                        qs = z_ref[rows, slab * LANES:(slab + 1) * LANES] * (HEAD_DIM ** -0.5 * LOG2E)
                        keep = lo if hd % 2 == 0 else jnp.logical_not(lo)
                        pieces.append(jnp.where(keep, qs, 0.0).astype(BF16))
                        sinks.append(jnp.full((1, WINDOW), sinks_ref[layer, hd] * LOG2E, F32))
                    q4 = jnp.concatenate(pieces, axis=0)
                    st8[n, kv, "sink"] = jnp.concatenate(sinks, axis=1)
                    st8[n, kv, "st"] = _dot_t(kk, q4) + bias

        def softmax():
            for n in blocks:
                for kv in range(N_KV_HEADS):
                    st, sink = st8.pop((n, kv, "st")), st8.pop((n, kv, "sink"))
                    m = jnp.maximum(jnp.max(st, axis=0, keepdims=True), sink)
                    p = jnp.exp2(st - m)
                    top = jnp.where(row8 == 0, jnp.exp2(sink - m), p[0:8])
                    st8[n, kv, "p"] = jnp.concatenate([top, p[8:]], axis=0).astype(BF16)

        def values():
            for n in blocks:
                r0 = n * WINDOW
                for kv in range(N_KV_HEADS):
                    vth = vt_ref[kv * HEAD_DIM:(kv + 1) * HEAD_DIM, r0:r0 + 2 * WINDOW]
                    a = jnp.concatenate([jnp.where(kcol == 0, 0.0, vth), ones], axis=0).astype(BF16)
                    st8[n, kv, "ot"] = _dot(a, st8.pop((n, kv, "p")))

        def store():
            for n in blocks:
                rows = slice(n * WINDOW, (n + 1) * WINDOW)
                for kv in range(N_KV_HEADS):
                    ot = st8.pop((n, kv, "ot"))
                    inv = 1.0 / ot[HEAD_DIM:HEAD_DIM + 8]
                    on = ot[0:HEAD_DIM] * jnp.concatenate([inv] * (HEAD_DIM // 8), axis=0)
                    for j in range(GROUP // 2):
                        c0 = 2 * j * WINDOW
                        pair = jnp.concatenate([on[:, c0:c0 + WINDOW], on[:, c0 + WINDOW:c0 + 2 * WINDOW]], axis=0)
                        slab = kv * (GROUP // 2) + j
                        mix_ref[rows, slab * LANES:(slab + 1) * LANES] = pair.T.astype(BF16)

        return scores, softmax, values, store

    def pool(hf):
        t0 = hf * half
        pos = lax.broadcasted_iota(jnp.int32, (half, 1), 0) + (s * tm + t0)
        first = POOL_HEAD + t0
        ext = slice(first - POOL_PAD, first + half)
        for gi, w in enumerate(POOL_WINDOWS):
            cols = slice(gi * POOL_GROUP_WIDTH, (gi + 1) * POOL_GROUP_WIDTH)
            src, src_cols, span, bufs = uext_ref, cols, 1, [ps1_ref, ps2_ref]
            while 2 * span < w:
                dst = bufs[0]
                dst[ext, :] = src[ext, src_cols] + src[first - POOL_PAD - span:first + half - span, src_cols]
                src, src_cols, span, bufs = dst, slice(None), 2 * span, bufs[::-1]
            tot = src[first:first + half, src_cols] + src[first - span:first - span + half, src_cols]
            cnt = jnp.minimum(w, pos + 1).astype(F32)
            dlt = (tot / cnt - uext_ref[first:first + half, cols]).astype(BF16)
            yp = _dot(dlt, wpool_ref[gi]) * pscale_ref[:, cols]
            mix_ref[t0:t0 + half, ATTN_WIDTH + gi * POOL_GROUP_WIDTH:ATTN_WIDTH + (gi + 1) * POOL_GROUP_WIDTH] = (
                yp.astype(BF16))

    def project_out(hf):
        hr = slice(hf * half, (hf + 1) * half)
        mixed = _dot(mix_ref[hr, :], wout_ref[...])
        y_ref[0, hr, :] = x_ref[0, hr, :] + _rms(mixed, gpost_ref[...])

    per_half = half // WINDOW
    n_piece = tm // half
    project_in(0)
    for hf in range(n_piece):
        blocks = list(range(hf * per_half, (hf + 1) * per_half))
        groups = [blocks[i:i + ATTN_GROUP] for i in range(0, len(blocks), ATTN_GROUP)]
        for gi, group in enumerate(groups):
            scores, softmax, values, store = attend_stages(group)
            scores()
            if gi == 0 and hf + 1 < n_piece:
                project_in(hf + 1)
            if gi == 0 and hf == 0:
                cast_weights()
            if gi == len(groups) - 1 and hf > 0:
                project_out(hf - 1)
            softmax()
            values()
            store()
        pool(hf)
    project_out(n_piece - 1)

    kext_ref[0:WINDOW, :] = kext_ref[tm:tm + WINDOW, :]
    vt_ref[:, 0:WINDOW] = vt_ref[:, tm:tm + WINDOW]
    uext_ref[POOL_HEAD - POOL_PAD:POOL_HEAD, :] = uext_ref[POOL_HEAD + tm - POOL_PAD:POOL_HEAD + tm, :]

    @pl.when(s == pl.num_programs(1) - 1)
    def _():
        klast_ref[0] = z_ref[tm - WINDOW:tm, K_OFF:K_OFF + KV_WIDTH]
        vlast_ref[0] = z_ref[tm - WINDOW:tm, V_OFF:V_OFF + KV_WIDTH]
        plast_ref[0] = uext_ref[POOL_HEAD + tm - POOL_PAD:POOL_HEAD + tm, :]


def _row_chunk_specs(shape, n_steps, layer, step_of):
    rows, cols = shape
    chunk = rows // n_steps
    assert chunk * n_steps == rows and chunk % 16 == 0, (shape, n_steps)
    return (pl.BlockSpec((None, chunk, cols), lambda *g: (layer, step_of(*g), 0)),
            pl.BlockSpec((chunk, cols), lambda *g: (step_of(*g), 0)))


def _mix_prompt(x, sinks, gpre, gpost, win, wout, wpool, pscale, wg32, wu32, wd32, wpg32, *, layer, tm, piece):
    b, s, _ = x.shape
    kern = functools.partial(_mix_prompt_kernel, tm=tm, piece=piece, layer=layer)
    n_tiles = s // tm
    cast_shapes = ((D_MODEL, D_FF), (D_MODEL, D_FF), (D_FF, D_MODEL), (D_MODEL, D_MODEL))
    cast_specs = [_row_chunk_specs(shape, b * n_tiles, layer, lambda i, j: i * n_tiles + j) for shape in cast_shapes]
    return pl.pallas_call(
        kern,
        grid=(b, n_tiles),
        in_specs=[
            pl.BlockSpec(memory_space=pltpu.SMEM),
            pl.BlockSpec((1, tm, D_MODEL), lambda i, j: (i, j, 0)),
            _const_spec((1, D_MODEL), layer),
            _const_spec((1, D_MODEL), layer),
            _const_spec((D_MODEL, IN_WIDTH), layer),
            _const_spec((D_MODEL, D_MODEL), layer),
            _const_spec((len(POOL_WINDOWS), POOL_GROUP_WIDTH, POOL_GROUP_WIDTH), layer),
            _const_spec((1, POOL_WIDTH), layer),
        ] + [spec_in for spec_in, _ in cast_specs],
        out_specs=[
            pl.BlockSpec((1, tm, D_MODEL), lambda i, j: (i, j, 0)),
            pl.BlockSpec((1, WINDOW, KV_WIDTH), lambda i, j: (i, 0, 0)),
            pl.BlockSpec((1, WINDOW, KV_WIDTH), lambda i, j: (i, 0, 0)),
            pl.BlockSpec((1, POOL_PAD, POOL_WIDTH), lambda i, j: (i, 0, 0)),
        ] + [spec_out for _, spec_out in cast_specs],
        out_shape=[
            jax.ShapeDtypeStruct((b, s, D_MODEL), F32),
            jax.ShapeDtypeStruct((b, WINDOW, KV_WIDTH), F32),
            jax.ShapeDtypeStruct((b, WINDOW, KV_WIDTH), F32),
            jax.ShapeDtypeStruct((b, POOL_PAD, POOL_WIDTH), F32),
        ] + [jax.ShapeDtypeStruct(shape, BF16) for shape in cast_shapes],
        scratch_shapes=[
            pltpu.VMEM((tm, IN_WIDTH), F32),
            pltpu.VMEM((WINDOW + tm, KV_WIDTH), F32),
            pltpu.VMEM((KV_WIDTH, WINDOW + tm), F32),
            pltpu.VMEM((POOL_HEAD + tm, POOL_WIDTH), F32),
            pltpu.VMEM((POOL_HEAD + tm, POOL_GROUP_WIDTH), F32),
            pltpu.VMEM((POOL_HEAD + tm, POOL_GROUP_WIDTH), F32),
            pltpu.VMEM((tm, D_MODEL), BF16),
            pltpu.VMEM((2, 2 * WINDOW, WINDOW), F32),
        ],
        compiler_params=pltpu.CompilerParams(
            dimension_semantics=("arbitrary", "arbitrary"), vmem_limit_bytes=VMEM_LIMIT),
        name=f"mix_prompt_{layer}",
    )(sinks, x, gpre, gpost, win, wout, wpool, pscale, wg32, wu32, wd32, wpg32)


def _swiglu(f, wg_ref, wu_ref, wd_ref):
    d, pending = None, None
    for c0, c1 in FF_CHUNKS:
        g = _dot(f, wg_ref[:, c0:c1])
        u = _dot(f, wu_ref[:, c0:c1])
        if pending is not None:
            part = _dot(pending[0], wd_ref[pending[1]:pending[2], :])
            d = part if d is None else d + part
        pending = ((g * _sigmoid(g) * u).astype(BF16), c0, c1)
    return d + _dot(pending[0], wd_ref[pending[1]:pending[2], :])


def _ffn_rows(x, p, gpre_ref, gpost_ref, wg_ref, wu_ref, wd_ref, wple_ref, wpg_ref):
    f = _rms(x, gpre_ref[...]).astype(BF16)
    x = x + _rms(_swiglu(f, wg_ref, wu_ref, wd_ref), gpost_ref[...])
    gate = _sigmoid(_dot(x.astype(BF16), wpg_ref[...]))
    return x + gate * _dot(p.astype(BF16), wple_ref[...])


def _ffn_kernel(x_ref, p_ref, gpre_ref, gpost_ref, wg_ref, wu_ref, wd_ref, wple_ref, wpg_ref, o_ref, *, nsub):
    sub = x_ref.shape[0] // nsub
    for i in range(nsub):
        rows = slice(i * sub, (i + 1) * sub)
        o_ref[rows, :] = _ffn_rows(x_ref[rows, :], p_ref[rows, :], gpre_ref, gpost_ref, wg_ref, wu_ref, wd_ref,
                                   wple_ref, wpg_ref)


def _ffn(x, p, gpre, gpost, wg, wu, wd, wple, wpg, *, layer, tm, nsub):
    n = x.shape[0]
    return pl.pallas_call(
        functools.partial(_ffn_kernel, nsub=nsub),
        grid=(n // tm,),
        in_specs=[
            pl.BlockSpec((tm, D_MODEL), lambda i: (i, 0)),
            pl.BlockSpec((None, tm, PLE_DIM), lambda i: (layer, i, 0)),
            _const_spec((1, D_MODEL), layer),
            _const_spec((1, D_MODEL), layer),
            _const_spec((D_MODEL, D_FF)),
            _const_spec((D_MODEL, D_FF)),
            _const_spec((D_FF, D_MODEL)),
            _const_spec((PLE_DIM, D_MODEL), layer),
            _const_spec((D_MODEL, D_MODEL)),
        ],
        out_specs=pl.BlockSpec((tm, D_MODEL), lambda i: (i, 0)),
        out_shape=jax.ShapeDtypeStruct((n, D_MODEL), F32),
        compiler_params=pltpu.CompilerParams(
            dimension_semantics=("arbitrary",), vmem_limit_bytes=VMEM_LIMIT),
        name=f"ffn_prompt_{layer}",
    )(x, p, gpre, gpost, wg, wu, wd, wple, wpg)


def _layer_spec(shape, single=False):
    kwargs = dict(pipeline_mode=pl.Buffered(1)) if single else {}
    return pl.BlockSpec((None,) + shape, lambda l, c: (l,) + (0,) * len(shape), **kwargs)


def _sample_kernel(sinks_ref, xs_ref, ps_ref, ck_hbm, cv_hbm, st_ref, gmpre_ref, gmpost_ref, gfpre_ref, gfpost_ref,
                   win_ref, wout_ref, wpool_ref, pscale_ref, wple_ref, *refs, bb, dec, nsub, depth):
    chunk_refs = refs[:4 * depth]
    (y_ref, ko_ref, vo_ref, po_ref, z_ref, mix_ref, ustage_ref, dlt_ref,
     wg_ref, wu_ref, wd_ref, wpg_ref, ckbuf_ref, cvbuf_ref, cache_sem) = refs[4 * depth:]
    l = pl.program_id(0)
    c = pl.program_id(1)
    n_rows = y_ref.shape[0]

    n_chunk = pl.num_programs(1)
    g = l * n_chunk + c
    n_steps = pl.num_programs(0) * n_chunk

    def cache_copies(step):
        slot = step % CACHE_SLOTS
        src = (step // n_chunk, pl.ds((step % n_chunk) * bb, bb))
        return (pltpu.make_async_copy(ck_hbm.at[src], ckbuf_ref.at[slot], cache_sem.at[0, slot]),
                pltpu.make_async_copy(cv_hbm.at[src], cvbuf_ref.at[slot], cache_sem.at[1, slot]))

    @pl.when(g == 0)
    def _():
        for ahead in range(CACHE_SLOTS - 1):
            for prio, copy in enumerate(cache_copies(g + ahead)):
                copy.start(priority=prio)

    @pl.when(g + (CACHE_SLOTS - 1) < n_steps)
    def _():
        for prio, copy in enumerate(cache_copies(g + (CACHE_SLOTS - 1))):
            copy.start(priority=prio)

    for copy in cache_copies(g):
        copy.wait()
    ck_ref = ckbuf_ref.at[g % CACHE_SLOTS]
    cv_ref = cvbuf_ref.at[g % CACHE_SLOTS]

    @pl.when((l == 0) & (c == 0))
    def _():
        y_ref[...] = xs_ref[...]

    @pl.when(c == 0)
    def _():
        h = _rms(y_ref[...], gmpre_ref[...]).astype(BF16)
        z_ref[...] = _dot(h, win_ref[...])

    for k in range(depth):
        @pl.when(l == k)
        def _(k=k):
            for src_ref, dst_ref in zip(chunk_refs[4 * k:4 * k + 4], (wg_ref, wu_ref, wd_ref, wpg_ref)):
                rows_w = src_ref.shape[0]
                dst_ref[pl.ds(pl.multiple_of(c * rows_w, 16), rows_w), :] = src_ref[...]

    per_tile = 8 // dec
    base = pl.multiple_of(c * (bb * dec), 8)
    lane = lax.broadcasted_iota(jnp.int32, (8, LANES), 1)
    lo = lane < HEAD_DIM
    hi = jnp.logical_not(lo)
    rows64 = N_HEADS * 8
    ext_w = 2 * WINDOW
    n_new = bb * dec
    new0 = ext_w - n_new
    chunk = pl.ds(base, n_new)
    tiles = range(bb // per_tile)
    trow = lax.broadcasted_iota(jnp.int32, (rows64, ext_w), 0) & 7
    tstep = trow % dec
    tbat = trow // dec
    klane = lax.broadcasted_iota(jnp.int32, (rows64, ext_w), 1)
    valid = []
    for t in tiles:
        fresh = klane - (new0 + 8 * t)
        valid.append(((klane < WINDOW) & (klane >= tstep + 1)) | (
            (fresh >= 0) & (fresh < 8) & (fresh // dec == tbat) & (fresh % dec <= tstep)))
    tbat_o = tbat[:, 0:LANES]
    wlane = lax.broadcasted_iota(jnp.int32, (KV_WIDTH, WINDOW), 1)
    zpad = jnp.zeros((WINDOW - n_new, KV_WIDTH), F32)
    knew = jnp.concatenate([zpad, z_ref[chunk, K_OFF:K_OFF + KV_WIDTH]], axis=0)
    vnew = jnp.concatenate([zpad, z_ref[chunk, V_OFF:V_OFF + KV_WIDTH]], axis=0)

    st8 = {}

    for t in tiles:
        rows = pl.ds(base + 8 * t, 8)
        q = z_ref[rows, 0:ATTN_WIDTH] * (HEAD_DIM ** -0.5)
        pieces, sinks = [], []
        for hd in range(N_HEADS):
            slab, half, kv = hd // 2, hd % 2, hd // GROUP
            qs = q[:, slab * LANES:(slab + 1) * LANES]
            src = qs if half == kv else pltpu.roll(qs, HEAD_DIM, 1)
            pieces.append(jnp.where(lo if kv == 0 else hi, src, 0.0))
            sinks.append(jnp.full((8, 1), sinks_ref[l, hd], F32))
        st8[t, "lhs"] = jnp.concatenate(pieces, axis=0).astype(BF16)
        st8[t, "sink"] = jnp.concatenate(sinks, axis=0)

    fresh_scores = _dot_t(jnp.concatenate([st8[t, "lhs"] for t in tiles], axis=0), knew.astype(BF16))
    for t in tiles:
        lhs_b = st8.pop((t, "lhs"))
        sc = None
        for i in range(per_tile):
            si = _dot(lhs_b, ck_ref[per_tile * t + i].astype(BF16))
            sc = si if sc is None else jnp.where(tbat_o == i, si, sc)
        sc = jnp.concatenate([sc, fresh_scores[rows64 * t:rows64 * (t + 1)]], axis=1)
        st8[t, "sc"] = jnp.where(valid[t], sc, -jnp.inf)

    for t in tiles:
        sc, sink = st8.pop((t, "sc")), st8.pop((t, "sink"))
        m = jnp.maximum(jnp.max(sc, axis=-1, keepdims=True), sink)
        p = jnp.exp(sc - m)
        st8[t, "denom"] = jnp.sum(p, axis=-1, keepdims=True) + jnp.exp(sink - m)
        st8[t, "pb"] = p.astype(BF16)

    fresh_out = _dot(jnp.concatenate([st8[t, "pb"][:, WINDOW:ext_w] for t in tiles], axis=0), vnew.astype(BF16))
    for t in tiles:
        pb = st8.pop((t, "pb"))
        o = None
        for i in range(per_tile):
            oi = _dot_t(pb[:, 0:WINDOW], cv_ref[per_tile * t + i].astype(BF16))
            o = oi if o is None else jnp.where(tbat_o == i, oi, o)
        st8[t, "o"] = o + fresh_out[rows64 * t:rows64 * (t + 1)]

    for t in tiles:
        rows = pl.ds(base + 8 * t, 8)
        o = st8.pop((t, "o")) / st8.pop((t, "denom"))
        for slab in range(N_HEADS // 2):
            kv = (2 * slab) // GROUP
            even = o[16 * slab:16 * slab + 8]
            odd = o[16 * slab + 8:16 * slab + 16]
            if kv == 0:
                res = jnp.where(lo, even, pltpu.roll(odd, HEAD_DIM, 1))
            else:
                res = jnp.where(lo, pltpu.roll(even, HEAD_DIM, 1), odd)
            mix_ref[rows, slab * LANES:(slab + 1) * LANES] = res

    for new, c_ref, o_ref in ((knew, ck_ref, ko_ref), (vnew, cv_ref, vo_ref)):
        cols = new.T
        for b in range(bb):
            shift = dec * (bb - 1 - b)
            mine = cols if shift == 0 else pltpu.roll(cols, shift, 1)
            kept = pltpu.roll(c_ref[b], WINDOW - dec, 1)
            o_ref[b] = jnp.where(wlane < WINDOW - dec, kept, mine)

    for r in range(POOL_STATE - dec):
        po_ref[r] = st_ref[r + dec]
    for gi, w in enumerate(POOL_WINDOWS):
        cols = slice(gi * POOL_GROUP_WIDTH, (gi + 1) * POOL_GROUP_WIDTH)
        ustage_ref[gi] = z_ref[chunk, U_OFF + gi * POOL_GROUP_WIDTH:U_OFF + (gi + 1) * POOL_GROUP_WIDTH]
        u = [ustage_ref[gi, pl.ds(j, bb, stride=dec), :] for j in range(dec)]
        suffix, acc = {}, None
        for m in range(1, w):
            row = st_ref[POOL_STATE - m, :, cols]
            acc = row if acc is None else acc + row
            suffix[m] = acc
        for j in range(dec):
            po_ref[POOL_STATE - dec + j, :, cols] = u[j]
            tot = suffix.get(w - 1 - j)
            for i in range(max(0, j - w + 1), j + 1):
                tot = u[i] if tot is None else tot + u[i]
            dlt_ref[gi, pl.ds(j, bb, stride=dec), :] = tot / float(w) - u[j]

    for gi in range(len(POOL_WINDOWS)):
        cols = slice(gi * POOL_GROUP_WIDTH, (gi + 1) * POOL_GROUP_WIDTH)
        yp = _dot(dlt_ref[gi].astype(BF16), wpool_ref[gi]) * pscale_ref[:, cols]
        mix_ref[chunk, ATTN_WIDTH + gi * POOL_GROUP_WIDTH:ATTN_WIDTH + (gi + 1) * POOL_GROUP_WIDTH] = yp

    @pl.when(c == pl.num_programs(1) - 1)
    def _():
        sub = n_rows // nsub
        for i in range(nsub):
            rs = slice(i * sub, (i + 1) * sub)
            mixed = _dot(mix_ref[rs, :].astype(BF16), wout_ref[...])
            x = y_ref[rs, :] + _rms(mixed, gmpost_ref[...])
            y_ref[rs, :] = _ffn_rows(x, ps_ref[rs, :], gfpre_ref, gfpost_ref, wg_ref, wu_ref, wd_ref, wple_ref,
                                     wpg_ref)


def _sample_path(xs, ps, sinks, ck, cv, st, gmpre, gmpost, gfpre, gfpost, win, wout, wpool, pscale,
                 wple, ff_weights, *, bb, dec, nsub):
    depth, nb = ck.shape[0], ck.shape[1]
    n = nb * dec
    kern = functools.partial(_sample_kernel, bb=bb, dec=dec, nsub=nsub, depth=depth)
    n_chunks = nb // bb

    def _chunk_spec(w, k):
        rows, cols = w.shape
        assert rows % n_chunks == 0 and (rows // n_chunks) % 16 == 0
        return pl.BlockSpec((rows // n_chunks, cols),
                            lambda l, c: (jnp.where(l == k, c, jnp.where(l > k, n_chunks - 1, 0)), 0))

    cache_spec = pl.BlockSpec((None, bb, KV_WIDTH, WINDOW), lambda l, c: (l, c, 0, 0))
    cache_in_spec = pl.BlockSpec(memory_space=pl.ANY)
    state_spec = pl.BlockSpec((None, POOL_STATE, bb, POOL_WIDTH), lambda l, c: (l, 0, c, 0))
    return pl.pallas_call(
        kern,
        grid=(depth, nb // bb),
        in_specs=[
            pl.BlockSpec(memory_space=pltpu.SMEM),
            pl.BlockSpec((n, D_MODEL), lambda l, c: (0, 0), pipeline_mode=pl.Buffered(1)),
            _layer_spec((n, PLE_DIM)),
            cache_in_spec, cache_in_spec, state_spec,
            _layer_spec((1, D_MODEL)), _layer_spec((1, D_MODEL)), _layer_spec((1, D_MODEL)), _layer_spec((1, D_MODEL)),
            _layer_spec((D_MODEL, IN_WIDTH), single=True),
            _layer_spec((D_MODEL, D_MODEL), single=True),
            _layer_spec((len(POOL_WINDOWS), POOL_GROUP_WIDTH, POOL_GROUP_WIDTH)),
            _layer_spec((1, POOL_WIDTH)),
            _layer_spec((PLE_DIM, D_MODEL), single=True),
        ] + [_chunk_spec(w, k) for k, ws in enumerate(ff_weights) for w in ws],
        out_specs=[
            pl.BlockSpec((n, D_MODEL), lambda l, c: (0, 0)),
            cache_spec, cache_spec, state_spec,
        ],
        out_shape=[
            jax.ShapeDtypeStruct((n, D_MODEL), F32),
            jax.ShapeDtypeStruct((depth, nb, KV_WIDTH, WINDOW), F32),
            jax.ShapeDtypeStruct((depth, nb, KV_WIDTH, WINDOW), F32),
            jax.ShapeDtypeStruct((depth, POOL_STATE, nb, POOL_WIDTH), F32),
        ],
        scratch_shapes=[
            pltpu.VMEM((n, IN_WIDTH), F32),
            pltpu.VMEM((n, D_MODEL), F32),
            pltpu.VMEM((len(POOL_WINDOWS), bb * dec, POOL_GROUP_WIDTH), F32),
            pltpu.VMEM((len(POOL_WINDOWS), bb * dec, POOL_GROUP_WIDTH), F32),
            pltpu.VMEM((D_MODEL, D_FF), BF16),
            pltpu.VMEM((D_MODEL, D_FF), BF16),
            pltpu.VMEM((D_FF, D_MODEL), BF16),
            pltpu.VMEM((D_MODEL, D_MODEL), BF16),
            pltpu.VMEM((CACHE_SLOTS, bb, KV_WIDTH, WINDOW), F32),
            pltpu.VMEM((CACHE_SLOTS, bb, KV_WIDTH, WINDOW), F32),
            pltpu.SemaphoreType.DMA((2, CACHE_SLOTS)),
        ],
        compiler_params=pltpu.CompilerParams(
            dimension_semantics=("arbitrary", "arbitrary"), vmem_limit_bytes=VMEM_LIMIT),
        name="sample_path",
    )(sinks, xs, ps, ck, cv, st, gmpre, gmpost, gfpre, gfpost, win, wout, wpool, pscale, wple,
      *[w for ws in ff_weights for w in ws])


def _cache_from_device_layout(c):
    return c.reshape(c.shape[0], c.shape[1], N_KV_HEADS, HEAD_DIM, WINDOW).transpose(0, 1, 4, 2, 3)


def kernel(x_prompt, x_sample, p_prompt, p_sample, cache_k, cache_v, state_pool, norm_mix_pre, norm_mix_post,
           norm_ffn_pre, norm_ffn_post, w_in, w_out, attn_sinks, w_pool, pool_scale, w_gate, w_up, w_down, w_ple,
           w_ple_gate):
    depth, nbat, seq, _ = p_prompt.shape
    dec_b, dec = x_sample.shape[0], x_sample.shape[1]
    assert 8 % dec == 0 and seq % PROMPT_TILE == 0 and (nbat * seq) % FFN_TILE == 0 and dec_b % SAMPLE_BATCHES == 0
    assert cache_k.shape[2:] == (WINDOW, N_KV_HEADS, HEAD_DIM) and state_pool.shape[2:] == (POOL_STATE, POOL_WIDTH)

    win, wout, wpool, wple = w_in.astype(BF16), w_out.astype(BF16), w_pool.astype(BF16), w_ple.astype(BF16)
    g_mix_pre = norm_mix_pre.reshape(depth, 1, D_MODEL)
    g_mix_post = norm_mix_post.reshape(depth, 1, D_MODEL)
    g_ffn_pre = norm_ffn_pre.reshape(depth, 1, D_MODEL)
    g_ffn_post = norm_ffn_post.reshape(depth, 1, D_MODEL)
    pscale = pool_scale.reshape(depth, 1, POOL_WIDTH)

    pp = p_prompt.reshape(depth, nbat * seq, PLE_DIM)
    ps = p_sample.reshape(depth, dec_b * dec, PLE_DIM)
    ck = cache_k.transpose(0, 1, 3, 4, 2).reshape(depth, dec_b, KV_WIDTH, WINDOW)
    cv = cache_v.transpose(0, 1, 3, 4, 2).reshape(depth, dec_b, KV_WIDTH, WINDOW)

    yp = x_prompt
    kp_l, vp_l, sp_l, ff_weights = [], [], [], []
    for i in range(depth):
        yp, kp, vp, sp, wg, wu, wd, wpg = _mix_prompt(yp, attn_sinks, g_mix_pre, g_mix_post, win, wout, wpool, pscale,
                                                      w_gate, w_up, w_down, w_ple_gate, layer=i, tm=PROMPT_TILE,
                                                      piece=PROMPT_PIECE)
        yp = _ffn(yp.reshape(nbat * seq, D_MODEL), pp, g_ffn_pre, g_ffn_post, wg, wu, wd, wple, wpg,
                  layer=i, tm=FFN_TILE, nsub=FFN_SUBTILES).reshape(nbat, seq, D_MODEL)
        ff_weights.append((wg, wu, wd, wpg))
        kp_l.append(kp)
        vp_l.append(vp)
        sp_l.append(sp[:, POOL_PAD - POOL_STATE:])

    ys, ks, vs, ss = _sample_path(x_sample.reshape(dec_b * dec, D_MODEL), ps, attn_sinks, ck, cv,
                                  state_pool.transpose(0, 2, 1, 3),
                                  g_mix_pre, g_mix_post, g_ffn_pre, g_ffn_post, win, wout, wpool, pscale,
                                  wple, ff_weights, bb=SAMPLE_BATCHES, dec=dec, nsub=SAMPLE_SUBTILES)

    kv_shape = (depth, -1, WINDOW, N_KV_HEADS, HEAD_DIM)
    return (yp, ys.reshape(dec_b, dec, D_MODEL),
            jnp.stack(kp_l).reshape(kv_shape), jnp.stack(vp_l).reshape(kv_shape), jnp.stack(sp_l),
            _cache_from_device_layout(ks), _cache_from_device_layout(vs), ss.transpose(0, 2, 1, 3))
```
